```python
import functools
import jax, jax.numpy as jnp
from jax import lax
import numpy as np

D_MODEL = 1024
BATCH = 8
SEQ = 2048
DEPTH = 2
DEC_BATCH = 128
DEC_SEQ = 4
PAST_LEN = 8192
PAGE_SIZE = 128

BR_W = D_MODEL // 2
N_BRANCH = 4
CONV_K = 3
HG_HEADS = 4
HG_DK = BR_W // HG_HEADS
HG_CHUNK = 64
MLA_HEADS = 4
MLA_NOPE = 64
MLA_ROPE = 32
MLA_QK = MLA_NOPE + MLA_ROPE
MLA_V = BR_W // MLA_HEADS
MLA_Q_RANK = 192
MLA_KV_RANK = 128
MLA_SCALE = MLA_QK ** -0.5
ROPE_THETA = 10000.0
Q_BLOCK = 128
MEM_LEN = 256
MEM_HEADS = 4
MEM_DH = BR_W // MEM_HEADS
MEM_SCALE = MEM_DH ** -0.5
EPS = 1e-6
NEG = -1e30
IN_SIZES = (BR_W, BR_W, BR_W,
            BR_W, BR_W, BR_W,
            MLA_Q_RANK, MLA_KV_RANK, MLA_ROPE,
            BR_W,
            N_BRANCH * BR_W,
            N_BRANCH * D_MODEL)
N_IN = sum(IN_SIZES)

kernel_name = 'hybrid_conv_hgrn2_mla_mem_step'


def rmsnorm(x, g):
    xf = x.astype(jnp.float32)
    y = xf * lax.rsqrt(jnp.mean(xf * xf, axis=-1, keepdims=True) + EPS)
    return (y * g).astype(x.dtype)


def rope(x, pos):
    half = MLA_ROPE // 2
    freqs = ROPE_THETA ** (-jnp.arange(half, dtype=jnp.float32) / half)
    ang = pos.astype(jnp.float32)[:, None] * freqs[None, :]
    ang = ang.reshape((pos.shape[0],) + (1,) * (x.ndim - 3) + (half,))
    cos, sin = jnp.cos(ang), jnp.sin(ang)
    xf = x.astype(jnp.float32)
    x1, x2 = xf[..., :half], xf[..., half:]
    return jnp.concatenate([x1 * cos - x2 * sin, x1 * sin + x2 * cos], -1).astype(x.dtype)


def hgrn_lower_bounds(lb_param):
    p = jax.nn.softmax(lb_param.astype(jnp.float32), axis=0)
    return jnp.cumsum(p, axis=0) - p[0]


def hgrn_forget(z, lb):
    f = lb + (1.0 - lb) * jax.nn.sigmoid(z.astype(jnp.float32))
    return jnp.log(f), 1.0 - f


def hgrn2_chunked(q, k, v, logf, S0):
    B, T, H, DK = q.shape
    DV = v.shape[-1]
    L = min(HG_CHUNK, T)
    n = -(-T // L)
    pad = n * L - T

    def chunks(a):
        a = jnp.pad(a.astype(jnp.float32), ((0, 0), (0, pad), (0, 0), (0, 0)))
        return a.reshape(B, n, L, H, a.shape[-1]).transpose(1, 0, 3, 2, 4)

    causal = jnp.tril(jnp.ones((L, L), bool))[:, :, None]

    def step(S, inp):
        qc, kc, vc, gc = inp
        b = jnp.cumsum(gc, axis=2)
        o = jnp.einsum('bhtk,bhkv->bhtv', qc * jnp.exp(b), S)
        decay = jnp.exp(jnp.where(causal, b[:, :, :, None, :] - b[:, :, None, :, :], NEG))
        a = jnp.einsum('bhtk,bhsk,bhtsk->bhts', qc, kc, decay)
        o = o + jnp.einsum('bhts,bhsv->bhtv', a, vc)
        bl = b[:, :, -1:, :]
        S = jnp.exp(bl[:, :, 0, :])[..., None] * S + jnp.einsum('bhsk,bhsv->bhkv', kc * jnp.exp(bl - b), vc)
        return S, o

    S, o = lax.scan(step, S0.astype(jnp.float32), (chunks(q), chunks(k), chunks(v), chunks(logf)))
    o = o.transpose(1, 0, 3, 2, 4).reshape(B, n * L, H, DV)[:, :T]
    return o, S


def mla_keys(c, r, w_uk, gk):
    lead = c.shape[:-1]
    kn = (c @ w_uk).reshape(lead + (MLA_HEADS, MLA_NOPE))
    kr = jnp.broadcast_to(r[..., None, :], lead + (MLA_HEADS, MLA_ROPE)).astype(kn.dtype)
    return rmsnorm(jnp.concatenate([kn, kr], -1), gk)


def mla_prompt_attend(q, c, r, w_uk, gk):
    B, T = q.shape[:2]
    k = mla_keys(c, r, w_uk, gk)
    blk = min(Q_BLOCK, T)
    nb = T // blk
    qb = q.reshape(B, nb, blk, MLA_HEADS, MLA_QK).transpose(1, 0, 2, 3, 4)
    kpos = jnp.arange(T)

    def block(args):
        qi, i = args
        qpos = i * blk + jnp.arange(blk)
        s = jnp.einsum('bqhd,bkhd->bhqk', qi, k).astype(jnp.float32) * MLA_SCALE
        s = jnp.where((kpos[None, :] <= qpos[:, None])[None, None], s, NEG)
        p = jax.nn.softmax(s, axis=-1).astype(c.dtype)
        return jnp.einsum('bhqk,bkr->bqhr', p, c)

    o = lax.map(block, (qb, jnp.arange(nb)))
    return o.transpose(1, 0, 2, 3, 4).reshape(B, T, MLA_HEADS, MLA_KV_RANK)


def mla_sample_attend(q, c, r, w_uk, gk, layer, pool_c, pool_r, page_table):
    Tq = q.shape[1]
    past = page_table.shape[1] * pool_c.shape[2]
    qpos = past + jnp.arange(Tq)
    kpos = jnp.arange(past + Tq)
    mask = kpos[None, :] <= qpos[:, None]

    def one(args):
        qb, cb, rb, pages = args
        c_all = jnp.concatenate([pool_c[layer, pages].reshape(past, MLA_KV_RANK).astype(cb.dtype), cb], 0)
        r_all = jnp.concatenate([pool_r[layer, pages].reshape(past, MLA_ROPE).astype(rb.dtype), rb], 0)
        k = mla_keys(c_all, r_all, w_uk, gk)
        s = jnp.einsum('qhd,khd->hqk', qb, k).astype(jnp.float32) * MLA_SCALE
        s = jnp.where(mask[None], s, NEG)
        p = jax.nn.softmax(s, axis=-1).astype(c_all.dtype)
        return jnp.einsum('hqk,kr->qhr', p, c_all)

    return lax.map(one, (q, c, r, page_table))


def memory_kv(mem, g, w_k, w_v, gk):
    B, M, _ = mem.shape
    mn = rmsnorm(mem, g)
    k = rmsnorm((mn @ w_k).reshape(B, M, MEM_HEADS, MEM_DH), gk)
    v = (mn @ w_v).reshape(B, M, MEM_HEADS, MEM_DH)
    return k, v


def hybrid_layer(x, conv_hist, S0, mem_k, mem_v, pos, lb, attend, lw):
    (norm_g, w_in, conv_w, hg_g, q_norm_g, w_uq, kv_norm_g, w_uk, w_uv, gq, gk, mem_gq, w_bout, w_o) = lw
    B, T, _ = x.shape
    hn = rmsnorm(x, norm_g)
    z = hn @ w_in
    offs = [int(o) for o in np.cumsum(IN_SIZES)[:-1]]
    (c_h, c_b, c_c, hg_q, hg_f, hg_i, q_lat, kv_lat, k_pe, m_q, sg, mg) = jnp.split(z, offs, axis=-1)

    u = c_c * c_h
    upad = jnp.concatenate([conv_hist.astype(u.dtype), u], axis=1)
    conv = sum(conv_w[j] * upad[:, j:j + T] for j in range(CONV_K))
    y_conv = c_b * conv
    new_hist = upad[:, -(CONV_K - 1):]

    logf, kk = hgrn_forget(hg_f, lb)
    hs = (B, T, HG_HEADS, HG_DK)
    o_hg, S = hgrn2_chunked((hg_q * HG_DK ** -0.5).reshape(hs), kk.reshape(hs), hg_i.reshape(hs), logf.reshape(hs), S0)
    y_hg = rmsnorm(o_hg, hg_g.reshape(HG_HEADS, HG_DK)).astype(x.dtype).reshape(B, T, BR_W)

    qf = (rmsnorm(q_lat, q_norm_g) @ w_uq).reshape(B, T, MLA_HEADS, MLA_QK)
    q = rmsnorm(jnp.concatenate([qf[..., :MLA_NOPE], rope(qf[..., MLA_NOPE:], pos)], -1), gq)
    c = rmsnorm(kv_lat, kv_norm_g)
    r = rope(k_pe, pos)
    o_lat = attend(q, c, r, w_uk, gk)
    y_mla = jnp.einsum('bthr,rhv->bthv', o_lat, w_uv.reshape(MLA_KV_RANK, MLA_HEADS, MLA_V)).reshape(B, T, BR_W)

    mq = rmsnorm(m_q.reshape(B, T, MEM_HEADS, MEM_DH), mem_gq)
    s = jnp.einsum('bthd,bmhd->bhtm', mq, mem_k.astype(mq.dtype)).astype(jnp.float32) * MEM_SCALE
    p = jax.nn.softmax(s, axis=-1).astype(mem_v.dtype)
    y_mem = jnp.einsum('bhtm,bmhd->bthd', p, mem_v).reshape(B, T, BR_W).astype(x.dtype)

    ys = jnp.stack([y_conv, y_hg, y_mla, y_mem], axis=2) * jax.nn.silu(sg.reshape(B, T, N_BRANCH, BR_W))
    proj = jnp.einsum('btnw,nwd->btnd', ys, w_bout)
    merged = jnp.sum(jax.nn.sigmoid(mg.reshape(B, T, N_BRANCH, D_MODEL)) * proj, axis=2)
    return x + merged @ w_o, new_hist, S, c, r


def setup_inputs(seed: int = 0) -> dict:
    key = jax.random.key(seed)
    ks = list(jax.random.split(key, 32))

    def nrm(shape, scale=1.0):
        return jax.random.normal(ks.pop(), shape, jnp.float32) * scale

    def gain(shape):
        return 1.0 + 0.1 * jax.random.normal(ks.pop(), shape, jnp.float32)

    n_pages = PAST_LEN // PAGE_SIZE
    n_used = DEC_BATCH * n_pages
    n_phys = n_used + max(n_used // 4, 1)
    x_prompt = nrm((BATCH, SEQ, D_MODEL))
    x_sample = nrm((DEC_BATCH, DEC_SEQ, D_MODEL))
    mem_prompt = nrm((BATCH, MEM_LEN, D_MODEL))
    cache_mla_latent = nrm((DEPTH, n_phys, PAGE_SIZE, MLA_KV_RANK))
    cache_mla_rope = nrm((DEPTH, n_phys, PAGE_SIZE, MLA_ROPE))
    page_table = jax.random.permutation(ks.pop(), n_phys)[:n_used].reshape(DEC_BATCH, n_pages).astype(jnp.int32)
    state_hgrn = nrm((DEPTH, DEC_BATCH, HG_HEADS, HG_DK, HG_DK), 0.5)
    state_conv = nrm((DEPTH, DEC_BATCH, CONV_K - 1, BR_W))
    cache_mem_k = nrm((DEPTH, DEC_BATCH, MEM_LEN, MEM_HEADS, MEM_DH))
    cache_mem_v = nrm((DEPTH, DEC_BATCH, MEM_LEN, MEM_HEADS, MEM_DH))
    return {
        'x_prompt': x_prompt, 'x_sample': x_sample, 'mem_prompt': mem_prompt,
        'cache_mla_latent': cache_mla_latent, 'cache_mla_rope': cache_mla_rope, 'page_table': page_table,
        'state_hgrn': state_hgrn, 'state_conv': state_conv,
        'cache_mem_k': cache_mem_k, 'cache_mem_v': cache_mem_v,
        'norm_gain': gain((DEPTH, D_MODEL)),
        'w_in': nrm((DEPTH, D_MODEL, N_IN), D_MODEL ** -0.5),
        'conv_w': nrm((DEPTH, CONV_K, BR_W), CONV_K ** -0.5),
        'hgrn_lb': nrm((DEPTH, BR_W)),
        'hgrn_norm': gain((DEPTH, BR_W)),
        'mla_q_norm': gain((DEPTH, MLA_Q_RANK)),
        'mla_w_uq': nrm((DEPTH, MLA_Q_RANK, MLA_HEADS * MLA_QK), MLA_Q_RANK ** -0.5),
        'mla_kv_norm': gain((DEPTH, MLA_KV_RANK)),
        'mla_w_uk': nrm((DEPTH, MLA_KV_RANK, MLA_HEADS * MLA_NOPE), MLA_KV_RANK ** -0.5),
        'mla_w_uv': nrm((DEPTH, MLA_KV_RANK, MLA_HEADS * MLA_V), MLA_KV_RANK ** -0.5),
        'mla_q_gain': gain((DEPTH, MLA_QK)),
        'mla_k_gain': gain((DEPTH, MLA_QK)),
        'mem_norm': gain((DEPTH, D_MODEL)),
        'mem_w_k': nrm((DEPTH, D_MODEL, BR_W), D_MODEL ** -0.5),
        'mem_w_v': nrm((DEPTH, D_MODEL, BR_W), D_MODEL ** -0.5),
        'mem_q_gain': gain((DEPTH, MEM_DH)),
        'mem_k_gain': gain((DEPTH, MEM_DH)),
        'w_branch_out': nrm((DEPTH, N_BRANCH, BR_W, D_MODEL), BR_W ** -0.5),
        'w_out': nrm((DEPTH, D_MODEL, D_MODEL), D_MODEL ** -0.5),
    }


def reference(x_prompt, x_sample, mem_prompt, cache_mla_latent, cache_mla_rope, page_table, state_hgrn, state_conv,
              cache_mem_k, cache_mem_v, norm_gain, w_in, conv_w, hgrn_lb, hgrn_norm, mla_q_norm, mla_w_uq,
              mla_kv_norm, mla_w_uk, mla_w_uv, mla_q_gain, mla_k_gain, mem_norm, mem_w_k, mem_w_v, mem_q_gain,
              mem_k_gain, w_branch_out, w_out):
    lbs = hgrn_lower_bounds(hgrn_lb)
    Bp, Tp, _ = x_prompt.shape
    Td = x_sample.shape[1]
    past = page_table.shape[1] * cache_mla_latent.shape[2]
    pos_p = jnp.arange(Tp, dtype=jnp.int32)
    pos_s = past + jnp.arange(Td, dtype=jnp.int32)
    xp, xs = x_prompt, x_sample
    p_lat, p_rope, p_hg, p_conv, p_mk, p_mv = [], [], [], [], [], []
    s_lat, s_rope, s_hg, s_conv = [], [], [], []
    for l in range(DEPTH):
        lw = (norm_gain[l], w_in[l], conv_w[l], hgrn_norm[l], mla_q_norm[l], mla_w_uq[l], mla_kv_norm[l],
              mla_w_uk[l], mla_w_uv[l], mla_q_gain[l], mla_k_gain[l], mem_q_gain[l], w_branch_out[l], w_out[l])
        mk, mv = memory_kv(mem_prompt, mem_norm[l], mem_w_k[l], mem_w_v[l], mem_k_gain[l])
        conv0 = jnp.zeros((Bp, CONV_K - 1, BR_W), xp.dtype)
        S0 = jnp.zeros((Bp, HG_HEADS, HG_DK, HG_DK), jnp.float32)
        xp, hist, S, c, r = hybrid_layer(xp, conv0, S0, mk, mv, pos_p, lbs[l], mla_prompt_attend, lw)
        p_lat.append(c)
        p_rope.append(r)
        p_hg.append(S.astype(xp.dtype))
        p_conv.append(hist)
        p_mk.append(mk)
        p_mv.append(mv)
        attend_s = functools.partial(mla_sample_attend, layer=l, pool_c=cache_mla_latent, pool_r=cache_mla_rope,
                                     page_table=page_table)
        xs, hist_s, S_s, c_s, r_s = hybrid_layer(xs, state_conv[l], state_hgrn[l], cache_mem_k[l], cache_mem_v[l],
                                                 pos_s, lbs[l], attend_s, lw)
        s_lat.append(c_s)
        s_rope.append(r_s)
        s_hg.append(S_s.astype(state_hgrn.dtype))
        s_conv.append(hist_s.astype(state_conv.dtype))
    return (xp, xs, jnp.stack(p_lat), jnp.stack(p_rope), jnp.stack(p_hg), jnp.stack(p_conv), jnp.stack(p_mk),
            jnp.stack(p_mv), jnp.stack(s_lat), jnp.stack(s_rope), jnp.stack(s_hg), jnp.stack(s_conv))
```

```python
import functools

import numpy as np
import jax
import jax.numpy as jnp
from jax import lax
from jax.experimental import pallas as pl
from jax.experimental.pallas import tpu as pltpu

F32 = jnp.float32
BF16 = jnp.bfloat16

N_HEADS = 4
HEAD_W = 128
MLA_NOPE = 64
MLA_ROPE = 32
MLA_QK = MLA_NOPE + MLA_ROPE
MLA_Q_RANK = 192
MLA_KV_RANK = 128
CONV_K = 3
ROPE_THETA = 10000.0
EPS = 1e-6
NEG = -1e30
PAGE = 128
SAMPLE_PAD_T = 8
HG_CHUNK = 32
VMEM_LIMIT = 56 * 1024 * 1024

COL_CONV = 0
COL_HG = 1536
COL_MLA = 3072
COL_MEMQ = 3584
COL_MG = 4096
COL_SG = 8192
N_COLS = 10240


def _cparams(sem):
    return pltpu.CompilerParams(dimension_semantics=sem, vmem_limit_bytes=VMEM_LIMIT)


def _rms(x, g):
    return x * lax.rsqrt(jnp.mean(x * x, axis=-1, keepdims=True) + EPS) * g


def _dot(a, b):
    return jnp.dot(a, b, preferred_element_type=F32)


def _dot_nt(a, b):
    return lax.dot_general(a, b, (((1,), (1,)), ((), ())), preferred_element_type=F32)


def _dot_tn(a, b):
    return lax.dot_general(a, b, (((0,), (0,)), ((), ())), preferred_element_type=F32)


def _inproj_kernel(x_ref, g_ref, w_ref, z_ref, hn_ref):
    @pl.when(pl.program_id(1) == 0)
    def _():
        hn_ref[...] = _rms(x_ref[...], g_ref[...]).astype(BF16)

    z_ref[...] = _dot(hn_ref[...], w_ref[...]).astype(BF16)


def _inproj(x2d, g, w_bf, tm, tn):
    n, d = x2d.shape
    ncol = w_bf.shape[1]
    return pl.pallas_call(
        _inproj_kernel,
        grid=(n // tm, ncol // tn),
        in_specs=[pl.BlockSpec((tm, d), lambda i, j: (i, 0)),
                  pl.BlockSpec((1, d), lambda i, j: (0, 0)),
                  pl.BlockSpec((d, tn), lambda i, j: (0, j))],
        out_specs=pl.BlockSpec((tm, tn), lambda i, j: (i, j)),
        out_shape=jax.ShapeDtypeStruct((n, ncol), BF16),
        scratch_shapes=[pltpu.VMEM((tm, d), BF16)],
        compiler_params=_cparams(("arbitrary", "arbitrary")),
        name="inproj",
    )(x2d, g.reshape(1, d), w_bf)


def _conv_kernel(*refs, seq_t, tr, tail_rows, has_hist):
    if has_hist:
        z_ref, halo_ref, hist_ref, w_ref, y_ref, tail_ref = refs
    else:
        z_ref, halo_ref, w_ref, y_ref, tail_ref = refs
    i = pl.program_id(0)
    br = y_ref.shape[1]
    z = z_ref[...].astype(F32)
    u = z[:, 2 * br:3 * br] * z[:, 0:br]
    zh = halo_ref[...].astype(F32)
    uh = zh[:, 2 * br:3 * br] * zh[:, 0:br]
    loc = lax.broadcasted_iota(jnp.int32, (tr, 1), 0)
    t = (loc + i * tr) % seq_t
    u1 = jnp.where(loc == 0, uh[7:8], pltpu.roll(u, 1, axis=0))
    u2 = jnp.where(loc == 0, uh[6:7], jnp.where(loc == 1, uh[7:8], pltpu.roll(u, 2, axis=0)))
    if has_hist:
        hp = hist_ref[...]
        u1 = jnp.where(t == 0, pltpu.roll(hp, tr - 1, axis=0), u1)
        u2 = jnp.where(t < 2, hp, u2)
    else:
        u1 = jnp.where(t == 0, 0.0, u1)
        u2 = jnp.where(t < 2, 0.0, u2)
    w = w_ref[...]
    conv = w[0:1] * u2 + w[1:2] * u1 + w[2:3] * u
    y_ref[...] = (z[:, br:2 * br] * conv).astype(BF16)

    @pl.when(((i + 1) * tr) % max(seq_t, tr) == 0)
    def _():
        tail_ref[...] = u[tr - tail_rows:, :]


def _conv(z2d, hist_rows, w, seq_t, tr, tail_rows):
    n = z2d.shape[0]
    br = w.shape[1]
    has_hist = hist_rows is not None
    group = max(seq_t, tr)
    n_tail = (n // group) * tail_rows
    in_specs = [pl.BlockSpec((tr, 3 * br), lambda i: (i, 0)),
                pl.BlockSpec((8, 3 * br), lambda i: (jnp.maximum(i * (tr // 8) - 1, 0), 0))]
    args = [z2d, z2d]
    if has_hist:
        in_specs.append(pl.BlockSpec((tr, br), lambda i: (i, 0)))
        args.append(hist_rows)
    in_specs.append(pl.BlockSpec((CONV_K, br), lambda i: (0, 0)))
    args.append(w)
    return pl.pallas_call(
        functools.partial(_conv_kernel, seq_t=seq_t, tr=tr, tail_rows=tail_rows, has_hist=has_hist),
        grid=(n // tr,),
        in_specs=in_specs,
        out_specs=[pl.BlockSpec((tr, br), lambda i: (i, 0)),
                   pl.BlockSpec((tail_rows, br), lambda i: ((i * tr) // group, 0))],
        out_shape=[jax.ShapeDtypeStruct((n, br), BF16), jax.ShapeDtypeStruct((n_tail, br), F32)],
        compiler_params=_cparams(("arbitrary",)),
        name="conv",
    )(*args)


def _hgrn_kernel(q_ref, f_ref, i_ref, lb_ref, g_ref, s0_ref, y_ref, sout_ref, st_ref, *, sb, tt, chunk, t_valid):
    tstep = pl.program_id(2)
    n_t = pl.num_programs(2)

    @pl.when(tstep == 0)
    def _():
        for s in range(sb):
            st_ref[s] = s0_ref[s, 0].T

    lb = lb_ref[0]
    gain = g_ref[0]
    row = lax.broadcasted_iota(jnp.int32, (chunk, chunk), 0)
    col = lax.broadcasted_iota(jnp.int32, (chunk, chunk), 1)
    tril = (row >= col)
    tril_bf = tril.astype(BF16)
    tril3 = (lax.broadcasted_iota(jnp.int32, (chunk, chunk, HEAD_W), 0)
             >= lax.broadcasted_iota(jnp.int32, (chunk, chunk, HEAD_W), 1))
    scale = HEAD_W ** -0.5

    for s in range(sb):
        def body(c, carry, s=s):
            r0 = pl.multiple_of(c * chunk, chunk)
            q = q_ref[s, pl.ds(r0, chunk), :].astype(F32) * scale
            zf = f_ref[s, pl.ds(r0, chunk), :].astype(F32)
            v = i_ref[s, pl.ds(r0, chunk), :].astype(F32)
            f = lb + (1.0 - lb) * jax.nn.sigmoid(zf)
            g = jnp.log(f)
            kk = 1.0 - f
            if t_valid is not None:
                pos = tstep * tt + r0 + lax.broadcasted_iota(jnp.int32, (chunk, 1), 0)
                g = jnp.where(pos < t_valid, g, 0.0)
                kk = jnp.where(pos < t_valid, kk, 0.0)
            g_hi = g.astype(BF16)
            g_lo = (g - g_hi.astype(F32)).astype(BF16)
            b = _dot(tril_bf, g_hi) + _dot(tril_bf, g_lo)
            st = st_ref[s]
            o = _dot_nt((q * jnp.exp(b)).astype(BF16), st.astype(BF16))
            d = b[:, None, :] - b[None, :, :]
            p = jnp.exp(jnp.where(tril3, d, NEG)) * (q[:, None, :] * kk[None, :, :])
            a = jnp.sum(p, axis=-1)
            vb = v.astype(BF16)
            o = o + _dot(a.astype(BF16), vb)
            bl = b[chunk - 1:chunk, :]
            kd = (kk * jnp.exp(bl - b)).astype(BF16)
            st_ref[s] = st * jnp.exp(bl) + _dot_tn(vb, kd)
            y_ref[s, pl.ds(r0, chunk), :] = _rms(o, gain).astype(BF16)
            return carry

        lax.fori_loop(0, tt // chunk, body, 0)

    @pl.when(tstep == n_t - 1)
    def _():
        for s in range(sb):
            sout_ref[s, 0] = st_ref[s].T


def _hgrn(z3d, lb, gain, s0, sb, tt, chunk, t_valid):
    b, t, _ = z3d.shape
    cb = COL_HG // HEAD_W
    kern = functools.partial(_hgrn_kernel, sb=sb, tt=tt, chunk=chunk, t_valid=t_valid)
    return pl.pallas_call(
        kern,
        grid=(b // sb, N_HEADS, t // tt),
        in_specs=[pl.BlockSpec((sb, tt, HEAD_W), lambda i, h, k: (i, k, cb + h)),
                  pl.BlockSpec((sb, tt, HEAD_W), lambda i, h, k: (i, k, cb + N_HEADS + h)),
                  pl.BlockSpec((sb, tt, HEAD_W), lambda i, h, k: (i, k, cb + 2 * N_HEADS + h)),
                  pl.BlockSpec((1, 1, HEAD_W), lambda i, h, k: (h, 0, 0)),
                  pl.BlockSpec((1, 1, HEAD_W), lambda i, h, k: (h, 0, 0)),
                  pl.BlockSpec((sb, 1, HEAD_W, HEAD_W), lambda i, h, k: (i, h, 0, 0))],
        out_specs=[pl.BlockSpec((sb, tt, HEAD_W), lambda i, h, k: (i, k, h)),
                   pl.BlockSpec((sb, 1, HEAD_W, HEAD_W), lambda i, h, k: (i, h, 0, 0))],
        out_shape=[jax.ShapeDtypeStruct((b, t, N_HEADS * HEAD_W), BF16),
                   jax.ShapeDtypeStruct((b, N_HEADS, HEAD_W, HEAD_W), F32)],
        scratch_shapes=[pltpu.VMEM((sb, HEAD_W, HEAD_W), F32)],
        compiler_params=_cparams(("arbitrary", "arbitrary", "arbitrary")),
        name="hgrn",
    )(z3d, z3d, z3d, lb.reshape(N_HEADS, 1, HEAD_W), gain.reshape(N_HEADS, 1, HEAD_W), s0)


def _mla_prep_kernel(z_ref, cos_ref, sin_ref, gq_ref, wuq_ref, gkv_ref, wuk_ref, wabs_ref, grope_ref,
                     indq_ref, indk_ref, c_ref, r_ref, kv_ref, rk_ref, qp_ref):
    z = z_ref[...].astype(F32)
    ql = z[:, 0:MLA_Q_RANK]
    kvl = z[:, 256:384]
    kpe = z[:, 384:512]
    cos = cos_ref[...]
    sin = sin_ref[...]
    lane = lax.broadcasted_iota(jnp.int32, (1, 128), 1)

    qf = _dot(_rms(ql, gq_ref[...]).astype(BF16), wuq_ref[...])
    q_nope = qf[:, 0:256]
    rq_in = qf[:, 256:384]
    rot_q = rq_in * cos + pltpu.roll(rq_in, 64, axis=1) * sin
    rot_k = kpe * cos + pltpu.roll(kpe, 64, axis=1) * sin

    c = _rms(kvl, gkv_ref[...])
    c_ref[...] = c
    r32 = jnp.where(lane < 16, rot_k, pltpu.roll(rot_k, 80, axis=1))
    r32 = jnp.where(lane < MLA_ROPE, r32, 0.0)
    r_ref[...] = r32[:, 0:MLA_ROPE]
    cb = c.astype(BF16)
    kv_ref[...] = jnp.concatenate([cb, r32.astype(BF16)], axis=1)

    kn = _dot(cb, wuk_ref[...])
    kcat2 = jnp.concatenate([kn * kn, rot_k * rot_k], axis=1).astype(BF16)
    ssk = _dot_nt(indk_ref[...], kcat2)
    rk_ref[...] = lax.rsqrt(ssk * (1.0 / MLA_QK) + EPS)

    qcat2 = jnp.concatenate([q_nope * q_nope, rot_q * rot_q], axis=1).astype(BF16)
    ssq = _dot(qcat2, indq_ref[...])
    rq = lax.rsqrt(ssq * (1.0 / MLA_QK) + EPS) * (MLA_QK ** -0.5)

    q_abs = _dot(q_nope.astype(BF16), wabs_ref[...])
    grope = grope_ref[...]
    for h in range(N_HEADS):
        a = rot_q if h == 0 else pltpu.roll(rot_q, 128 - 16 * h, axis=1)
        qr = jnp.where(lane < 16, a, pltpu.roll(a, 80, axis=1)) * grope
        rq_h = rq[:, h:h + 1]
        qp_ref[:, h * 256:h * 256 + 128] = (q_abs[:, h * 128:(h + 1) * 128] * rq_h).astype(BF16)
        qp_ref[:, h * 256 + 128:(h + 1) * 256] = (qr * rq_h).astype(BF16)


def _mla_prep(z2d, cos_t, sin_t, pw, tm):
    n = z2d.shape[0]
    n_tab = cos_t.shape[0] // tm
    full = lambda shape: pl.BlockSpec(shape, lambda i: (0,) * len(shape))
    return pl.pallas_call(
        _mla_prep_kernel,
        grid=(n // tm,),
        in_specs=[pl.BlockSpec((tm, 512), lambda i: (i, COL_MLA // 512)),
                  pl.BlockSpec((tm, 128), lambda i: (i % n_tab, 0)),
                  pl.BlockSpec((tm, 128), lambda i: (i % n_tab, 0)),
                  full((1, MLA_Q_RANK)), full((MLA_Q_RANK, 384)), full((1, MLA_KV_RANK)),
                  full((MLA_KV_RANK, 256)), full((256, 512)), full((1, 128)),
                  full((384, 128)), full((8, 384))],
        out_specs=[pl.BlockSpec((tm, MLA_KV_RANK), lambda i: (i, 0)),
                   pl.BlockSpec((tm, MLA_ROPE), lambda i: (i, 0)),
                   pl.BlockSpec((tm, 256), lambda i: (i, 0)),
                   pl.BlockSpec((8, tm), lambda i: (0, i)),
                   pl.BlockSpec((tm, 1024), lambda i: (i, 0))],
        out_shape=[jax.ShapeDtypeStruct((n, MLA_KV_RANK), F32),
                   jax.ShapeDtypeStruct((n, MLA_ROPE), F32),
                   jax.ShapeDtypeStruct((n, 256), BF16),
                   jax.ShapeDtypeStruct((8, n), F32),
                   jax.ShapeDtypeStruct((n, 1024), BF16)],
        compiler_params=_cparams(("arbitrary",)),
        name="mla_prep",
    )(z2d, cos_t, sin_t, pw["gq_norm"], pw["wuq"], pw["gkv_norm"], pw["wuk"], pw["wabs"], pw["grope"],
      pw["indq"], pw["indk"])


def _mla_prompt_kernel(qp_ref, kv_ref, rk_ref, wuv_ref, y_ref, *, tq):
    i = pl.program_id(1)
    row = lax.broadcasted_iota(jnp.int32, (tq, tq), 0)
    col = lax.broadcasted_iota(jnp.int32, (tq, tq), 1)
    for h in range(N_HEADS):
        q = qp_ref[:, h * 256:(h + 1) * 256]

        def step(j, carry, h=h, q=q):
            m, l, acc = carry
            k0 = pl.multiple_of(j * tq, tq)
            kv = kv_ref[pl.ds(k0, tq), :]
            s = _dot_nt(q, kv) * rk_ref[h:h + 1, pl.ds(k0, tq)]
            s = jnp.where(jnp.logical_or(j < i, col <= row), s, NEG)
            m_new = jnp.maximum(m, jnp.max(s, axis=-1, keepdims=True))
            alpha = jnp.exp(m - m_new)
            p = jnp.exp(s - m_new)
            l = alpha * l + jnp.sum(p, axis=-1, keepdims=True)
            acc = alpha * acc + _dot(p.astype(BF16), kv[:, 0:MLA_KV_RANK])
            return m_new, l, acc

        init = (jnp.full((tq, 1), NEG, F32), jnp.zeros((tq, 1), F32), jnp.zeros((tq, MLA_KV_RANK), F32))
        _, l, acc = lax.fori_loop(0, i + 1, step, init)
        o = (acc / l).astype(BF16)
        y_ref[:, h * HEAD_W:(h + 1) * HEAD_W] = _dot(o, wuv_ref[:, h * HEAD_W:(h + 1) * HEAD_W]).astype(BF16)


def _mla_prompt(qp, kv, rk, wuv, b, t, tq):
    n = b * t
    nq = t // tq
    return pl.pallas_call(
        functools.partial(_mla_prompt_kernel, tq=tq),
        grid=(b, nq),
        in_specs=[pl.BlockSpec((tq, 1024), lambda bi, i: (bi * nq + i, 0)),
                  pl.BlockSpec((t, 256), lambda bi, i: (bi, 0)),
                  pl.BlockSpec((8, t), lambda bi, i: (0, bi)),
                  pl.BlockSpec((MLA_KV_RANK, N_HEADS * HEAD_W), lambda bi, i: (0, 0))],
        out_specs=pl.BlockSpec((tq, N_HEADS * HEAD_W), lambda bi, i: (bi * nq + i, 0)),
        out_shape=jax.ShapeDtypeStruct((n, N_HEADS * HEAD_W), BF16),
        compiler_params=_cparams(("arbitrary", "arbitrary")),
        name="mla_prompt",
    )(qp, kv, rk, wuv)


def _mla_sample_kernel(pt_ref, qp_ref, cnew_ref, rnew_ref, wuk_ref, wuv_ref, ind_ref, lat_hbm, rope_hbm,
                       y_ref, cbuf, rbuf, cbf, s_scr, sem, *, layer, n_pages, ck):
    b = pl.program_id(0)
    n_b = pl.num_programs(0)
    past = n_pages * PAGE
    tk = past + PAGE
    slot = b % 2
    tp = SAMPLE_PAD_T
    nrow = N_HEADS * tp

    def lat_copy(seq, p, sl):
        pg = pt_ref[seq * n_pages + p]
        return pltpu.make_async_copy(lat_hbm.at[layer, pg], cbuf.at[sl, pl.ds(p * PAGE, PAGE), :], sem.at[0, sl])

    def rope_copy(seq, p, sl):
        pg = pt_ref[seq * n_pages + p]
        return pltpu.make_async_copy(rope_hbm.at[layer, pg], rbuf.at[sl, pl.ds(p * PAGE, PAGE), :], sem.at[1, sl])

    def issue(seq, sl):
        def body(p, carry):
            lat_copy(seq, p, sl).start()
            rope_copy(seq, p, sl).start()
            return carry
        lax.fori_loop(0, n_pages, body, 0)

    @pl.when(b == 0)
    def _():
        for sl in range(2):
            cbuf[sl, pl.ds(past, PAGE), :] = jnp.zeros((PAGE, MLA_KV_RANK), F32)
            rbuf[sl, pl.ds(past, PAGE), :] = jnp.zeros((PAGE, MLA_ROPE), F32)
        issue(0, 0)

    @pl.when(b + 1 < n_b)
    def _():
        issue(b + 1, 1 - slot)

    def wait_body(p, carry):
        lat_copy(b, p, slot).wait()
        rope_copy(b, p, slot).wait()
        return carry
    lax.fori_loop(0, n_pages, wait_body, 0)

    cbuf[slot, pl.ds(past, tp), :] = cnew_ref[0]
    rbuf[slot, pl.ds(past, tp), :] = rnew_ref[0]

    q = qp_ref[0]
    qc = q[:, 0:MLA_KV_RANK]
    qr = q[:, MLA_KV_RANK:MLA_KV_RANK + MLA_ROPE]
    ind = ind_ref[...]
    ones_r = jnp.ones((nrow, MLA_ROPE), BF16)
    qpos = past + lax.broadcasted_iota(jnp.int32, (nrow, 1), 0) % tp
    for k0 in range(0, tk, ck):
        c_f = cbuf[slot, pl.ds(k0, ck), :]
        r_f = rbuf[slot, pl.ds(k0, ck), :]
        cb = c_f.astype(BF16)
        rb = r_f.astype(BF16)
        cbf[pl.ds(k0, ck), :] = cb
        kn = _dot(cb, wuk_ref[...])
        ss = _dot_nt(ind, (kn * kn).astype(BF16)) + _dot_nt(ones_r, (r_f * r_f).astype(BF16))
        rk = lax.rsqrt(ss * (1.0 / MLA_QK) + EPS)
        s = (_dot_nt(qc, cb) + _dot_nt(qr, rb)) * rk
        kpos = k0 + lax.broadcasted_iota(jnp.int32, (1, ck), 1)
        s_scr[:, pl.ds(k0, ck)] = jnp.where(kpos <= qpos, s, NEG)

    s = s_scr[...]
    m = jnp.max(s, axis=-1, keepdims=True)
    p = jnp.exp(s - m)
    l = jnp.sum(p, axis=-1, keepdims=True)
    o = _dot(p.astype(BF16), cbf[...]) / l
    for h in range(N_HEADS):
        y_ref[0, :, h * HEAD_W:(h + 1) * HEAD_W] = _dot(
            o[h * tp:(h + 1) * tp].astype(BF16), wuv_ref[:, h * HEAD_W:(h + 1) * HEAD_W]).astype(BF16)


def _mla_sample(page_flat, qp3, c3, r3, wuk, wuv, ind, lat, rope, layer, n_pages):
    bs = qp3.shape[0]
    tp = SAMPLE_PAD_T
    nrow = N_HEADS * tp
    tk = n_pages * PAGE + PAGE
    ck = max(d for d in range(PAGE, 2048 + 1, PAGE) if tk % d == 0)
    kern = functools.partial(_mla_sample_kernel, layer=layer, n_pages=n_pages, ck=ck)
    grid_spec = pltpu.PrefetchScalarGridSpec(
        num_scalar_prefetch=1,
        grid=(bs,),
        in_specs=[pl.BlockSpec((1, nrow, 256), lambda b, pt: (b, 0, 0)),
                  pl.BlockSpec((1, tp, MLA_KV_RANK), lambda b, pt: (b, 0, 0)),
                  pl.BlockSpec((1, tp, MLA_ROPE), lambda b, pt: (b, 0, 0)),
                  pl.BlockSpec((MLA_KV_RANK, 256), lambda b, pt: (0, 0)),
                  pl.BlockSpec((MLA_KV_RANK, N_HEADS * HEAD_W), lambda b, pt: (0, 0)),
                  pl.BlockSpec((nrow, 256), lambda b, pt: (0, 0)),
                  pl.BlockSpec(memory_space=pl.ANY),
                  pl.BlockSpec(memory_space=pl.ANY)],
        out_specs=pl.BlockSpec((1, tp, N_HEADS * HEAD_W), lambda b, pt: (b, 0, 0)),
        scratch_shapes=[pltpu.VMEM((2, tk, MLA_KV_RANK), F32),
                        pltpu.VMEM((2, tk, MLA_ROPE), F32),
                        pltpu.VMEM((tk, MLA_KV_RANK), BF16),
                        pltpu.VMEM((nrow, tk), F32),
                        pltpu.SemaphoreType.DMA((2, 2))],
    )
    return pl.pallas_call(
        kern,
        grid_spec=grid_spec,
        out_shape=jax.ShapeDtypeStruct((bs, tp, N_HEADS * HEAD_W), BF16),
        compiler_params=_cparams(("arbitrary",)),
        name="mla_sample",
    )(page_flat, qp3, c3, r3, wuk, wuv, ind, lat, rope)


def _memkv_kernel(x_ref, g_ref, wk_ref, wv_ref, gk_ref, k_ref, v_ref):
    mn = _rms(x_ref[...], g_ref[...]).astype(BF16)
    k = _dot(mn, wk_ref[...])
    v_ref[...] = _dot(mn, wv_ref[...])
    gk = gk_ref[...]
    for h in range(N_HEADS):
        k_ref[:, h * HEAD_W:(h + 1) * HEAD_W] = _rms(k[:, h * HEAD_W:(h + 1) * HEAD_W], gk)


def _memkv(mem2d, g, wk, wv, gk, tm):
    n, d = mem2d.shape
    br = wk.shape[1]
    return pl.pallas_call(
        _memkv_kernel,
        grid=(n // tm,),
        in_specs=[pl.BlockSpec((tm, d), lambda i: (i, 0)),
                  pl.BlockSpec((1, d), lambda i: (0, 0)),
                  pl.BlockSpec((d, br), lambda i: (0, 0)),
                  pl.BlockSpec((d, br), lambda i: (0, 0)),
                  pl.BlockSpec((1, HEAD_W), lambda i: (0, 0))],
        out_specs=[pl.BlockSpec((tm, br), lambda i: (i, 0)), pl.BlockSpec((tm, br), lambda i: (i, 0))],
        out_shape=[jax.ShapeDtypeStruct((n, br), F32), jax.ShapeDtypeStruct((n, br), F32)],
        compiler_params=_cparams(("arbitrary",)),
        name="memkv",
    )(mem2d, g.reshape(1, d), wk, wv, gk.reshape(1, HEAD_W))


def _mem_attn_kernel(q_ref, k_ref, v_ref, gq_ref, y_ref, *, sb):
    gq = gq_ref[...]
    scale = HEAD_W ** -0.5
    for s in range(sb):
        q = q_ref[s].astype(F32)
        for h in range(N_HEADS):
            sl = slice(h * HEAD_W, (h + 1) * HEAD_W)
            qh = (_rms(q[:, sl], gq) * scale).astype(BF16)
            sc = _dot_nt(qh, k_ref[s, :, sl].astype(BF16))
            m = jnp.max(sc, axis=-1, keepdims=True)
            p = jnp.exp(sc - m)
            l = jnp.sum(p, axis=-1, keepdims=True)
            o = _dot(p.astype(BF16), v_ref[s, :, sl].astype(BF16)) / l
            y_ref[s, :, sl] = o.astype(BF16)


def _mem_attn(z3d, k3, v3, gq, sb, tq):
    b, t, _ = z3d.shape
    m = k3.shape[1]
    br = N_HEADS * HEAD_W
    return pl.pallas_call(
        functools.partial(_mem_attn_kernel, sb=sb),
        grid=(b // sb, t // tq),
        in_specs=[pl.BlockSpec((sb, tq, br), lambda i, j: (i, j, COL_MEMQ // br)),
                  pl.BlockSpec((sb, m, br), lambda i, j: (i, 0, 0)),
                  pl.BlockSpec((sb, m, br), lambda i, j: (i, 0, 0)),
                  pl.BlockSpec((1, HEAD_W), lambda i, j: (0, 0))],
        out_specs=pl.BlockSpec((sb, tq, br), lambda i, j: (i, j, 0)),
        out_shape=jax.ShapeDtypeStruct((b, t, br), BF16),
        compiler_params=_cparams(("arbitrary", "arbitrary")),
        name="mem_attn",
    )(z3d, k3, v3, gq.reshape(1, HEAD_W))


def _outproj_kernel(x_ref, mg_ref, sg_ref, y0_ref, y1_ref, y2_ref, y3_ref, wb_ref, wo_ref, o_ref):
    d = x_ref.shape[1]
    br = y0_ref.shape[1]
    merged = None
    for n, y_ref in enumerate((y0_ref, y1_ref, y2_ref, y3_ref)):
        sg = sg_ref[:, n * br:(n + 1) * br].astype(F32)
        ys = (y_ref[...].astype(F32) * (sg * jax.nn.sigmoid(sg))).astype(BF16)
        proj = _dot(ys, wb_ref[n])
        term = jax.nn.sigmoid(mg_ref[:, n * d:(n + 1) * d].astype(F32)) * proj
        merged = term if merged is None else merged + term
    o_ref[...] = x_ref[...] + _dot(merged.astype(BF16), wo_ref[...])


def _outproj(x2d, z2d, ys, wb, wo, tm):
    n, d = x2d.shape
    br = ys[0].shape[1]
    nb = wb.shape[0]
    yspec = pl.BlockSpec((tm, br), lambda i: (i, 0))
    return pl.pallas_call(
        _outproj_kernel,
        grid=(n // tm,),
        in_specs=[pl.BlockSpec((tm, d), lambda i: (i, 0)),
                  pl.BlockSpec((tm, nb * d), lambda i: (i, COL_MG // (nb * d))),
                  pl.BlockSpec((tm, nb * br), lambda i: (i, COL_SG // (nb * br))),
                  yspec, yspec, yspec, yspec,
                  pl.BlockSpec((nb, br, d), lambda i: (0, 0, 0)),
                  pl.BlockSpec((d, d), lambda i: (0, 0))],
        out_specs=pl.BlockSpec((tm, d), lambda i: (i, 0)),
        out_shape=jax.ShapeDtypeStruct((n, d), F32),
        compiler_params=_cparams(("arbitrary",)),
        name="outproj",
    )(x2d, z2d, z2d, *ys, wb, wo)


def _rope_tables(pos):
    half = MLA_ROPE // 2
    freqs = ROPE_THETA ** (-np.arange(half, dtype=np.float64) / half)
    ang = np.asarray(pos, np.float64)[:, None] * freqs[None, :]
    cos = np.tile(np.cos(ang), (1, 8))
    sin = np.tile(np.sin(ang), (1, 8))
    sin[:, :64] *= -1.0
    return jnp.asarray(cos, F32), jnp.asarray(sin, F32)


def _relayout_w_in(w):
    d = w.shape[0]
    z = lambda n: jnp.zeros((d, n), w.dtype)
    o_q, o_kv, o_pe, o_mq, o_sg, o_mg = 3072, 3264, 3392, 3424, 3936, 5984
    half = MLA_ROPE // 2
    cols = [w[:, :o_q],
            w[:, o_q:o_kv], z(64),
            w[:, o_kv:o_pe],
            w[:, o_pe:o_pe + half], z(48), w[:, o_pe + half:o_mq], z(48),
            w[:, o_mq:o_sg],
            w[:, o_mg:],
            w[:, o_sg:o_mg]]
    return jnp.concatenate(cols, axis=1).astype(BF16)


def _mla_params(q_norm, w_uq, kv_norm, w_uk, gq, gk):
    half = MLA_ROPE // 2
    wq = w_uq.reshape(MLA_Q_RANK, N_HEADS, MLA_QK)
    wuq = jnp.concatenate([wq[:, :, :MLA_NOPE].reshape(MLA_Q_RANK, -1),
                           wq[:, :, MLA_NOPE:MLA_NOPE + half].reshape(MLA_Q_RANK, -1),
                           wq[:, :, MLA_NOPE + half:].reshape(MLA_Q_RANK, -1)], axis=1).astype(BF16)
    g2 = gq * gk
    wk = w_uk.reshape(MLA_KV_RANK, N_HEADS, MLA_NOPE)
    wabs = jnp.zeros((N_HEADS * MLA_NOPE, N_HEADS * MLA_KV_RANK), F32)
    for h in range(N_HEADS):
        blk = (wk[:, h, :] * g2[None, :MLA_NOPE]).T
        wabs = wabs.at[h * MLA_NOPE:(h + 1) * MLA_NOPE, h * MLA_KV_RANK:(h + 1) * MLA_KV_RANK].set(blk)
    grope = jnp.zeros((1, 128), F32).at[0, :MLA_ROPE].set(g2[MLA_NOPE:])
    j = np.arange(384)
    head_of = np.where(j < 256, j // MLA_NOPE, (j % 64) // half)
    indq = (head_of[:, None] == np.arange(128)[None, :]).astype(np.float32)
    lane = j - 256
    is_rope = (j >= 256) & ((lane < half) | ((lane >= 64) & (lane < 64 + half)))
    indk = np.zeros((8, 384), np.float32)
    for h in range(N_HEADS):
        indk[h] = ((j < 256) & (j // MLA_NOPE == h)) | is_rope
    return dict(gq_norm=q_norm.reshape(1, -1), wuq=wuq, gkv_norm=kv_norm.reshape(1, -1),
                wuk=w_uk.astype(BF16), wabs=wabs.astype(BF16), grope=grope,
                indq=jnp.asarray(indq, BF16), indk=jnp.asarray(indk, BF16))


def _hgrn_lower_bounds(lb_param):
    p = jax.nn.softmax(lb_param.astype(F32), axis=0)
    return jnp.cumsum(p, axis=0) - p[0]


def kernel(x_prompt, x_sample, mem_prompt, cache_mla_latent, cache_mla_rope, page_table, state_hgrn, state_conv, cache_mem_k, cache_mem_v, norm_gain, w_in, conv_w, hgrn_lb, hgrn_norm, mla_q_norm, mla_w_uq, mla_kv_norm, mla_w_uk, mla_w_uv, mla_q_gain, mla_k_gain, mem_norm, mem_w_k, mem_w_v, mem_q_gain, mem_k_gain, w_branch_out, w_out):
    bp, tp, d = x_prompt.shape
    bs, ts, _ = x_sample.shape
    depth = w_in.shape[0]
    br = conv_w.shape[2]
    mem_len = mem_prompt.shape[1]
    n_pages = page_table.shape[1]
    past = n_pages * cache_mla_latent.shape[2]
    tpad = SAMPLE_PAD_T
    n_p = bp * tp
    n_s = bs * tpad

    lbs = _hgrn_lower_bounds(hgrn_lb)
    cos_p, sin_p = _rope_tables(np.arange(tp))
    cos_s, sin_s = _rope_tables(past + np.arange(tpad))
    tm_s = min(n_s, 1024)
    cos_s = jnp.tile(cos_s, (tm_s // tpad, 1))
    sin_s = jnp.tile(sin_s, (tm_s // tpad, 1))
    page_flat = page_table.reshape(-1).astype(jnp.int32)
    j = np.arange(256)
    ind_s = jnp.asarray((j[None, :] // MLA_NOPE == (np.arange(N_HEADS * tpad) // tpad)[:, None]), BF16)

    tm_p = min(tp, 512)
    xp = x_prompt.reshape(n_p, d)
    xs = jnp.pad(x_sample, ((0, 0), (0, tpad - ts), (0, 0))).reshape(n_s, d)
    mem2d = mem_prompt.reshape(bp * mem_len, d)
    zero_state = jnp.zeros((bp, N_HEADS, HEAD_W, HEAD_W), F32)

    outs = {k: [] for k in ("p_lat", "p_rope", "p_hg", "p_conv", "p_mk", "p_mv", "s_lat", "s_rope", "s_hg", "s_conv")}
    for l in range(depth):
        w_in_l = _relayout_w_in(w_in[l])
        pw = _mla_params(mla_q_norm[l], mla_w_uq[l], mla_kv_norm[l], mla_w_uk[l], mla_q_gain[l], mla_k_gain[l])
        wuv = mla_w_uv[l].astype(BF16)
        wb = w_branch_out[l].astype(BF16)
        wo = w_out[l].astype(BF16)

        z = _inproj(xp, norm_gain[l], w_in_l, min(n_p, 1024), 2048)
        z3 = z.reshape(bp, tp, N_COLS)
        y_conv, tail = _conv(z, None, conv_w[l], tp, tm_p, 8)
        y_hg, s_fin = _hgrn(z3, lbs[l], hgrn_norm[l], zero_state, 1, tm_p, HG_CHUNK, None)
        c, r, kv, rk, qp = _mla_prep(z, cos_p, sin_p, pw, tm_p)
        y_mla = _mla_prompt(qp, kv, rk, wuv, bp, tp, min(tp, 256))
        mk, mv = _memkv(mem2d, mem_norm[l], mem_w_k[l].astype(BF16), mem_w_v[l].astype(BF16), mem_k_gain[l],
                        min(bp * mem_len, 512))
        y_mem = _mem_attn(z3, mk.reshape(bp, mem_len, br), mv.reshape(bp, mem_len, br), mem_q_gain[l], 1, tm_p)
        xp = _outproj(xp, z, (y_conv, y_hg.reshape(n_p, br), y_mla, y_mem.reshape(n_p, br)), wb, wo, tm_p)
        outs["p_lat"].append(c.reshape(bp, tp, MLA_KV_RANK))
        outs["p_rope"].append(r.reshape(bp, tp, MLA_ROPE))
        outs["p_hg"].append(s_fin)
        outs["p_conv"].append(tail.reshape(bp, 8, br)[:, 8 - (CONV_K - 1):])
        outs["p_mk"].append(mk.reshape(bp, mem_len, N_HEADS, HEAD_W))
        outs["p_mv"].append(mv.reshape(bp, mem_len, N_HEADS, HEAD_W))

        z = _inproj(xs, norm_gain[l], w_in_l, tm_s, 2048)
        z3 = z.reshape(bs, tpad, N_COLS)
        hist = jnp.pad(state_conv[l], ((0, 0), (0, tpad - (CONV_K - 1)), (0, 0))).reshape(n_s, br)
        y_conv, u_all = _conv(z, hist, conv_w[l], tpad, tm_s, tm_s)
        sb = 8 if bs % 8 == 0 else 1
        y_hg, s_fin = _hgrn(z3, lbs[l], hgrn_norm[l], state_hgrn[l], sb, tpad, tpad, ts)
        c, r, kv, rk, qp = _mla_prep(z, cos_s, sin_s, pw, tm_s)
        qp3 = qp.reshape(bs, tpad, N_HEADS, 256).transpose(0, 2, 1, 3).reshape(bs, N_HEADS * tpad, 256)
        y_mla = _mla_sample(page_flat, qp3, c.reshape(bs, tpad, MLA_KV_RANK), r.reshape(bs, tpad, MLA_ROPE),
                            pw["wuk"], wuv, ind_s, cache_mla_latent, cache_mla_rope, l, n_pages)
        y_mem = _mem_attn(z3, cache_mem_k[l].reshape(bs, mem_len, br), cache_mem_v[l].reshape(bs, mem_len, br),
                          mem_q_gain[l], sb, tpad)
        xs = _outproj(xs, z, (y_conv, y_hg.reshape(n_s, br), y_mla.reshape(n_s, br), y_mem.reshape(n_s, br)),
                      wb, wo, tm_s)
        outs["s_lat"].append(c.reshape(bs, tpad, MLA_KV_RANK)[:, :ts])
        outs["s_rope"].append(r.reshape(bs, tpad, MLA_ROPE)[:, :ts])
        outs["s_hg"].append(s_fin)
        outs["s_conv"].append(u_all.reshape(bs, tpad, br)[:, ts - (CONV_K - 1):ts])

    st = lambda k: jnp.stack(outs[k])
    return (xp.reshape(bp, tp, d), xs.reshape(bs, tpad, d)[:, :ts], st("p_lat"), st("p_rope"), st("p_hg"),
            st("p_conv"), st("p_mk"), st("p_mv"), st("s_lat"), st("s_rope"), st("s_hg"), st("s_conv"))
```

```python
import functools

import numpy as np
import jax
import jax.numpy as jnp
from jax import lax
from jax.experimental import pallas as pl
from jax.experimental.pallas import tpu as pltpu

F32 = jnp.float32
BF16 = jnp.bfloat16

N_HEADS = 4
HEAD_W = 128
MLA_NOPE = 64
MLA_ROPE = 32
MLA_QK = MLA_NOPE + MLA_ROPE
MLA_Q_RANK = 192
MLA_KV_RANK = 128
CONV_K = 3
ROPE_THETA = 10000.0
EPS = 1e-6
NEG = -1e30
PAGE = 128
SAMPLE_PAD_T = 8
HG_CHUNK = 64
HG_BLOCK = 8
HG_UNROLL = 8
LOG2E = 1.4426950408889634
VMEM_LIMIT = 56 * 1024 * 1024

COL_CONV = 0
COL_HG = 1536
COL_MLA = 3072
COL_MEMQ = 3584
COL_MG = 4096
COL_SG = 8192
N_COLS = 10240


def _cparams(sem):
    return pltpu.CompilerParams(dimension_semantics=sem, vmem_limit_bytes=VMEM_LIMIT)


def _rms(x, g):
    return x * lax.rsqrt(jnp.mean(x * x, axis=-1, keepdims=True) + EPS) * g


def _dot(a, b):
    return jnp.dot(a, b, preferred_element_type=F32)


def _dot_nt(a, b):
    return lax.dot_general(a, b, (((1,), (1,)), ((), ())), preferred_element_type=F32)


def _dot_tn(a, b):
    return lax.dot_general(a, b, (((0,), (0,)), ((), ())), preferred_element_type=F32)


def _inproj_kernel(x_ref, g_ref, w_ref, z_ref, hn_ref):
    @pl.when(pl.program_id(1) == 0)
    def _():
        hn_ref[...] = _rms(x_ref[...], g_ref[...]).astype(BF16)

    z_ref[...] = _dot(hn_ref[...], w_ref[...]).astype(BF16)


def _inproj(x2d, g, w_bf, tm, tn):
    n, d = x2d.shape
    ncol = w_bf.shape[1]
    return pl.pallas_call(
        _inproj_kernel,
        grid=(n // tm, ncol // tn),
        in_specs=[pl.BlockSpec((tm, d), lambda i, j: (i, 0)),
                  pl.BlockSpec((1, d), lambda i, j: (0, 0)),
                  pl.BlockSpec((d, tn), lambda i, j: (0, j))],
        out_specs=pl.BlockSpec((tm, tn), lambda i, j: (i, j)),
        out_shape=jax.ShapeDtypeStruct((n, ncol), BF16),
        scratch_shapes=[pltpu.VMEM((tm, d), BF16)],
        compiler_params=_cparams(("arbitrary", "arbitrary")),
        name="inproj",
    )(x2d, g.reshape(1, d), w_bf)


def _conv_kernel(*refs, seq_t, tr, tail_rows, has_hist):
    if has_hist:
        z_ref, halo_ref, hist_ref, w_ref, y_ref, tail_ref = refs
    else:
        z_ref, halo_ref, w_ref, y_ref, tail_ref = refs
    i = pl.program_id(0)
    br = y_ref.shape[1]
    z = z_ref[...].astype(F32)
    u = z[:, 2 * br:3 * br] * z[:, 0:br]
    zh = halo_ref[...].astype(F32)
    uh = zh[:, 2 * br:3 * br] * zh[:, 0:br]
    loc = lax.broadcasted_iota(jnp.int32, (tr, 1), 0)
    t = (loc + i * tr) % seq_t
    u1 = jnp.where(loc == 0, uh[7:8], pltpu.roll(u, 1, axis=0))
    u2 = jnp.where(loc == 0, uh[6:7], jnp.where(loc == 1, uh[7:8], pltpu.roll(u, 2, axis=0)))
    if has_hist:
        hp = hist_ref[...]
        u1 = jnp.where(t == 0, pltpu.roll(hp, tr - 1, axis=0), u1)
        u2 = jnp.where(t < 2, hp, u2)
    else:
        u1 = jnp.where(t == 0, 0.0, u1)
        u2 = jnp.where(t < 2, 0.0, u2)
    w = w_ref[...]
    conv = w[0:1] * u2 + w[1:2] * u1 + w[2:3] * u
    y_ref[...] = (z[:, br:2 * br] * conv).astype(BF16)

    @pl.when(((i + 1) * tr) % max(seq_t, tr) == 0)
    def _():
        tail_ref[...] = u[tr - tail_rows:, :]


def _conv(z2d, hist_rows, w, seq_t, tr, tail_rows):
    n = z2d.shape[0]
    br = w.shape[1]
    has_hist = hist_rows is not None
    group = max(seq_t, tr)
    n_tail = (n // group) * tail_rows
    in_specs = [pl.BlockSpec((tr, 3 * br), lambda i: (i, 0)),
                pl.BlockSpec((8, 3 * br), lambda i: (jnp.maximum(i * (tr // 8) - 1, 0), 0))]
    args = [z2d, z2d]
    if has_hist:
        in_specs.append(pl.BlockSpec((tr, br), lambda i: (i, 0)))
        args.append(hist_rows)
    in_specs.append(pl.BlockSpec((CONV_K, br), lambda i: (0, 0)))
    args.append(w)
    return pl.pallas_call(
        functools.partial(_conv_kernel, seq_t=seq_t, tr=tr, tail_rows=tail_rows, has_hist=has_hist),
        grid=(n // tr,),
        in_specs=in_specs,
        out_specs=[pl.BlockSpec((tr, br), lambda i: (i, 0)),
                   pl.BlockSpec((tail_rows, br), lambda i: ((i * tr) // group, 0))],
        out_shape=[jax.ShapeDtypeStruct((n, br), BF16), jax.ShapeDtypeStruct((n_tail, br), F32)],
        compiler_params=_cparams(("arbitrary",)),
        name="conv",
    )(*args)


def _hgrn_chunk(q, zf, v, lb, st, valid, tril_bf, chunk):
    nb = chunk // HG_BLOCK
    f = lb + (1.0 - lb) * jax.nn.sigmoid(zf)
    g = jnp.log(f)
    kk = 1.0 - f
    if valid is not None:
        g = jnp.where(valid, g, 0.0)
        kk = jnp.where(valid, kk, 0.0)
    g_hi = g.astype(BF16)
    g_lo = (g - g_hi.astype(F32)).astype(BF16)
    b2 = (_dot(tril_bf, g_hi) + _dot(tril_bf, g_lo)) * LOG2E
    bl2 = b2[chunk - 1:chunk, :]
    vb = v.astype(BF16)

    o = _dot_nt((q * jnp.exp2(b2)).astype(BF16), st.astype(BF16))
    kd = (kk * jnp.exp2(bl2 - b2)).astype(BF16)
    st_new = st * jnp.exp2(bl2) + _dot_tn(vb, kd)

    lane = lax.broadcasted_iota(jnp.int32, (HG_BLOCK, chunk), 1)
    srow = lax.broadcasted_iota(jnp.int32, (HG_BLOCK, chunk), 0)
    at_rows = []
    for gb in range(nb):
        r = gb * HG_BLOCK
        b_blk = b2[r:r + HG_BLOCK]
        k_blk = kk[r:r + HG_BLOCK]
        acc = jnp.zeros((HG_BLOCK, chunk), F32)
        for t in range(HG_BLOCK):
            p = jnp.exp2(jnp.minimum(b2[r + t:r + t + 1] - b_blk, 0.0)) * (q[r + t:r + t + 1] * k_blk)
            acc = jnp.where(lane == r + t, jnp.sum(p, axis=-1, keepdims=True), acc)
        at_rows.append(jnp.where(lane - r >= srow, acc, 0.0))
    at = at_rows[0] if nb == 1 else jnp.concatenate(at_rows, axis=0)
    if nb > 1:
        zero = jnp.zeros((HG_BLOCK, HEAD_W), F32)
        q_rows, k_rows = [], []
        for i in range(nb):
            blk = slice(i * HG_BLOCK, (i + 1) * HG_BLOCK)
            q_tiles, k_tiles = [], []
            for j in range(nb - 1):
                rho = b2[(j + 1) * HG_BLOCK - 1:(j + 1) * HG_BLOCK]
                q_tiles.append(q[blk] * jnp.exp2(b2[blk] - rho) if j < i else zero)
                k_tiles.append(kk[blk] * jnp.exp2(rho - b2[blk]) if j == i else zero)
            q_rows.append(jnp.concatenate(q_tiles, axis=1))
            k_rows.append(jnp.concatenate(k_tiles, axis=1))
        q_hat = jnp.concatenate(q_rows, axis=0).astype(BF16)
        k_hat = jnp.concatenate(k_rows, axis=0).astype(BF16)
        at = at + _dot_nt(k_hat, q_hat)
    o = o + _dot_tn(at.astype(BF16), vb)
    return o, st_new


def _hgrn_kernel(q_ref, f_ref, i_ref, lb_ref, g_ref, s0_ref, y_ref, sout_ref, st_ref, *, sb, tt, chunk, t_valid):
    tstep = pl.program_id(2)
    n_t = pl.num_programs(2)

    @pl.when(tstep == 0)
    def _():
        for s in range(sb):
            st_ref[s] = s0_ref[s, 0].T

    lb = lb_ref[0]
    gain = g_ref[0]
    row = lax.broadcasted_iota(jnp.int32, (chunk, chunk), 0)
    col = lax.broadcasted_iota(jnp.int32, (chunk, chunk), 1)
    tril_bf = (row >= col).astype(BF16)
    scale = HEAD_W ** -0.5

    for s in range(sb):
        def body(c, carry, s=s):
            r0 = pl.multiple_of(c * chunk, chunk)
            q = q_ref[s, pl.ds(r0, chunk), :].astype(F32) * scale
            zf = f_ref[s, pl.ds(r0, chunk), :].astype(F32)
            v = i_ref[s, pl.ds(r0, chunk), :].astype(F32)
            valid = None
            if t_valid is not None:
                valid = tstep * tt + r0 + lax.broadcasted_iota(jnp.int32, (chunk, 1), 0) < t_valid
            o, st_new = _hgrn_chunk(q, zf, v, lb, st_ref[s], valid, tril_bf, chunk)
            st_ref[s] = st_new
            y_ref[s, pl.ds(r0, chunk), :] = _rms(o, gain).astype(BF16)
            return carry

        lax.fori_loop(0, tt // chunk, body, 0, unroll=min(tt // chunk, HG_UNROLL))

    @pl.when(tstep == n_t - 1)
    def _():
        for s in range(sb):
            sout_ref[s, 0] = st_ref[s].T


def _hgrn(z3d, lb, gain, s0, layer, sb, tt, chunk, t_valid):
    b, t, _ = z3d.shape
    cb = COL_HG // HEAD_W
    kern = functools.partial(_hgrn_kernel, sb=sb, tt=tt, chunk=chunk, t_valid=t_valid)
    return pl.pallas_call(
        kern,
        grid=(b // sb, N_HEADS, t // tt),
        in_specs=[pl.BlockSpec((sb, tt, HEAD_W), lambda i, h, k: (i, k, cb + h)),
                  pl.BlockSpec((sb, tt, HEAD_W), lambda i, h, k: (i, k, cb + N_HEADS + h)),
                  pl.BlockSpec((sb, tt, HEAD_W), lambda i, h, k: (i, k, cb + 2 * N_HEADS + h)),
                  pl.BlockSpec((1, 1, HEAD_W), lambda i, h, k: (h, 0, 0)),
                  pl.BlockSpec((1, 1, HEAD_W), lambda i, h, k: (h, 0, 0)),
                  pl.BlockSpec((None, sb, 1, HEAD_W, HEAD_W), lambda i, h, k: (layer, i, h, 0, 0))],
        out_specs=[pl.BlockSpec((sb, tt, HEAD_W), lambda i, h, k: (i, k, h)),
                   pl.BlockSpec((sb, 1, HEAD_W, HEAD_W), lambda i, h, k: (i, h, 0, 0))],
        out_shape=[jax.ShapeDtypeStruct((b, t, N_HEADS * HEAD_W), BF16),
                   jax.ShapeDtypeStruct((b, N_HEADS, HEAD_W, HEAD_W), F32)],
        scratch_shapes=[pltpu.VMEM((sb, HEAD_W, HEAD_W), F32)],
        compiler_params=_cparams(("arbitrary", "arbitrary", "arbitrary")),
        name="hgrn",
    )(z3d, z3d, z3d, lb.reshape(N_HEADS, 1, HEAD_W), gain.reshape(N_HEADS, 1, HEAD_W), s0)


def _mla_prep_kernel(z_ref, cos_ref, sin_ref, gq_ref, wuq_ref, gkv_ref, wuk_ref, wabs_ref, grope_ref,
                     indq_ref, indk_ref, c_ref, r_ref, kv_ref, rk_ref, qp_ref):
    z = z_ref[...].astype(F32)
    ql = z[:, 0:MLA_Q_RANK]
    kvl = z[:, 256:384]
    kpe = z[:, 384:512]
    cos = cos_ref[...]
    sin = sin_ref[...]
    lane = lax.broadcasted_iota(jnp.int32, (1, 128), 1)

    qf = _dot(_rms(ql, gq_ref[...]).astype(BF16), wuq_ref[...])
    q_nope = qf[:, 0:256]
    rq_in = qf[:, 256:384]
    rot_q = rq_in * cos + pltpu.roll(rq_in, 64, axis=1) * sin
    rot_k = kpe * cos + pltpu.roll(kpe, 64, axis=1) * sin

    c = _rms(kvl, gkv_ref[...])
    c_ref[...] = c
    r32 = jnp.where(lane < 16, rot_k, pltpu.roll(rot_k, 80, axis=1))
    r32 = jnp.where(lane < MLA_ROPE, r32, 0.0)
    r_ref[...] = r32[:, 0:MLA_ROPE]
    cb = c.astype(BF16)
    kv_ref[...] = jnp.concatenate([cb, r32.astype(BF16)], axis=1)

    kn = _dot(cb, wuk_ref[...])
    kcat2 = jnp.concatenate([kn * kn, rot_k * rot_k], axis=1).astype(BF16)
    ssk = _dot_nt(indk_ref[...], kcat2)
    rk_ref[...] = lax.rsqrt(ssk * (1.0 / MLA_QK) + EPS)

    qcat2 = jnp.concatenate([q_nope * q_nope, rot_q * rot_q], axis=1).astype(BF16)
    ssq = _dot(qcat2, indq_ref[...])
    rq = lax.rsqrt(ssq * (1.0 / MLA_QK) + EPS) * (MLA_QK ** -0.5)

    q_abs = _dot(q_nope.astype(BF16), wabs_ref[...])
    grope = grope_ref[...]
    for h in range(N_HEADS):
        a = rot_q if h == 0 else pltpu.roll(rot_q, 128 - 16 * h, axis=1)
        qr = jnp.where(lane < 16, a, pltpu.roll(a, 80, axis=1)) * grope
        rq_h = rq[:, h:h + 1]
        qp_ref[:, h * 256:h * 256 + 128] = (q_abs[:, h * 128:(h + 1) * 128] * rq_h).astype(BF16)
        qp_ref[:, h * 256 + 128:(h + 1) * 256] = (qr * rq_h).astype(BF16)


def _mla_prep(z2d, cos_t, sin_t, pw, tm):
    n = z2d.shape[0]
    n_tab = cos_t.shape[0] // tm
    full = lambda shape: pl.BlockSpec(shape, lambda i: (0,) * len(shape))
    return pl.pallas_call(
        _mla_prep_kernel,
        grid=(n // tm,),
        in_specs=[pl.BlockSpec((tm, 512), lambda i: (i, COL_MLA // 512)),
                  pl.BlockSpec((tm, 128), lambda i: (i % n_tab, 0)),
                  pl.BlockSpec((tm, 128), lambda i: (i % n_tab, 0)),
                  full((1, MLA_Q_RANK)), full((MLA_Q_RANK, 384)), full((1, MLA_KV_RANK)),
                  full((MLA_KV_RANK, 256)), full((256, 512)), full((1, 128)),
                  full((384, 128)), full((8, 384))],
        out_specs=[pl.BlockSpec((tm, MLA_KV_RANK), lambda i: (i, 0)),
                   pl.BlockSpec((tm, MLA_ROPE), lambda i: (i, 0)),
                   pl.BlockSpec((tm, 256), lambda i: (i, 0)),
                   pl.BlockSpec((8, tm), lambda i: (0, i)),
                   pl.BlockSpec((tm, 1024), lambda i: (i, 0))],
        out_shape=[jax.ShapeDtypeStruct((n, MLA_KV_RANK), F32),
                   jax.ShapeDtypeStruct((n, MLA_ROPE), F32),
                   jax.ShapeDtypeStruct((n, 256), BF16),
                   jax.ShapeDtypeStruct((8, n), F32),
                   jax.ShapeDtypeStruct((n, 1024), BF16)],
        compiler_params=_cparams(("arbitrary",)),
        name="mla_prep",
    )(z2d, cos_t, sin_t, pw["gq_norm"], pw["wuq"], pw["gkv_norm"], pw["wuk"], pw["wabs"], pw["grope"],
      pw["indq"], pw["indk"])


def _mla_prompt_kernel(qp_ref, kv_ref, rk_ref, wuv_ref, y_ref, *, tq):
    i = pl.program_id(1)
    row = lax.broadcasted_iota(jnp.int32, (tq, tq), 0)
    col = lax.broadcasted_iota(jnp.int32, (tq, tq), 1)
    for h in range(N_HEADS):
        q = qp_ref[:, h * 256:(h + 1) * 256]

        def step(j, carry, h=h, q=q):
            m, l, acc = carry
            k0 = pl.multiple_of(j * tq, tq)
            kv = kv_ref[pl.ds(k0, tq), :]
            s = _dot_nt(q, kv) * rk_ref[h:h + 1, pl.ds(k0, tq)]
            s = jnp.where(jnp.logical_or(j < i, col <= row), s, NEG)
            m_new = jnp.maximum(m, jnp.max(s, axis=-1, keepdims=True))
            alpha = jnp.exp(m - m_new)
            p = jnp.exp(s - m_new)
            l = alpha * l + jnp.sum(p, axis=-1, keepdims=True)
            acc = alpha * acc + _dot(p.astype(BF16), kv[:, 0:MLA_KV_RANK])
            return m_new, l, acc

        init = (jnp.full((tq, 1), NEG, F32), jnp.zeros((tq, 1), F32), jnp.zeros((tq, MLA_KV_RANK), F32))
        _, l, acc = lax.fori_loop(0, i + 1, step, init)
        o = (acc / l).astype(BF16)
        y_ref[:, h * HEAD_W:(h + 1) * HEAD_W] = _dot(o, wuv_ref[:, h * HEAD_W:(h + 1) * HEAD_W]).astype(BF16)


def _mla_prompt(qp, kv, rk, wuv, b, t, tq):
    n = b * t
    nq = t // tq
    return pl.pallas_call(
        functools.partial(_mla_prompt_kernel, tq=tq),
        grid=(b, nq),
        in_specs=[pl.BlockSpec((tq, 1024), lambda bi, i: (bi * nq + i, 0)),
                  pl.BlockSpec((t, 256), lambda bi, i: (bi, 0)),
                  pl.BlockSpec((8, t), lambda bi, i: (0, bi)),
                  pl.BlockSpec((MLA_KV_RANK, N_HEADS * HEAD_W), lambda bi, i: (0, 0))],
        out_specs=pl.BlockSpec((tq, N_HEADS * HEAD_W), lambda bi, i: (bi * nq + i, 0)),
        out_shape=jax.ShapeDtypeStruct((n, N_HEADS * HEAD_W), BF16),
        compiler_params=_cparams(("arbitrary", "arbitrary")),
        name="mla_prompt",
    )(qp, kv, rk, wuv)


def _mla_sample_kernel(pt_ref, qp_ref, cnew_ref, rnewt_ref, wukt_ref, wuv_ref, lat_hbm, ropet_hbm,
                       y_ref, cbuf, rbuf, cbf, s_scr, sem, *, layer, n_pages, ck):
    b = pl.program_id(0)
    n_b = pl.num_programs(0)
    past = n_pages * PAGE
    tk = past + PAGE
    slot = b % 2
    tp = SAMPLE_PAD_T
    nrow = N_HEADS * tp

    def lat_copy(pg, p, sl):
        return pltpu.make_async_copy(lat_hbm.at[layer, pg], cbuf.at[sl, pl.ds(p * PAGE, PAGE), :], sem.at[0, sl])

    def rope_copy(pg, p, sl):
        return pltpu.make_async_copy(ropet_hbm.at[layer, pg], rbuf.at[sl, :, pl.ds(p * PAGE, PAGE)], sem.at[1, sl])

    def issue(seq, sl):
        def body(p, carry):
            pg = pt_ref[seq * n_pages + p]
            lat_copy(pg, p, sl).start()
            rope_copy(pg, p, sl).start()
            return carry
        lax.fori_loop(0, n_pages, body, 0, unroll=4)

    @pl.when(b == 0)
    def _():
        for sl in range(2):
            cbuf[sl, pl.ds(past, PAGE), :] = jnp.zeros((PAGE, MLA_KV_RANK), F32)
        issue(0, 0)

    @pl.when(b + 1 < n_b)
    def _():
        issue(b + 1, 1 - slot)

    def wait_body(p, carry):
        lat_copy(0, p, slot).wait()
        rope_copy(0, p, slot).wait()
        return carry
    lax.fori_loop(0, n_pages, wait_body, 0, unroll=4)

    cbuf[slot, pl.ds(past, tp), :] = cnew_ref[0]
    rbuf[slot, :, pl.ds(past, PAGE)] = rnewt_ref[0]

    q = qp_ref[0]
    qr = q[:, MLA_KV_RANK:MLA_KV_RANK + MLA_ROPE]
    w_stack = jnp.concatenate([wukt_ref[...], q[:, 0:MLA_KV_RANK]], axis=0)
    n_up = N_HEADS * MLA_NOPE
    qpos = past + lax.broadcasted_iota(jnp.int32, (nrow, 1), 0) % tp

    for k0 in range(0, tk, ck):
        cb = cbuf[slot, pl.ds(k0, ck), :].astype(BF16)
        cbf[pl.ds(k0, ck), :] = cb
        rt = rbuf[slot, :, pl.ds(k0, ck)]
        big = _dot_nt(w_stack, cb)
        kn2 = big[0:n_up] * big[0:n_up]
        ssr = jnp.sum(rt * rt, axis=0, keepdims=True)
        rk_rows = []
        for h in range(N_HEADS):
            ss = jnp.sum(kn2[h * MLA_NOPE:(h + 1) * MLA_NOPE], axis=0, keepdims=True) + ssr
            rk_rows.append(jnp.broadcast_to(lax.rsqrt(ss * (1.0 / MLA_QK) + EPS), (tp, ck)))
        s = (big[n_up:n_up + nrow] + _dot(qr, rt.astype(BF16))) * jnp.concatenate(rk_rows, axis=0)
        kpos = k0 + lax.broadcasted_iota(jnp.int32, (1, ck), 1)
        s_scr[:, pl.ds(k0, ck)] = jnp.where(kpos <= qpos, s, NEG)

    s = s_scr[...]
    m = jnp.max(s, axis=-1, keepdims=True)
    p = jnp.exp(s - m)
    l = jnp.sum(p, axis=-1, keepdims=True)
    o = _dot(p.astype(BF16), cbf[...]) / l
    for h in range(N_HEADS):
        y_ref[0, :, h * HEAD_W:(h + 1) * HEAD_W] = _dot(
            o[h * tp:(h + 1) * tp].astype(BF16), wuv_ref[:, h * HEAD_W:(h + 1) * HEAD_W]).astype(BF16)


def _mla_sample(page_flat, qp3, c3, rt3, wukt, wuv, lat, ropet, layer, n_pages):
    bs = qp3.shape[0]
    tp = SAMPLE_PAD_T
    nrow = N_HEADS * tp
    tk = n_pages * PAGE + PAGE
    ck = max(d for d in range(PAGE, 768 + 1, PAGE) if tk % d == 0)
    kern = functools.partial(_mla_sample_kernel, layer=layer, n_pages=n_pages, ck=ck)
    grid_spec = pltpu.PrefetchScalarGridSpec(
        num_scalar_prefetch=1,
        grid=(bs,),
        in_specs=[pl.BlockSpec((1, nrow, 256), lambda b, pt: (b, 0, 0)),
                  pl.BlockSpec((1, tp, MLA_KV_RANK), lambda b, pt: (b, 0, 0)),
                  pl.BlockSpec((1, MLA_ROPE, PAGE), lambda b, pt: (b, 0, 0)),
                  pl.BlockSpec((N_HEADS * MLA_NOPE, MLA_KV_RANK), lambda b, pt: (0, 0)),
                  pl.BlockSpec((MLA_KV_RANK, N_HEADS * HEAD_W), lambda b, pt: (0, 0)),
                  pl.BlockSpec(memory_space=pl.ANY),
                  pl.BlockSpec(memory_space=pl.ANY)],
        out_specs=pl.BlockSpec((1, tp, N_HEADS * HEAD_W), lambda b, pt: (b, 0, 0)),
        scratch_shapes=[pltpu.VMEM((2, tk, MLA_KV_RANK), F32),
                        pltpu.VMEM((2, MLA_ROPE, tk), F32),
                        pltpu.VMEM((tk, MLA_KV_RANK), BF16),
                        pltpu.VMEM((nrow, tk), F32),
                        pltpu.SemaphoreType.DMA((2, 2))],
    )
    return pl.pallas_call(
        kern,
        grid_spec=grid_spec,
        out_shape=jax.ShapeDtypeStruct((bs, tp, N_HEADS * HEAD_W), BF16),
        compiler_params=_cparams(("arbitrary",)),
        name="mla_sample",
    )(page_flat, qp3, c3, rt3, wukt, wuv, lat, ropet)


def _memkv_kernel(x_ref, g_ref, wk_ref, wv_ref, gk_ref, k_ref, v_ref):
    mn = _rms(x_ref[...], g_ref[...]).astype(BF16)
    k = _dot(mn, wk_ref[...])
    gk = gk_ref[...]
    tm = x_ref.shape[0]
    v = _dot(mn, wv_ref[...])
    for h in range(N_HEADS):
        sl = slice(h * HEAD_W, (h + 1) * HEAD_W)
        k_ref[pl.ds(h, tm, stride=N_HEADS), :] = _rms(k[:, sl], gk)
        v_ref[pl.ds(h, tm, stride=N_HEADS), :] = v[:, sl]


def _memkv(mem2d, g, wk, wv, gk, tm):
    n, d = mem2d.shape
    br = wk.shape[1]
    return pl.pallas_call(
        _memkv_kernel,
        grid=(n // tm,),
        in_specs=[pl.BlockSpec((tm, d), lambda i: (i, 0)),
                  pl.BlockSpec((1, d), lambda i: (0, 0)),
                  pl.BlockSpec((d, br), lambda i: (0, 0)),
                  pl.BlockSpec((d, br), lambda i: (0, 0)),
                  pl.BlockSpec((1, HEAD_W), lambda i: (0, 0))],
        out_specs=[pl.BlockSpec((tm * N_HEADS, HEAD_W), lambda i: (i, 0)),
                   pl.BlockSpec((tm * N_HEADS, HEAD_W), lambda i: (i, 0))],
        out_shape=[jax.ShapeDtypeStruct((n * N_HEADS, HEAD_W), F32),
                   jax.ShapeDtypeStruct((n * N_HEADS, HEAD_W), F32)],
        compiler_params=_cparams(("arbitrary",)),
        name="memkv",
    )(mem2d, g.reshape(1, d), wk, wv, gk.reshape(1, HEAD_W))


def _mem_attn_kernel(q_ref, k_ref, v_ref, gq_ref, y_ref, *, sb, m):
    gq = gq_ref[...]
    scale = HEAD_W ** -0.5
    for s in range(sb):
        q = q_ref[s].astype(F32)
        for h in range(N_HEADS):
            sl = slice(h * HEAD_W, (h + 1) * HEAD_W)
            qh = (_rms(q[:, sl], gq) * scale).astype(BF16)
            sc = _dot_nt(qh, k_ref[s, pl.ds(h, m, stride=N_HEADS), :].astype(BF16))
            mx = jnp.max(sc, axis=-1, keepdims=True)
            p = jnp.exp(sc - mx)
            l = jnp.sum(p, axis=-1, keepdims=True)
            o = _dot(p.astype(BF16), v_ref[s, pl.ds(h, m, stride=N_HEADS), :].astype(BF16)) / l
            y_ref[s, :, sl] = o.astype(BF16)


def _mem_attn(z3d, k4, v4, gq, layer, sb, tq):
    b, t, _ = z3d.shape
    m = k4.shape[2] // N_HEADS
    br = N_HEADS * HEAD_W
    return pl.pallas_call(
        functools.partial(_mem_attn_kernel, sb=sb, m=m),
        grid=(b // sb, t // tq),
        in_specs=[pl.BlockSpec((sb, tq, br), lambda i, j: (i, j, COL_MEMQ // br)),
                  pl.BlockSpec((None, sb, m * N_HEADS, HEAD_W), lambda i, j: (layer, i, 0, 0)),
                  pl.BlockSpec((None, sb, m * N_HEADS, HEAD_W), lambda i, j: (layer, i, 0, 0)),
                  pl.BlockSpec((1, HEAD_W), lambda i, j: (0, 0))],
        out_specs=pl.BlockSpec((sb, tq, br), lambda i, j: (i, j, 0)),
        out_shape=jax.ShapeDtypeStruct((b, t, br), BF16),
        compiler_params=_cparams(("arbitrary", "arbitrary")),
        name="mem_attn",
    )(z3d, k4, v4, gq.reshape(1, HEAD_W))


def _outproj_kernel(x_ref, mg_ref, sg_ref, y0_ref, y1_ref, y2_ref, y3_ref, wb_ref, wo_ref, o_ref):
    d = x_ref.shape[1]
    br = y0_ref.shape[1]
    merged = None
    for n, y_ref in enumerate((y0_ref, y1_ref, y2_ref, y3_ref)):
        sg = sg_ref[:, n * br:(n + 1) * br].astype(F32)
        ys = (y_ref[...].astype(F32) * (sg * jax.nn.sigmoid(sg))).astype(BF16)
        proj = _dot(ys, wb_ref[n])
        term = jax.nn.sigmoid(mg_ref[:, n * d:(n + 1) * d].astype(F32)) * proj
        merged = term if merged is None else merged + term
    o_ref[...] = x_ref[...] + _dot(merged.astype(BF16), wo_ref[...])


def _outproj(x2d, z2d, ys, wb, wo, tm):
    n, d = x2d.shape
    br = ys[0].shape[1]
    nb = wb.shape[0]
    yspec = pl.BlockSpec((tm, br), lambda i: (i, 0))
    return pl.pallas_call(
        _outproj_kernel,
        grid=(n // tm,),
        in_specs=[pl.BlockSpec((tm, d), lambda i: (i, 0)),
                  pl.BlockSpec((tm, nb * d), lambda i: (i, COL_MG // (nb * d))),
                  pl.BlockSpec((tm, nb * br), lambda i: (i, COL_SG // (nb * br))),
                  yspec, yspec, yspec, yspec,
                  pl.BlockSpec((nb, br, d), lambda i: (0, 0, 0)),
                  pl.BlockSpec((d, d), lambda i: (0, 0))],
        out_specs=pl.BlockSpec((tm, d), lambda i: (i, 0)),
        out_shape=jax.ShapeDtypeStruct((n, d), F32),
        compiler_params=_cparams(("arbitrary",)),
        name="outproj",
    )(x2d, z2d, z2d, *ys, wb, wo)


def _rope_tables(pos):
    half = MLA_ROPE // 2
    freqs = ROPE_THETA ** (-np.arange(half, dtype=np.float64) / half)
    ang = np.asarray(pos, np.float64)[:, None] * freqs[None, :]
    cos = np.tile(np.cos(ang), (1, 8))
    sin = np.tile(np.sin(ang), (1, 8))
    sin[:, :64] *= -1.0
    return jnp.asarray(cos, F32), jnp.asarray(sin, F32)


def _relayout_w_in(w):
    d = w.shape[0]
    z = lambda n: jnp.zeros((d, n), w.dtype)
    o_q, o_kv, o_pe, o_mq, o_sg, o_mg = 3072, 3264, 3392, 3424, 3936, 5984
    half = MLA_ROPE // 2
    cols = [w[:, :o_q],
            w[:, o_q:o_kv], z(64),
            w[:, o_kv:o_pe],
            w[:, o_pe:o_pe + half], z(48), w[:, o_pe + half:o_mq], z(48),
            w[:, o_mq:o_sg],
            w[:, o_mg:],
            w[:, o_sg:o_mg]]
    return jnp.concatenate(cols, axis=1).astype(BF16)


def _mla_params(q_norm, w_uq, kv_norm, w_uk, gq, gk):
    half = MLA_ROPE // 2
    wq = w_uq.reshape(MLA_Q_RANK, N_HEADS, MLA_QK)
    wuq = jnp.concatenate([wq[:, :, :MLA_NOPE].reshape(MLA_Q_RANK, -1),
                           wq[:, :, MLA_NOPE:MLA_NOPE + half].reshape(MLA_Q_RANK, -1),
                           wq[:, :, MLA_NOPE + half:].reshape(MLA_Q_RANK, -1)], axis=1).astype(BF16)
    g2 = gq * gk
    wk = w_uk.reshape(MLA_KV_RANK, N_HEADS, MLA_NOPE)
    wabs = jnp.zeros((N_HEADS * MLA_NOPE, N_HEADS * MLA_KV_RANK), F32)
    for h in range(N_HEADS):
        blk = (wk[:, h, :] * g2[None, :MLA_NOPE]).T
        wabs = wabs.at[h * MLA_NOPE:(h + 1) * MLA_NOPE, h * MLA_KV_RANK:(h + 1) * MLA_KV_RANK].set(blk)
    grope = jnp.zeros((1, 128), F32).at[0, :MLA_ROPE].set(g2[MLA_NOPE:])
    j = np.arange(384)
    head_of = np.where(j < 256, j // MLA_NOPE, (j % 64) // half)
    indq = (head_of[:, None] == np.arange(128)[None, :]).astype(np.float32)
    lane = j - 256
    is_rope = (j >= 256) & ((lane < half) | ((lane >= 64) & (lane < 64 + half)))
    indk = np.zeros((8, 384), np.float32)
    for h in range(N_HEADS):
        indk[h] = ((j < 256) & (j // MLA_NOPE == h)) | is_rope
    return dict(gq_norm=q_norm.reshape(1, -1), wuq=wuq, gkv_norm=kv_norm.reshape(1, -1),
                wuk=w_uk.astype(BF16), wukt=w_uk.T.astype(BF16), wabs=wabs.astype(BF16), grope=grope,
                indq=jnp.asarray(indq, BF16), indk=jnp.asarray(indk, BF16))


def _hgrn_lower_bounds(lb_param):
    p = jax.nn.softmax(lb_param.astype(F32), axis=0)
    return jnp.cumsum(p, axis=0) - p[0]


def kernel(x_prompt, x_sample, mem_prompt, cache_mla_latent, cache_mla_rope, page_table, state_hgrn, state_conv, cache_mem_k, cache_mem_v, norm_gain, w_in, conv_w, hgrn_lb, hgrn_norm, mla_q_norm, mla_w_uq, mla_kv_norm, mla_w_uk, mla_w_uv, mla_q_gain, mla_k_gain, mem_norm, mem_w_k, mem_w_v, mem_q_gain, mem_k_gain, w_branch_out, w_out):
    bp, tp, d = x_prompt.shape
    bs, ts, _ = x_sample.shape
    depth = w_in.shape[0]
    br = conv_w.shape[2]
    mem_len = mem_prompt.shape[1]
    n_pages = page_table.shape[1]
    past = n_pages * cache_mla_latent.shape[2]
    tpad = SAMPLE_PAD_T
    n_p = bp * tp
    n_s = bs * tpad

    lbs = _hgrn_lower_bounds(hgrn_lb)
    cos_p, sin_p = _rope_tables(np.arange(tp))
    cos_s, sin_s = _rope_tables(past + np.arange(tpad))
    tm_s = min(n_s, 1024)
    cos_s = jnp.tile(cos_s, (tm_s // tpad, 1))
    sin_s = jnp.tile(sin_s, (tm_s // tpad, 1))
    page_flat = page_table.reshape(-1).astype(jnp.int32)
    rope_t = jnp.swapaxes(cache_mla_rope, 2, 3)
    mem_k4 = cache_mem_k.reshape(depth, bs, mem_len * N_HEADS, HEAD_W)
    mem_v4 = cache_mem_v.reshape(depth, bs, mem_len * N_HEADS, HEAD_W)

    tm_p = min(tp, 512)
    xp = x_prompt.reshape(n_p, d)
    xs = jnp.pad(x_sample, ((0, 0), (0, tpad - ts), (0, 0))).reshape(n_s, d)
    mem2d = mem_prompt.reshape(bp * mem_len, d)
    zero_state = jnp.zeros((1, bp, N_HEADS, HEAD_W, HEAD_W), F32)

    outs = {k: [] for k in ("p_lat", "p_rope", "p_hg", "p_conv", "p_mk", "p_mv", "s_lat", "s_rope", "s_hg", "s_conv")}
    for l in range(depth):
        w_in_l = _relayout_w_in(w_in[l])
        pw = _mla_params(mla_q_norm[l], mla_w_uq[l], mla_kv_norm[l], mla_w_uk[l], mla_q_gain[l], mla_k_gain[l])
        wuv = mla_w_uv[l].astype(BF16)
        wb = w_branch_out[l].astype(BF16)
        wo = w_out[l].astype(BF16)

        z = _inproj(xp, norm_gain[l], w_in_l, min(n_p, 1024), 2048)
        z3 = z.reshape(bp, tp, N_COLS)
        y_conv, tail = _conv(z, None, conv_w[l], tp, tm_p, 8)
        y_hg, s_fin = _hgrn(z3, lbs[l], hgrn_norm[l], zero_state, 0, 1, tm_p, HG_CHUNK, None)
        c, r, kv, rk, qp = _mla_prep(z, cos_p, sin_p, pw, tm_p)
        y_mla = _mla_prompt(qp, kv, rk, wuv, bp, tp, min(tp, 256))
        mk, mv = _memkv(mem2d, mem_norm[l], mem_w_k[l].astype(BF16), mem_w_v[l].astype(BF16), mem_k_gain[l],
                        min(bp * mem_len, 512))
        rows_m = mem_len * N_HEADS
        y_mem = _mem_attn(z3, mk.reshape(1, bp, rows_m, HEAD_W), mv.reshape(1, bp, rows_m, HEAD_W), mem_q_gain[l],
                          0, 1, tm_p)
        xp = _outproj(xp, z, (y_conv, y_hg.reshape(n_p, br), y_mla, y_mem.reshape(n_p, br)), wb, wo, tm_p)
        outs["p_lat"].append(c.reshape(bp, tp, MLA_KV_RANK))
        outs["p_rope"].append(r.reshape(bp, tp, MLA_ROPE))
        outs["p_hg"].append(s_fin)
        outs["p_conv"].append(tail.reshape(bp, 8, br)[:, 8 - (CONV_K - 1):])
        outs["p_mk"].append(mk.reshape(bp, mem_len, N_HEADS, HEAD_W))
        outs["p_mv"].append(mv.reshape(bp, mem_len, N_HEADS, HEAD_W))

        z = _inproj(xs, norm_gain[l], w_in_l, tm_s, 2048)
        z3 = z.reshape(bs, tpad, N_COLS)
        hist = jnp.pad(state_conv[l], ((0, 0), (0, tpad - (CONV_K - 1)), (0, 0))).reshape(n_s, br)
        y_conv, u_all = _conv(z, hist, conv_w[l], tpad, tm_s, tm_s)
        sb = 8 if bs % 8 == 0 else 1
        y_hg, s_fin = _hgrn(z3, lbs[l], hgrn_norm[l], state_hgrn, l, sb, tpad, tpad, ts)
        c, r, kv, rk, qp = _mla_prep(z, cos_s, sin_s, pw, tm_s)
        qp3 = qp.reshape(bs, tpad, N_HEADS, 256).transpose(0, 2, 1, 3).reshape(bs, N_HEADS * tpad, 256)
        rt3 = jnp.pad(r.reshape(bs, tpad, MLA_ROPE).transpose(0, 2, 1), ((0, 0), (0, 0), (0, PAGE - tpad)))
        y_mla = _mla_sample(page_flat, qp3, c.reshape(bs, tpad, MLA_KV_RANK), rt3, pw["wukt"], wuv,
                            cache_mla_latent, rope_t, l, n_pages)
        y_mem = _mem_attn(z3, mem_k4, mem_v4, mem_q_gain[l], l, sb, tpad)
        xs = _outproj(xs, z, (y_conv, y_hg.reshape(n_s, br), y_mla.reshape(n_s, br), y_mem.reshape(n_s, br)),
                      wb, wo, tm_s)
        outs["s_lat"].append(c.reshape(bs, tpad, MLA_KV_RANK)[:, :ts])
        outs["s_rope"].append(r.reshape(bs, tpad, MLA_ROPE)[:, :ts])
        outs["s_hg"].append(s_fin)
        outs["s_conv"].append(u_all.reshape(bs, tpad, br)[:, ts - (CONV_K - 1):ts])

    st = lambda k: jnp.stack(outs[k])
    return (xp.reshape(bp, tp, d), xs.reshape(bs, tpad, d)[:, :ts], st("p_lat"), st("p_rope"), st("p_hg"),
            st("p_conv"), st("p_mk"), st("p_mv"), st("s_lat"), st("s_rope"), st("s_hg"), st("s_conv"))
```

```python
import functools

import numpy as np
import jax
import jax.numpy as jnp
from jax import lax
from jax.experimental import pallas as pl
from jax.experimental.pallas import tpu as pltpu

F32 = jnp.float32
BF16 = jnp.bfloat16

N_HEADS = 4
HEAD_W = 128
MLA_NOPE = 64
MLA_ROPE = 32
MLA_QK = MLA_NOPE + MLA_ROPE
MLA_Q_RANK = 192
MLA_KV_RANK = 128
CONV_K = 3
ROPE_THETA = 10000.0
EPS = 1e-6
NEG = -1e30
PAGE = 128
SAMPLE_PAD_T = 8
HG_CHUNK = 64
HG_BLOCK = 8
HG_UNROLL = 8
MLA_ROW_GROUP = 128
MLA_TQ = 512
LOG2E = 1.4426950408889634
VMEM_LIMIT = 56 * 1024 * 1024

COL_CONV = 0
COL_HG = 1536
COL_MLA = 3072
COL_MEMQ = 3584
COL_MG = 4096
COL_SG = 8192
N_COLS = 10240


def _cparams(sem):
    return pltpu.CompilerParams(dimension_semantics=sem, vmem_limit_bytes=VMEM_LIMIT)


def _rms(x, g):
    return x * lax.rsqrt(jnp.mean(x * x, axis=-1, keepdims=True) + EPS) * g


def _dot(a, b):
    return jnp.dot(a, b, preferred_element_type=F32)


def _dot_nt(a, b):
    return lax.dot_general(a, b, (((1,), (1,)), ((), ())), preferred_element_type=F32)


def _dot_tn(a, b):
    return lax.dot_general(a, b, (((0,), (0,)), ((), ())), preferred_element_type=F32)


def _inproj_kernel(x_ref, g_ref, w_ref, z_ref, hn_ref):
    @pl.when(pl.program_id(1) == 0)
    def _():
        hn_ref[...] = _rms(x_ref[...], g_ref[...]).astype(BF16)

    z_ref[...] = _dot(hn_ref[...], w_ref[...]).astype(BF16)


def _inproj(x2d, g, w_bf, tm, tn):
    n, d = x2d.shape
    ncol = w_bf.shape[1]
    return pl.pallas_call(
        _inproj_kernel,
        grid=(n // tm, ncol // tn),
        in_specs=[pl.BlockSpec((tm, d), lambda i, j: (i, 0)),
                  pl.BlockSpec((1, d), lambda i, j: (0, 0)),
                  pl.BlockSpec((d, tn), lambda i, j: (0, j))],
        out_specs=pl.BlockSpec((tm, tn), lambda i, j: (i, j)),
        out_shape=jax.ShapeDtypeStruct((n, ncol), BF16),
        scratch_shapes=[pltpu.VMEM((tm, d), BF16)],
        compiler_params=_cparams(("arbitrary", "arbitrary")),
        name="inproj",
    )(x2d, g.reshape(1, d), w_bf)


def _conv_kernel(*refs, seq_t, tr, tail_rows, has_hist):
    if has_hist:
        z_ref, halo_ref, hist_ref, w_ref, y_ref, tail_ref = refs
    else:
        z_ref, halo_ref, w_ref, y_ref, tail_ref = refs
    i = pl.program_id(0)
    br = y_ref.shape[1]
    z = z_ref[...].astype(F32)
    u = z[:, 2 * br:3 * br] * z[:, 0:br]
    zh = halo_ref[...].astype(F32)
    uh = zh[:, 2 * br:3 * br] * zh[:, 0:br]
    loc = lax.broadcasted_iota(jnp.int32, (tr, 1), 0)
    t = (loc + i * tr) % seq_t
    u1 = jnp.where(loc == 0, uh[7:8], pltpu.roll(u, 1, axis=0))
    u2 = jnp.where(loc == 0, uh[6:7], jnp.where(loc == 1, uh[7:8], pltpu.roll(u, 2, axis=0)))
    if has_hist:
        hp = hist_ref[...]
        u1 = jnp.where(t == 0, pltpu.roll(hp, tr - 1, axis=0), u1)
        u2 = jnp.where(t < 2, hp, u2)
    else:
        u1 = jnp.where(t == 0, 0.0, u1)
        u2 = jnp.where(t < 2, 0.0, u2)
    w = w_ref[...]
    conv = w[0:1] * u2 + w[1:2] * u1 + w[2:3] * u
    y_ref[...] = (z[:, br:2 * br] * conv).astype(BF16)

    @pl.when(((i + 1) * tr) % max(seq_t, tr) == 0)
    def _():
        tail_ref[...] = u[tr - tail_rows:, :]


def _conv(z2d, hist_rows, w, seq_t, tr, tail_rows):
    n = z2d.shape[0]
    br = w.shape[1]
    has_hist = hist_rows is not None
    group = max(seq_t, tr)
    n_tail = (n // group) * tail_rows
    in_specs = [pl.BlockSpec((tr, 3 * br), lambda i: (i, 0)),
                pl.BlockSpec((8, 3 * br), lambda i: (jnp.maximum(i * (tr // 8) - 1, 0), 0))]
    args = [z2d, z2d]
    if has_hist:
        in_specs.append(pl.BlockSpec((tr, br), lambda i: (i, 0)))
        args.append(hist_rows)
    in_specs.append(pl.BlockSpec((CONV_K, br), lambda i: (0, 0)))
    args.append(w)
    return pl.pallas_call(
        functools.partial(_conv_kernel, seq_t=seq_t, tr=tr, tail_rows=tail_rows, has_hist=has_hist),
        grid=(n // tr,),
        in_specs=in_specs,
        out_specs=[pl.BlockSpec((tr, br), lambda i: (i, 0)),
                   pl.BlockSpec((tail_rows, br), lambda i: ((i * tr) // group, 0))],
        out_shape=[jax.ShapeDtypeStruct((n, br), BF16), jax.ShapeDtypeStruct((n_tail, br), F32)],
        compiler_params=_cparams(("arbitrary",)),
        name="conv",
    )(*args)


def _hgrn_chunk(q, zf, v, lb, st, valid, tril_bf, chunk):
    nb = chunk // HG_BLOCK
    f = lb + (1.0 - lb) * jax.nn.sigmoid(zf)
    g = jnp.log(f)
    kk = 1.0 - f
    if valid is not None:
        g = jnp.where(valid, g, 0.0)
        kk = jnp.where(valid, kk, 0.0)
    g_hi = g.astype(BF16)
    g_lo = (g - g_hi.astype(F32)).astype(BF16)
    b2 = (_dot(tril_bf, g_hi) + _dot(tril_bf, g_lo)) * LOG2E
    bl2 = b2[chunk - 1:chunk, :]
    vb = v.astype(BF16)

    o = _dot_nt((q * jnp.exp2(b2)).astype(BF16), st.astype(BF16))
    kd = (kk * jnp.exp2(bl2 - b2)).astype(BF16)
    st_new = st * jnp.exp2(bl2) + _dot_tn(vb, kd)

    lane = lax.broadcasted_iota(jnp.int32, (HG_BLOCK, chunk), 1)
    srow = lax.broadcasted_iota(jnp.int32, (HG_BLOCK, chunk), 0)
    at_rows = []
    for gb in range(nb):
        r = gb * HG_BLOCK
        b_blk = b2[r:r + HG_BLOCK]
        k_blk = kk[r:r + HG_BLOCK]
        acc = jnp.zeros((HG_BLOCK, chunk), F32)
        for t in range(HG_BLOCK):
            p = jnp.exp2(jnp.minimum(b2[r + t:r + t + 1] - b_blk, 0.0)) * (q[r + t:r + t + 1] * k_blk)
            acc = jnp.where(lane == r + t, jnp.sum(p, axis=-1, keepdims=True), acc)
        at_rows.append(jnp.where(lane - r >= srow, acc, 0.0))
    at = at_rows[0] if nb == 1 else jnp.concatenate(at_rows, axis=0)
    if nb > 1:
        zero = jnp.zeros((HG_BLOCK, HEAD_W), F32)
        q_rows, k_rows = [], []
        for i in range(nb):
            blk = slice(i * HG_BLOCK, (i + 1) * HG_BLOCK)
            q_tiles, k_tiles = [], []
            for j in range(nb - 1):
                rho = b2[(j + 1) * HG_BLOCK - 1:(j + 1) * HG_BLOCK]
                q_tiles.append(q[blk] * jnp.exp2(b2[blk] - rho) if j < i else zero)
                k_tiles.append(kk[blk] * jnp.exp2(rho - b2[blk]) if j == i else zero)
            q_rows.append(jnp.concatenate(q_tiles, axis=1))
            k_rows.append(jnp.concatenate(k_tiles, axis=1))
        q_hat = jnp.concatenate(q_rows, axis=0).astype(BF16)
        k_hat = jnp.concatenate(k_rows, axis=0).astype(BF16)
        at = at + _dot_nt(k_hat, q_hat)
    o = o + _dot_tn(at.astype(BF16), vb)
    return o, st_new


def _hgrn_kernel(q_ref, f_ref, i_ref, lb_ref, g_ref, s0_ref, y_ref, sout_ref, st_ref, *, sb, tt, chunk, t_valid):
    tstep = pl.program_id(2)
    n_t = pl.num_programs(2)

    @pl.when(tstep == 0)
    def _():
        for s in range(sb):
            st_ref[s] = s0_ref[s, 0].T

    lb = lb_ref[0]
    gain = g_ref[0]
    row = lax.broadcasted_iota(jnp.int32, (chunk, chunk), 0)
    col = lax.broadcasted_iota(jnp.int32, (chunk, chunk), 1)
    tril_bf = (row >= col).astype(BF16)
    scale = HEAD_W ** -0.5

    for s in range(sb):
        def body(c, carry, s=s):
            r0 = pl.multiple_of(c * chunk, chunk)
            q = q_ref[s, pl.ds(r0, chunk), :].astype(F32) * scale
            zf = f_ref[s, pl.ds(r0, chunk), :].astype(F32)
            v = i_ref[s, pl.ds(r0, chunk), :].astype(F32)
            valid = None
            if t_valid is not None:
                valid = tstep * tt + r0 + lax.broadcasted_iota(jnp.int32, (chunk, 1), 0) < t_valid
            o, st_new = _hgrn_chunk(q, zf, v, lb, st_ref[s], valid, tril_bf, chunk)
            st_ref[s] = st_new
            y_ref[s, pl.ds(r0, chunk), :] = _rms(o, gain).astype(BF16)
            return carry

        lax.fori_loop(0, tt // chunk, body, 0, unroll=min(tt // chunk, HG_UNROLL))

    @pl.when(tstep == n_t - 1)
    def _():
        for s in range(sb):
            sout_ref[s, 0] = st_ref[s].T


def _hgrn(z3d, lb, gain, s0, layer, sb, tt, chunk, t_valid):
    b, t, _ = z3d.shape
    cb = COL_HG // HEAD_W
    kern = functools.partial(_hgrn_kernel, sb=sb, tt=tt, chunk=chunk, t_valid=t_valid)
    return pl.pallas_call(
        kern,
        grid=(b // sb, N_HEADS, t // tt),
        in_specs=[pl.BlockSpec((sb, tt, HEAD_W), lambda i, h, k: (i, k, cb + h)),
                  pl.BlockSpec((sb, tt, HEAD_W), lambda i, h, k: (i, k, cb + N_HEADS + h)),
                  pl.BlockSpec((sb, tt, HEAD_W), lambda i, h, k: (i, k, cb + 2 * N_HEADS + h)),
                  pl.BlockSpec((1, 1, HEAD_W), lambda i, h, k: (h, 0, 0)),
                  pl.BlockSpec((1, 1, HEAD_W), lambda i, h, k: (h, 0, 0)),
                  pl.BlockSpec((None, sb, 1, HEAD_W, HEAD_W), lambda i, h, k: (layer, i, h, 0, 0))],
        out_specs=[pl.BlockSpec((sb, tt, HEAD_W), lambda i, h, k: (i, k, h)),
                   pl.BlockSpec((sb, 1, HEAD_W, HEAD_W), lambda i, h, k: (i, h, 0, 0))],
        out_shape=[jax.ShapeDtypeStruct((b, t, N_HEADS * HEAD_W), BF16),
                   jax.ShapeDtypeStruct((b, N_HEADS, HEAD_W, HEAD_W), F32)],
        scratch_shapes=[pltpu.VMEM((sb, HEAD_W, HEAD_W), F32)],
        compiler_params=_cparams(("arbitrary", "arbitrary", "arbitrary")),
        name="hgrn",
    )(z3d, z3d, z3d, lb.reshape(N_HEADS, 1, HEAD_W), gain.reshape(N_HEADS, 1, HEAD_W), s0)


def _mla_prep_kernel(z_ref, cos_ref, sin_ref, gq_ref, wuq_ref, gkv_ref, wuk_ref, wabs_ref, grope_ref,
                     indq_ref, indk_ref, c_ref, r_ref, kv_ref, rk_ref, qp_ref):
    z = z_ref[...].astype(F32)
    ql = z[:, 0:MLA_Q_RANK]
    kvl = z[:, 256:384]
    kpe = z[:, 384:512]
    cos = cos_ref[...]
    sin = sin_ref[...]
    lane = lax.broadcasted_iota(jnp.int32, (1, 128), 1)

    qf = _dot(_rms(ql, gq_ref[...]).astype(BF16), wuq_ref[...])
    q_nope = qf[:, 0:256]
    rq_in = qf[:, 256:384]
    rot_q = rq_in * cos + pltpu.roll(rq_in, 64, axis=1) * sin
    rot_k = kpe * cos + pltpu.roll(kpe, 64, axis=1) * sin

    c = _rms(kvl, gkv_ref[...])
    c_ref[...] = c
    r32 = jnp.where(lane < 16, rot_k, pltpu.roll(rot_k, 80, axis=1))
    r32 = jnp.where(lane < MLA_ROPE, r32, 0.0)
    r_ref[...] = r32[:, 0:MLA_ROPE]
    cb = c.astype(BF16)
    kv_ref[...] = jnp.concatenate([cb, r32.astype(BF16)], axis=1)

    kn = _dot(cb, wuk_ref[...])
    kcat2 = jnp.concatenate([kn * kn, rot_k * rot_k], axis=1).astype(BF16)
    ssk = _dot_nt(indk_ref[...], kcat2)
    rk_ref[...] = lax.rsqrt(ssk * (1.0 / MLA_QK) + EPS)

    qcat2 = jnp.concatenate([q_nope * q_nope, rot_q * rot_q], axis=1).astype(BF16)
    ssq = _dot(qcat2, indq_ref[...])
    rq = lax.rsqrt(ssq * (1.0 / MLA_QK) + EPS) * (MLA_QK ** -0.5 * LOG2E)

    q_abs = _dot(q_nope.astype(BF16), wabs_ref[...])
    grope = grope_ref[...]
    for h in range(N_HEADS):
        a = rot_q if h == 0 else pltpu.roll(rot_q, 128 - 16 * h, axis=1)
        qr = jnp.where(lane < 16, a, pltpu.roll(a, 80, axis=1)) * grope
        rq_h = rq[:, h:h + 1]
        qp_ref[h, :, 0:128] = (q_abs[:, h * 128:(h + 1) * 128] * rq_h).astype(BF16)
        qp_ref[h, :, 128:256] = (qr * rq_h).astype(BF16)


def _mla_prep(z2d, cos_t, sin_t, pw, tm):
    n = z2d.shape[0]
    n_tab = cos_t.shape[0] // tm
    full = lambda shape: pl.BlockSpec(shape, lambda i: (0,) * len(shape))
    return pl.pallas_call(
        _mla_prep_kernel,
        grid=(n // tm,),
        in_specs=[pl.BlockSpec((tm, 512), lambda i: (i, COL_MLA // 512)),
                  pl.BlockSpec((tm, 128), lambda i: (i % n_tab, 0)),
                  pl.BlockSpec((tm, 128), lambda i: (i % n_tab, 0)),
                  full((1, MLA_Q_RANK)), full((MLA_Q_RANK, 384)), full((1, MLA_KV_RANK)),
                  full((MLA_KV_RANK, 256)), full((256, 512)), full((1, 128)),
                  full((384, 128)), full((8, 384))],
        out_specs=[pl.BlockSpec((tm, MLA_KV_RANK), lambda i: (i, 0)),
                   pl.BlockSpec((tm, MLA_ROPE), lambda i: (i, 0)),
                   pl.BlockSpec((tm, 256), lambda i: (i, 0)),
                   pl.BlockSpec((8, tm), lambda i: (0, i)),
                   pl.BlockSpec((N_HEADS, tm, 256), lambda i: (0, i, 0))],
        out_shape=[jax.ShapeDtypeStruct((n, MLA_KV_RANK), F32),
                   jax.ShapeDtypeStruct((n, MLA_ROPE), F32),
                   jax.ShapeDtypeStruct((n, 256), BF16),
                   jax.ShapeDtypeStruct((8, n), F32),
                   jax.ShapeDtypeStruct((N_HEADS, n, 256), BF16)],
        compiler_params=_cparams(("arbitrary",)),
        name="mla_prep",
    )(z2d, cos_t, sin_t, pw["gq_norm"], pw["wuq"], pw["gkv_norm"], pw["wuk"], pw["wabs"], pw["grope"],
      pw["indq"], pw["indk"])


def _mla_prompt_kernel(qp_ref, kv_ref, rk_ref, wuv_ref, y_ref, m_ref, l_ref, acc_ref, *, tq, rb):
    i = pl.program_id(1)
    nrow = N_HEADS * tq
    m_ref[...] = jnp.full((nrow, 128), NEG, F32)
    l_ref[...] = jnp.zeros((nrow, 128), F32)
    acc_ref[...] = jnp.zeros((nrow, MLA_KV_RANK), F32)
    row = lax.broadcasted_iota(jnp.int32, (rb, 128), 0)
    col = lax.broadcasted_iota(jnp.int32, (rb, 128), 1)
    lane_tiles = range(0, tq, 128)

    def block(k0, diagonal):
        kv = kv_ref[pl.ds(k0, tq), :]
        cv = kv[:, 0:MLA_KV_RANK]

        def qk(h):
            return _dot_nt(qp_ref[h], kv)

        def softmax_update(h, s_h):
            p_rows = []
            for r in range(0, tq, rb):
                rows = pl.ds(h * tq + r, rb)
                tiles = []
                for c in lane_tiles:
                    s = s_h[r:r + rb, c:c + 128] * rk_ref[h:h + 1, pl.ds(k0 + c, 128)]
                    if diagonal:
                        s = jnp.where(col + c <= row + r, s, NEG)
                    tiles.append(s)
                m_old = m_ref[rows, :]
                m_new = jnp.maximum(m_old, jnp.max(functools.reduce(jnp.maximum, tiles), axis=-1, keepdims=True))
                alpha = jnp.exp2(m_old - m_new)
                probs = [jnp.exp2(s - m_new) for s in tiles]
                l_ref[rows, :] = alpha * l_ref[rows, :] + jnp.sum(functools.reduce(jnp.add, probs), axis=-1,
                                                                  keepdims=True)
                m_ref[rows, :] = m_new
                acc_ref[rows, :] = alpha * acc_ref[rows, :]
                p_rows.append(jnp.concatenate([p.astype(BF16) for p in probs], axis=1))
            return jnp.concatenate(p_rows, axis=0)

        def pv(h, p_h):
            acc_ref[pl.ds(h * tq, tq), :] += _dot(p_h, cv)

        s0 = qk(0)
        s1 = qk(1)
        p0 = softmax_update(0, s0)
        s2 = qk(2)
        p1 = softmax_update(1, s1)
        pv(0, p0)
        s3 = qk(3)
        p2 = softmax_update(2, s2)
        pv(1, p1)
        p3 = softmax_update(3, s3)
        pv(2, p2)
        pv(3, p3)

    def step(j, carry):
        block(pl.multiple_of(j * tq, tq), False)
        return carry

    lax.fori_loop(0, i, step, 0)
    block(pl.multiple_of(i * tq, tq), True)
    for h in range(N_HEADS):
        rows = pl.ds(h * tq, tq)
        o = (acc_ref[rows, :] / l_ref[rows, :]).astype(BF16)
        y_ref[:, h * HEAD_W:(h + 1) * HEAD_W] = _dot(o, wuv_ref[:, h * HEAD_W:(h + 1) * HEAD_W]).astype(BF16)


def _mla_prompt(qp, kv, rk, wuv, b, t, tq):
    n = b * t
    nq = t // tq
    nrow = N_HEADS * tq
    return pl.pallas_call(
        functools.partial(_mla_prompt_kernel, tq=tq, rb=min(tq, MLA_ROW_GROUP)),
        grid=(b, nq),
        scratch_shapes=[pltpu.VMEM((nrow, 128), F32), pltpu.VMEM((nrow, 128), F32),
                        pltpu.VMEM((nrow, MLA_KV_RANK), F32)],
        in_specs=[pl.BlockSpec((N_HEADS, tq, 256), lambda bi, i: (0, bi * nq + i, 0)),
                  pl.BlockSpec((t, 256), lambda bi, i: (bi, 0)),
                  pl.BlockSpec((8, t), lambda bi, i: (0, bi)),
                  pl.BlockSpec((MLA_KV_RANK, N_HEADS * HEAD_W), lambda bi, i: (0, 0))],
        out_specs=pl.BlockSpec((tq, N_HEADS * HEAD_W), lambda bi, i: (bi * nq + i, 0)),
        out_shape=jax.ShapeDtypeStruct((n, N_HEADS * HEAD_W), BF16),
        compiler_params=_cparams(("arbitrary", "arbitrary")),
        name="mla_prompt",
    )(qp, kv, rk, wuv)


def _mla_sample_kernel(pt_ref, qp_ref, cnew_ref, rnewt_ref, wukt_ref, wuv_ref, lat_hbm, ropet_hbm,
                       y_ref, cbuf, rbuf, cbf, s_scr, part_scr, sem, *, layer, n_pages, ck):
    b = pl.program_id(0)
    n_b = pl.num_programs(0)
    past = n_pages * PAGE
    tk = past + PAGE
    slot = b % 2
    tp = SAMPLE_PAD_T
    nrow = N_HEADS * tp

    def lat_copy(pg, p, sl):
        return pltpu.make_async_copy(lat_hbm.at[layer, pg], cbuf.at[sl, pl.ds(p * PAGE, PAGE), :], sem.at[0, sl])

    def rope_copy(pg, p, sl):
        return pltpu.make_async_copy(ropet_hbm.at[layer, pg], rbuf.at[sl, :, pl.ds(p * PAGE, PAGE)], sem.at[1, sl])

    def issue(seq, sl):
        def body(p, carry):
            pg = pt_ref[seq * n_pages + p]
            lat_copy(pg, p, sl).start()
            rope_copy(pg, p, sl).start()
            return carry
        lax.fori_loop(0, n_pages, body, 0, unroll=4)

    @pl.when(b == 0)
    def _():
        for sl in range(2):
            cbuf[sl, pl.ds(past, PAGE), :] = jnp.zeros((PAGE, MLA_KV_RANK), F32)
        issue(0, 0)

    @pl.when(b + 1 < n_b)
    def _():
        issue(b + 1, 1 - slot)

    def wait_body(p, carry):
        lat_copy(0, p, slot).wait()
        rope_copy(0, p, slot).wait()
        return carry
    lax.fori_loop(0, n_pages, wait_body, 0, unroll=4)

    cbuf[slot, pl.ds(past, tp), :] = cnew_ref[0]
    rbuf[slot, :, pl.ds(past, PAGE)] = rnewt_ref[0]

    q = qp_ref[0]
    qr = q[:, MLA_KV_RANK:MLA_KV_RANK + MLA_ROPE]
    w_stack = jnp.concatenate([wukt_ref[...], q[:, 0:MLA_KV_RANK]], axis=0)
    n_up = N_HEADS * MLA_NOPE
    qpos = past + lax.broadcasted_iota(jnp.int32, (nrow, 1), 0) % tp
    head_ones = (lax.broadcasted_iota(jnp.int32, (nrow, nrow), 0) // tp
                 == lax.broadcasted_iota(jnp.int32, (nrow, nrow), 1) // 8).astype(BF16)

    for k0 in range(0, tk, ck):
        cb = cbuf[slot, pl.ds(k0, ck), :].astype(BF16)
        cbf[pl.ds(k0, ck), :] = cb
        rt = rbuf[slot, :, pl.ds(k0, ck)]
        big = _dot_nt(w_stack, cb)
        kn2 = big[0:n_up] * big[0:n_up]
        part_r = jnp.sum((rt * rt).reshape(MLA_ROPE // 8, 8, ck), axis=0)
        parts = [jnp.sum(kn2[h * MLA_NOPE:(h + 1) * MLA_NOPE].reshape(MLA_NOPE // 8, 8, ck), axis=0) + part_r
                 for h in range(N_HEADS)]
        part_scr[:, pl.ds(k0, ck)] = jnp.concatenate(parts, axis=0).astype(BF16)
        s_scr[:, pl.ds(k0, ck)] = big[n_up:n_up + nrow]

    rk = lax.rsqrt(_dot(head_ones, part_scr[...]) * (1.0 / MLA_QK) + EPS)
    s = (s_scr[...] + _dot(qr, rbuf[slot].astype(BF16))) * rk
    kpos = lax.broadcasted_iota(jnp.int32, (1, tk), 1)
    s = jnp.where(kpos <= qpos, s, NEG)
    m = jnp.max(s, axis=-1, keepdims=True)
    p = jnp.exp2(s - m)
    l = jnp.sum(p, axis=-1, keepdims=True)
    o = _dot(p.astype(BF16), cbf[...]) / l
    for h in range(N_HEADS):
        y_ref[0, :, h * HEAD_W:(h + 1) * HEAD_W] = _dot(
            o[h * tp:(h + 1) * tp].astype(BF16), wuv_ref[:, h * HEAD_W:(h + 1) * HEAD_W]).astype(BF16)


def _mla_sample(page_flat, qp3, c3, rt3, wukt, wuv, lat, ropet, layer, n_pages):
    bs = qp3.shape[0]
    tp = SAMPLE_PAD_T
    nrow = N_HEADS * tp
    tk = n_pages * PAGE + PAGE
    ck = max(d for d in range(PAGE, 768 + 1, PAGE) if tk % d == 0)
    kern = functools.partial(_mla_sample_kernel, layer=layer, n_pages=n_pages, ck=ck)
    grid_spec = pltpu.PrefetchScalarGridSpec(
        num_scalar_prefetch=1,
        grid=(bs,),
        in_specs=[pl.BlockSpec((1, nrow, 256), lambda b, pt: (b, 0, 0)),
                  pl.BlockSpec((1, tp, MLA_KV_RANK), lambda b, pt: (b, 0, 0)),
                  pl.BlockSpec((1, MLA_ROPE, PAGE), lambda b, pt: (b, 0, 0)),
                  pl.BlockSpec((N_HEADS * MLA_NOPE, MLA_KV_RANK), lambda b, pt: (0, 0)),
                  pl.BlockSpec((MLA_KV_RANK, N_HEADS * HEAD_W), lambda b, pt: (0, 0)),
                  pl.BlockSpec(memory_space=pl.ANY),
                  pl.BlockSpec(memory_space=pl.ANY)],
        out_specs=pl.BlockSpec((1, tp, N_HEADS * HEAD_W), lambda b, pt: (b, 0, 0)),
        scratch_shapes=[pltpu.VMEM((2, tk, MLA_KV_RANK), F32),
                        pltpu.VMEM((2, MLA_ROPE, tk), F32),
                        pltpu.VMEM((tk, MLA_KV_RANK), BF16),
                        pltpu.VMEM((nrow, tk), F32),
                        pltpu.VMEM((nrow, tk), BF16),
                        pltpu.SemaphoreType.DMA((2, 2))],
    )
    return pl.pallas_call(
        kern,
        grid_spec=grid_spec,
        out_shape=jax.ShapeDtypeStruct((bs, tp, N_HEADS * HEAD_W), BF16),
        compiler_params=_cparams(("arbitrary",)),
        name="mla_sample",
    )(page_flat, qp3, c3, rt3, wukt, wuv, lat, ropet)


def _memkv_kernel(x_ref, g_ref, wk_ref, wv_ref, gk_ref, k_ref, v_ref):
    mn = _rms(x_ref[...], g_ref[...]).astype(BF16)
    k = _dot(mn, wk_ref[...])
    gk = gk_ref[...]
    tm = x_ref.shape[0]
    v = _dot(mn, wv_ref[...])
    for h in range(N_HEADS):
        sl = slice(h * HEAD_W, (h + 1) * HEAD_W)
        k_ref[pl.ds(h, tm, stride=N_HEADS), :] = _rms(k[:, sl], gk)
        v_ref[pl.ds(h, tm, stride=N_HEADS), :] = v[:, sl]


def _memkv(mem2d, g, wk, wv, gk, tm):
    n, d = mem2d.shape
    br = wk.shape[1]
    return pl.pallas_call(
        _memkv_kernel,
        grid=(n // tm,),
        in_specs=[pl.BlockSpec((tm, d), lambda i: (i, 0)),
                  pl.BlockSpec((1, d), lambda i: (0, 0)),
                  pl.BlockSpec((d, br), lambda i: (0, 0)),
                  pl.BlockSpec((d, br), lambda i: (0, 0)),
                  pl.BlockSpec((1, HEAD_W), lambda i: (0, 0))],
        out_specs=[pl.BlockSpec((tm * N_HEADS, HEAD_W), lambda i: (i, 0)),
                   pl.BlockSpec((tm * N_HEADS, HEAD_W), lambda i: (i, 0))],
        out_shape=[jax.ShapeDtypeStruct((n * N_HEADS, HEAD_W), F32),
                   jax.ShapeDtypeStruct((n * N_HEADS, HEAD_W), F32)],
        compiler_params=_cparams(("arbitrary",)),
        name="memkv",
    )(mem2d, g.reshape(1, d), wk, wv, gk.reshape(1, HEAD_W))


def _mem_attn_kernel(q_ref, k_ref, v_ref, gq_ref, y_ref, *, sb, m):
    gq = gq_ref[...]
    scale = HEAD_W ** -0.5
    for s in range(sb):
        q = q_ref[s].astype(F32)
        for h in range(N_HEADS):
            sl = slice(h * HEAD_W, (h + 1) * HEAD_W)
            qh = (_rms(q[:, sl], gq) * scale).astype(BF16)
            sc = _dot_nt(qh, k_ref[s, pl.ds(h, m, stride=N_HEADS), :].astype(BF16))
            mx = jnp.max(sc, axis=-1, keepdims=True)
            p = jnp.exp(sc - mx)
            l = jnp.sum(p, axis=-1, keepdims=True)
            o = _dot(p.astype(BF16), v_ref[s, pl.ds(h, m, stride=N_HEADS), :].astype(BF16)) / l
            y_ref[s, :, sl] = o.astype(BF16)


def _mem_attn(z3d, k4, v4, gq, layer, sb, tq):
    b, t, _ = z3d.shape
    m = k4.shape[2] // N_HEADS
    br = N_HEADS * HEAD_W
    return pl.pallas_call(
        functools.partial(_mem_attn_kernel, sb=sb, m=m),
        grid=(b // sb, t // tq),
        in_specs=[pl.BlockSpec((sb, tq, br), lambda i, j: (i, j, COL_MEMQ // br)),
                  pl.BlockSpec((None, sb, m * N_HEADS, HEAD_W), lambda i, j: (layer, i, 0, 0)),
                  pl.BlockSpec((None, sb, m * N_HEADS, HEAD_W), lambda i, j: (layer, i, 0, 0)),
                  pl.BlockSpec((1, HEAD_W), lambda i, j: (0, 0))],
        out_specs=pl.BlockSpec((sb, tq, br), lambda i, j: (i, j, 0)),
        out_shape=jax.ShapeDtypeStruct((b, t, br), BF16),
        compiler_params=_cparams(("arbitrary", "arbitrary")),
        name="mem_attn",
    )(z3d, k4, v4, gq.reshape(1, HEAD_W))


def _outproj_kernel(x_ref, mg_ref, sg_ref, y0_ref, y1_ref, y2_ref, y3_ref, wb_ref, wo_ref, o_ref):
    d = x_ref.shape[1]
    br = y0_ref.shape[1]
    merged = None
    for n, y_ref in enumerate((y0_ref, y1_ref, y2_ref, y3_ref)):
        sg = sg_ref[:, n * br:(n + 1) * br].astype(F32)
        ys = (y_ref[...].astype(F32) * (sg * jax.nn.sigmoid(sg))).astype(BF16)
        proj = _dot(ys, wb_ref[n])
        term = jax.nn.sigmoid(mg_ref[:, n * d:(n + 1) * d].astype(F32)) * proj
        merged = term if merged is None else merged + term
    o_ref[...] = x_ref[...] + _dot(merged.astype(BF16), wo_ref[...])


def _outproj(x2d, z2d, ys, wb, wo, tm):
    n, d = x2d.shape
    br = ys[0].shape[1]
    nb = wb.shape[0]
    yspec = pl.BlockSpec((tm, br), lambda i: (i, 0))
    return pl.pallas_call(
        _outproj_kernel,
        grid=(n // tm,),
        in_specs=[pl.BlockSpec((tm, d), lambda i: (i, 0)),
                  pl.BlockSpec((tm, nb * d), lambda i: (i, COL_MG // (nb * d))),
                  pl.BlockSpec((tm, nb * br), lambda i: (i, COL_SG // (nb * br))),
                  yspec, yspec, yspec, yspec,
                  pl.BlockSpec((nb, br, d), lambda i: (0, 0, 0)),
                  pl.BlockSpec((d, d), lambda i: (0, 0))],
        out_specs=pl.BlockSpec((tm, d), lambda i: (i, 0)),
        out_shape=jax.ShapeDtypeStruct((n, d), F32),
        compiler_params=_cparams(("arbitrary",)),
        name="outproj",
    )(x2d, z2d, z2d, *ys, wb, wo)


def _rope_tables(pos):
    half = MLA_ROPE // 2
    freqs = ROPE_THETA ** (-np.arange(half, dtype=np.float64) / half)
    ang = np.asarray(pos, np.float64)[:, None] * freqs[None, :]
    cos = np.tile(np.cos(ang), (1, 8))
    sin = np.tile(np.sin(ang), (1, 8))
    sin[:, :64] *= -1.0
    return jnp.asarray(cos, F32), jnp.asarray(sin, F32)


def _relayout_w_in(w):
    d = w.shape[0]
    z = lambda n: jnp.zeros((d, n), w.dtype)
    o_q, o_kv, o_pe, o_mq, o_sg, o_mg = 3072, 3264, 3392, 3424, 3936, 5984
    half = MLA_ROPE // 2
    cols = [w[:, :o_q],
            w[:, o_q:o_kv], z(64),
            w[:, o_kv:o_pe],
            w[:, o_pe:o_pe + half], z(48), w[:, o_pe + half:o_mq], z(48),
            w[:, o_mq:o_sg],
            w[:, o_mg:],
            w[:, o_sg:o_mg]]
    return jnp.concatenate(cols, axis=1).astype(BF16)


def _mla_params(q_norm, w_uq, kv_norm, w_uk, gq, gk):
    half = MLA_ROPE // 2
    wq = w_uq.reshape(MLA_Q_RANK, N_HEADS, MLA_QK)
    wuq = jnp.concatenate([wq[:, :, :MLA_NOPE].reshape(MLA_Q_RANK, -1),
                           wq[:, :, MLA_NOPE:MLA_NOPE + half].reshape(MLA_Q_RANK, -1),
                           wq[:, :, MLA_NOPE + half:].reshape(MLA_Q_RANK, -1)], axis=1).astype(BF16)
    g2 = gq * gk
    wk = w_uk.reshape(MLA_KV_RANK, N_HEADS, MLA_NOPE)
    wabs = jnp.zeros((N_HEADS * MLA_NOPE, N_HEADS * MLA_KV_RANK), F32)
    for h in range(N_HEADS):
        blk = (wk[:, h, :] * g2[None, :MLA_NOPE]).T
        wabs = wabs.at[h * MLA_NOPE:(h + 1) * MLA_NOPE, h * MLA_KV_RANK:(h + 1) * MLA_KV_RANK].set(blk)
    grope = jnp.zeros((1, 128), F32).at[0, :MLA_ROPE].set(g2[MLA_NOPE:])
    j = np.arange(384)
    head_of = np.where(j < 256, j // MLA_NOPE, (j % 64) // half)
    indq = (head_of[:, None] == np.arange(128)[None, :]).astype(np.float32)
    lane = j - 256
    is_rope = (j >= 256) & ((lane < half) | ((lane >= 64) & (lane < 64 + half)))
    indk = np.zeros((8, 384), np.float32)
    for h in range(N_HEADS):
        indk[h] = ((j < 256) & (j // MLA_NOPE == h)) | is_rope
    return dict(gq_norm=q_norm.reshape(1, -1), wuq=wuq, gkv_norm=kv_norm.reshape(1, -1),
                wuk=w_uk.astype(BF16), wukt=w_uk.T.astype(BF16), wabs=wabs.astype(BF16), grope=grope,
                indq=jnp.asarray(indq, BF16), indk=jnp.asarray(indk, BF16))


def _hgrn_lower_bounds(lb_param):
    p = jax.nn.softmax(lb_param.astype(F32), axis=0)
    return jnp.cumsum(p, axis=0) - p[0]


def kernel(x_prompt, x_sample, mem_prompt, cache_mla_latent, cache_mla_rope, page_table, state_hgrn, state_conv, cache_mem_k, cache_mem_v, norm_gain, w_in, conv_w, hgrn_lb, hgrn_norm, mla_q_norm, mla_w_uq, mla_kv_norm, mla_w_uk, mla_w_uv, mla_q_gain, mla_k_gain, mem_norm, mem_w_k, mem_w_v, mem_q_gain, mem_k_gain, w_branch_out, w_out):
    bp, tp, d = x_prompt.shape
    bs, ts, _ = x_sample.shape
    depth = w_in.shape[0]
    br = conv_w.shape[2]
    mem_len = mem_prompt.shape[1]
    n_pages = page_table.shape[1]
    past = n_pages * cache_mla_latent.shape[2]
    tpad = SAMPLE_PAD_T
    n_p = bp * tp
    n_s = bs * tpad

    lbs = _hgrn_lower_bounds(hgrn_lb)
    cos_p, sin_p = _rope_tables(np.arange(tp))
    cos_s, sin_s = _rope_tables(past + np.arange(tpad))
    tm_s = min(n_s, 1024)
    cos_s = jnp.tile(cos_s, (tm_s // tpad, 1))
    sin_s = jnp.tile(sin_s, (tm_s // tpad, 1))
    page_flat = page_table.reshape(-1).astype(jnp.int32)
    rope_t = jnp.swapaxes(cache_mla_rope, 2, 3)
    mem_k4 = cache_mem_k.reshape(depth, bs, mem_len * N_HEADS, HEAD_W)
    mem_v4 = cache_mem_v.reshape(depth, bs, mem_len * N_HEADS, HEAD_W)

    tm_p = min(tp, 512)
    xp = x_prompt.reshape(n_p, d)
    xs = jnp.pad(x_sample, ((0, 0), (0, tpad - ts), (0, 0))).reshape(n_s, d)
    mem2d = mem_prompt.reshape(bp * mem_len, d)
    zero_state = jnp.zeros((1, bp, N_HEADS, HEAD_W, HEAD_W), F32)

    outs = {k: [] for k in ("p_lat", "p_rope", "p_hg", "p_conv", "p_mk", "p_mv", "s_lat", "s_rope", "s_hg", "s_conv")}
    for l in range(depth):
        w_in_l = _relayout_w_in(w_in[l])
        pw = _mla_params(mla_q_norm[l], mla_w_uq[l], mla_kv_norm[l], mla_w_uk[l], mla_q_gain[l], mla_k_gain[l])
        wuv = mla_w_uv[l].astype(BF16)
        wb = w_branch_out[l].astype(BF16)
        wo = w_out[l].astype(BF16)

        z = _inproj(xp, norm_gain[l], w_in_l, min(n_p, 1024), 2048)
        z3 = z.reshape(bp, tp, N_COLS)
        y_conv, tail = _conv(z, None, conv_w[l], tp, tm_p, 8)
        y_hg, s_fin = _hgrn(z3, lbs[l], hgrn_norm[l], zero_state, 0, 1, tm_p, HG_CHUNK, None)
        c, r, kv, rk, qp = _mla_prep(z, cos_p, sin_p, pw, tm_p)
        y_mla = _mla_prompt(qp, kv, rk, wuv, bp, tp, min(tp, MLA_TQ))
        mk, mv = _memkv(mem2d, mem_norm[l], mem_w_k[l].astype(BF16), mem_w_v[l].astype(BF16), mem_k_gain[l],
                        min(bp * mem_len, 512))
        rows_m = mem_len * N_HEADS
        y_mem = _mem_attn(z3, mk.reshape(1, bp, rows_m, HEAD_W), mv.reshape(1, bp, rows_m, HEAD_W), mem_q_gain[l],
                          0, 1, tm_p)
        xp = _outproj(xp, z, (y_conv, y_hg.reshape(n_p, br), y_mla, y_mem.reshape(n_p, br)), wb, wo, tm_p)
        outs["p_lat"].append(c.reshape(bp, tp, MLA_KV_RANK))
        outs["p_rope"].append(r.reshape(bp, tp, MLA_ROPE))
        outs["p_hg"].append(s_fin)
        outs["p_conv"].append(tail.reshape(bp, 8, br)[:, 8 - (CONV_K - 1):])
        outs["p_mk"].append(mk.reshape(bp, mem_len, N_HEADS, HEAD_W))
        outs["p_mv"].append(mv.reshape(bp, mem_len, N_HEADS, HEAD_W))

        z = _inproj(xs, norm_gain[l], w_in_l, tm_s, 2048)
        z3 = z.reshape(bs, tpad, N_COLS)
        hist = jnp.pad(state_conv[l], ((0, 0), (0, tpad - (CONV_K - 1)), (0, 0))).reshape(n_s, br)
        y_conv, u_all = _conv(z, hist, conv_w[l], tpad, tm_s, tm_s)
        sb = 8 if bs % 8 == 0 else 1
        y_hg, s_fin = _hgrn(z3, lbs[l], hgrn_norm[l], state_hgrn, l, sb, tpad, tpad, ts)
        c, r, kv, rk, qp = _mla_prep(z, cos_s, sin_s, pw, tm_s)
        qp3 = qp.reshape(N_HEADS, bs, tpad, 256).transpose(1, 0, 2, 3).reshape(bs, N_HEADS * tpad, 256)
        rt3 = jnp.pad(r.reshape(bs, tpad, MLA_ROPE).transpose(0, 2, 1), ((0, 0), (0, 0), (0, PAGE - tpad)))
        y_mla = _mla_sample(page_flat, qp3, c.reshape(bs, tpad, MLA_KV_RANK), rt3, pw["wukt"], wuv,
                            cache_mla_latent, rope_t, l, n_pages)
        y_mem = _mem_attn(z3, mem_k4, mem_v4, mem_q_gain[l], l, sb, tpad)
        xs = _outproj(xs, z, (y_conv, y_hg.reshape(n_s, br), y_mla.reshape(n_s, br), y_mem.reshape(n_s, br)),
                      wb, wo, tm_s)
        outs["s_lat"].append(c.reshape(bs, tpad, MLA_KV_RANK)[:, :ts])
        outs["s_rope"].append(r.reshape(bs, tpad, MLA_ROPE)[:, :ts])
        outs["s_hg"].append(s_fin)
        outs["s_conv"].append(u_all.reshape(bs, tpad, br)[:, ts - (CONV_K - 1):ts])

    st = lambda k: jnp.stack(outs[k])
    return (xp.reshape(bp, tp, d), xs.reshape(bs, tpad, d)[:, :ts], st("p_lat"), st("p_rope"), st("p_hg"),
            st("p_conv"), st("p_mk"), st("p_mv"), st("s_lat"), st("s_rope"), st("s_hg"), st("s_conv"))
```

```python
import functools

import numpy as np
import jax
import jax.numpy as jnp
from jax import lax
from jax.experimental import pallas as pl
from jax.experimental.pallas import tpu as pltpu

F32 = jnp.float32
BF16 = jnp.bfloat16

N_HEADS = 4
HEAD_W = 128
MLA_NOPE = 64
MLA_ROPE = 32
MLA_QK = MLA_NOPE + MLA_ROPE
MLA_Q_RANK = 192
MLA_KV_RANK = 128
CONV_K = 3
ROPE_THETA = 10000.0
EPS = 1e-6
NEG = -1e30
PAGE = 128
SAMPLE_PAD_T = 8
HG_CHUNK = 64
HG_BLOCK = 8
MLA_ROW_GROUP = 128
MLA_TQ = 512
LOG2E = 1.4426950408889634
VMEM_LIMIT = 56 * 1024 * 1024

COL_CONV = 0
COL_HG = 1536
COL_MLA = 3072
COL_MEMQ = 3584
COL_MG = 4096
COL_SG = 8192
N_COLS = 10240


def _cparams(sem):
    return pltpu.CompilerParams(dimension_semantics=sem, vmem_limit_bytes=VMEM_LIMIT)


def _rms(x, g):
    return x * lax.rsqrt(jnp.mean(x * x, axis=-1, keepdims=True) + EPS) * g


def _sigmoid(x):
    return 0.5 * jnp.tanh(0.5 * x) + 0.5


def _dot(a, b):
    return jnp.dot(a, b, preferred_element_type=F32)


def _dot_nt(a, b):
    return lax.dot_general(a, b, (((1,), (1,)), ((), ())), preferred_element_type=F32)


def _dot_tn(a, b):
    return lax.dot_general(a, b, (((0,), (0,)), ((), ())), preferred_element_type=F32)


def _inproj_kernel(x_ref, g_ref, w_ref, z_ref, hn_ref):
    @pl.when(pl.program_id(1) == 0)
    def _():
        hn_ref[...] = _rms(x_ref[...], g_ref[...]).astype(BF16)

    z_ref[...] = _dot(hn_ref[...], w_ref[...]).astype(BF16)


def _inproj(x2d, g, w_bf, tm, tn):
    n, d = x2d.shape
    ncol = w_bf.shape[1]
    return pl.pallas_call(
        _inproj_kernel,
        grid=(n // tm, ncol // tn),
        in_specs=[pl.BlockSpec((tm, d), lambda i, j: (i, 0)),
                  pl.BlockSpec((1, d), lambda i, j: (0, 0)),
                  pl.BlockSpec((d, tn), lambda i, j: (0, j))],
        out_specs=pl.BlockSpec((tm, tn), lambda i, j: (i, j)),
        out_shape=jax.ShapeDtypeStruct((n, ncol), BF16),
        scratch_shapes=[pltpu.VMEM((tm, d), BF16)],
        compiler_params=_cparams(("arbitrary", "arbitrary")),
        name="inproj",
    )(x2d, g.reshape(1, d), w_bf)


def _conv_kernel(*refs, seq_t, tr, tail_rows, has_hist):
    if has_hist:
        z_ref, halo_ref, hist_ref, w_ref, y_ref, tail_ref = refs
    else:
        z_ref, halo_ref, w_ref, y_ref, tail_ref = refs
    i = pl.program_id(0)
    br = y_ref.shape[1]
    z = z_ref[...].astype(F32)
    u = z[:, 2 * br:3 * br] * z[:, 0:br]
    zh = halo_ref[...].astype(F32)
    uh = zh[:, 2 * br:3 * br] * zh[:, 0:br]
    loc = lax.broadcasted_iota(jnp.int32, (tr, 1), 0)
    t = (loc + i * tr) % seq_t
    u1 = jnp.where(loc == 0, uh[7:8], pltpu.roll(u, 1, axis=0))
    u2 = jnp.where(loc == 0, uh[6:7], jnp.where(loc == 1, uh[7:8], pltpu.roll(u, 2, axis=0)))
    if has_hist:
        hp = hist_ref[...]
        u1 = jnp.where(t == 0, pltpu.roll(hp, tr - 1, axis=0), u1)
        u2 = jnp.where(t < 2, hp, u2)
    else:
        u1 = jnp.where(t == 0, 0.0, u1)
        u2 = jnp.where(t < 2, 0.0, u2)
    w = w_ref[...]
    conv = w[0:1] * u2 + w[1:2] * u1 + w[2:3] * u
    y_ref[...] = (z[:, br:2 * br] * conv).astype(BF16)

    @pl.when(((i + 1) * tr) % max(seq_t, tr) == 0)
    def _():
        tail_ref[...] = u[tr - tail_rows:, :]


def _conv(z2d, hist_rows, w, seq_t, tr, tail_rows):
    n = z2d.shape[0]
    br = w.shape[1]
    has_hist = hist_rows is not None
    group = max(seq_t, tr)
    n_tail = (n // group) * tail_rows
    in_specs = [pl.BlockSpec((tr, 3 * br), lambda i: (i, 0)),
                pl.BlockSpec((8, 3 * br), lambda i: (jnp.maximum(i * (tr // 8) - 1, 0), 0))]
    args = [z2d, z2d]
    if has_hist:
        in_specs.append(pl.BlockSpec((tr, br), lambda i: (i, 0)))
        args.append(hist_rows)
    in_specs.append(pl.BlockSpec((CONV_K, br), lambda i: (0, 0)))
    args.append(w)
    return pl.pallas_call(
        functools.partial(_conv_kernel, seq_t=seq_t, tr=tr, tail_rows=tail_rows, has_hist=has_hist),
        grid=(n // tr,),
        in_specs=in_specs,
        out_specs=[pl.BlockSpec((tr, br), lambda i: (i, 0)),
                   pl.BlockSpec((tail_rows, br), lambda i: ((i * tr) // group, 0))],
        out_shape=[jax.ShapeDtypeStruct((n, br), BF16), jax.ShapeDtypeStruct((n_tail, br), F32)],
        compiler_params=_cparams(("arbitrary",)),
        name="conv",
    )(*args)


def _hgrn_intra_diag(q, kk, b2, chunk):
    lane = lax.broadcasted_iota(jnp.int32, (HG_BLOCK, chunk), 1)
    srow = lax.broadcasted_iota(jnp.int32, (HG_BLOCK, chunk), 0)
    at_rows = []
    for r in range(0, chunk, HG_BLOCK):
        b_blk = b2[r:r + HG_BLOCK]
        k_blk = kk[r:r + HG_BLOCK]
        acc = jnp.zeros((HG_BLOCK, chunk), F32)
        for t in range(HG_BLOCK):
            p = jnp.exp2(jnp.minimum(b2[r + t:r + t + 1] - b_blk, 0.0)) * (q[r + t:r + t + 1] * k_blk)
            acc = jnp.where(lane == r + t, jnp.sum(p, axis=-1, keepdims=True), acc)
        at_rows.append(jnp.where(lane - r >= srow, acc, 0.0))
    return at_rows[0] if len(at_rows) == 1 else jnp.concatenate(at_rows, axis=0)


def _hgrn_inter_block_factors(q, kk, b2, chunk):
    nb = chunk // HG_BLOCK
    zero = jnp.zeros((HG_BLOCK, HEAD_W), F32)
    q_rows, k_rows = [], []
    for i in range(nb):
        blk = slice(i * HG_BLOCK, (i + 1) * HG_BLOCK)
        q_tiles, k_tiles = [], []
        for j in range(nb - 1):
            rho = b2[(j + 1) * HG_BLOCK - 1:(j + 1) * HG_BLOCK]
            q_tiles.append(q[blk] * jnp.exp2(b2[blk] - rho) if j < i else zero)
            k_tiles.append(kk[blk] * jnp.exp2(rho - b2[blk]) if j == i else zero)
        q_rows.append(jnp.concatenate(q_tiles, axis=1))
        k_rows.append(jnp.concatenate(k_tiles, axis=1))
    return jnp.concatenate(q_rows, axis=0).astype(BF16), jnp.concatenate(k_rows, axis=0).astype(BF16)


def _hgrn_tile(load, store, lb, gain, sts, valid_fn, tril_bf, chunk, n_c):
    units = [(s, c) for s in range(len(sts)) for c in range(n_c)]
    chunks = range(len(units))
    q, v, kk, g_hi, g_lo = [], [], [], [], []
    for s, c in units:
        qc, zf, vc = load(s, c)
        f = lb + (1.0 - lb) * _sigmoid(zf)
        g = jnp.log(f)
        k = 1.0 - f
        valid = valid_fn(c)
        if valid is not None:
            g = jnp.where(valid, g, 0.0)
            k = jnp.where(valid, k, 0.0)
        hi = g.astype(BF16)
        q.append(qc)
        v.append(vc.astype(BF16))
        kk.append(k)
        g_hi.append(hi)
        g_lo.append((g - hi.astype(F32)).astype(BF16))
    b2 = [(_dot(tril_bf, g_hi[c]) + _dot(tril_bf, g_lo[c])) * LOG2E for c in chunks]
    bl2 = [b[chunk - 1:chunk, :] for b in b2]
    upd = [_dot_tn(v[c], (kk[c] * jnp.exp2(bl2[c] - b2[c])).astype(BF16)) for c in chunks]
    at = [_hgrn_intra_diag(q[c], kk[c], b2[c], chunk) for c in chunks]
    if chunk > HG_BLOCK:
        factors = [_hgrn_inter_block_factors(q[c], kk[c], b2[c], chunk) for c in chunks]
        at = [at[c] + _dot_nt(factors[c][1], factors[c][0]) for c in chunks]
    o = [_dot_tn(at[c].astype(BF16), v[c]) for c in chunks]
    sts = list(sts)
    states = []
    for u, (s, _) in enumerate(units):
        states.append(sts[s].astype(BF16))
        sts[s] = sts[s] * jnp.exp2(bl2[u]) + upd[u]
    for u, (s, c) in enumerate(units):
        oc = o[u] + _dot_nt((q[u] * jnp.exp2(b2[u])).astype(BF16), states[u])
        store(s, c, _rms(oc, gain).astype(BF16))
    return sts


def _hgrn_kernel(q_ref, f_ref, i_ref, lb_ref, g_ref, s0_ref, y_ref, sout_ref, st_ref, *, sb, tt, chunk, t_valid):
    tstep = pl.program_id(2)
    n_t = pl.num_programs(2)

    @pl.when(tstep == 0)
    def _():
        for s in range(sb):
            st_ref[s] = s0_ref[s, 0].T

    lb = lb_ref[0]
    gain = g_ref[0]
    row = lax.broadcasted_iota(jnp.int32, (chunk, chunk), 0)
    col = lax.broadcasted_iota(jnp.int32, (chunk, chunk), 1)
    tril_bf = (row >= col).astype(BF16)
    scale = HEAD_W ** -0.5

    def load(s, c):
        rows = pl.ds(c * chunk, chunk)
        return (q_ref[s, rows, :].astype(F32) * scale, f_ref[s, rows, :].astype(F32),
                i_ref[s, rows, :].astype(F32))

    def store(s, c, y):
        y_ref[s, pl.ds(c * chunk, chunk), :] = y

    def valid_fn(c):
        if t_valid is None:
            return None
        return tstep * tt + c * chunk + lax.broadcasted_iota(jnp.int32, (chunk, 1), 0) < t_valid

    new_states = _hgrn_tile(load, store, lb, gain, [st_ref[s] for s in range(sb)], valid_fn, tril_bf, chunk,
                            tt // chunk)
    for s in range(sb):
        st_ref[s] = new_states[s]

    @pl.when(tstep == n_t - 1)
    def _():
        for s in range(sb):
            sout_ref[s, 0] = st_ref[s].T


def _hgrn(z3d, lb, gain, s0, layer, sb, tt, chunk, t_valid):
    b, t, _ = z3d.shape
    cb = COL_HG // HEAD_W
    kern = functools.partial(_hgrn_kernel, sb=sb, tt=tt, chunk=chunk, t_valid=t_valid)
    return pl.pallas_call(
        kern,
        grid=(b // sb, N_HEADS, t // tt),
        in_specs=[pl.BlockSpec((sb, tt, HEAD_W), lambda i, h, k: (i, k, cb + h)),
                  pl.BlockSpec((sb, tt, HEAD_W), lambda i, h, k: (i, k, cb + N_HEADS + h)),
                  pl.BlockSpec((sb, tt, HEAD_W), lambda i, h, k: (i, k, cb + 2 * N_HEADS + h)),
                  pl.BlockSpec((1, 1, HEAD_W), lambda i, h, k: (h, 0, 0)),
                  pl.BlockSpec((1, 1, HEAD_W), lambda i, h, k: (h, 0, 0)),
                  pl.BlockSpec((None, sb, 1, HEAD_W, HEAD_W), lambda i, h, k: (layer, i, h, 0, 0))],
        out_specs=[pl.BlockSpec((sb, tt, HEAD_W), lambda i, h, k: (i, k, h)),
                   pl.BlockSpec((sb, 1, HEAD_W, HEAD_W), lambda i, h, k: (i, h, 0, 0))],
        out_shape=[jax.ShapeDtypeStruct((b, t, N_HEADS * HEAD_W), BF16),
                   jax.ShapeDtypeStruct((b, N_HEADS, HEAD_W, HEAD_W), F32)],
        scratch_shapes=[pltpu.VMEM((sb, HEAD_W, HEAD_W), F32)],
        compiler_params=_cparams(("arbitrary", "arbitrary", "arbitrary")),
        name="hgrn",
    )(z3d, z3d, z3d, lb.reshape(N_HEADS, 1, HEAD_W), gain.reshape(N_HEADS, 1, HEAD_W), s0)


def _mla_prep_kernel(z_ref, cos_ref, sin_ref, gq_ref, wuq_ref, gkv_ref, wuk_ref, wabs_ref, grope_ref,
                     indq_ref, indk_ref, c_ref, r_ref, kv_ref, rk_ref, qp_ref):
    z = z_ref[...].astype(F32)
    ql = z[:, 0:MLA_Q_RANK]
    kvl = z[:, 256:384]
    kpe = z[:, 384:512]
    cos = cos_ref[...]
    sin = sin_ref[...]
    lane = lax.broadcasted_iota(jnp.int32, (1, 128), 1)

    qf = _dot(_rms(ql, gq_ref[...]).astype(BF16), wuq_ref[...])
    q_nope = qf[:, 0:256]
    rq_in = qf[:, 256:384]
    rot_q = rq_in * cos + pltpu.roll(rq_in, 64, axis=1) * sin
    rot_k = kpe * cos + pltpu.roll(kpe, 64, axis=1) * sin

    c = _rms(kvl, gkv_ref[...])
    c_ref[...] = c
    r32 = jnp.where(lane < 16, rot_k, pltpu.roll(rot_k, 80, axis=1))
    r32 = jnp.where(lane < MLA_ROPE, r32, 0.0)
    r_ref[...] = r32[:, 0:MLA_ROPE]
    cb = c.astype(BF16)
    kv_ref[...] = jnp.concatenate([cb, r32.astype(BF16)], axis=1)

    kn = _dot(cb, wuk_ref[...])
    kcat2 = jnp.concatenate([kn * kn, rot_k * rot_k], axis=1).astype(BF16)
    ssk = _dot_nt(indk_ref[...], kcat2)
    rk_ref[...] = lax.rsqrt(ssk * (1.0 / MLA_QK) + EPS)

    qcat2 = jnp.concatenate([q_nope * q_nope, rot_q * rot_q], axis=1).astype(BF16)
    ssq = _dot(qcat2, indq_ref[...])
    rq = lax.rsqrt(ssq * (1.0 / MLA_QK) + EPS) * (MLA_QK ** -0.5 * LOG2E)

    q_abs = _dot(q_nope.astype(BF16), wabs_ref[...])
    grope = grope_ref[...]
    for h in range(N_HEADS):
        a = rot_q if h == 0 else pltpu.roll(rot_q, 128 - 16 * h, axis=1)
        qr = jnp.where(lane < 16, a, pltpu.roll(a, 80, axis=1)) * grope
        rq_h = rq[:, h:h + 1]
        qp_ref[h, :, 0:128] = (q_abs[:, h * 128:(h + 1) * 128] * rq_h).astype(BF16)
        qp_ref[h, :, 128:256] = (qr * rq_h).astype(BF16)


def _mla_prep(z2d, cos_t, sin_t, pw, tm):
    n = z2d.shape[0]
    n_tab = cos_t.shape[0] // tm
    full = lambda shape: pl.BlockSpec(shape, lambda i: (0,) * len(shape))
    return pl.pallas_call(
        _mla_prep_kernel,
        grid=(n // tm,),
        in_specs=[pl.BlockSpec((tm, 512), lambda i: (i, COL_MLA // 512)),
                  pl.BlockSpec((tm, 128), lambda i: (i % n_tab, 0)),
                  pl.BlockSpec((tm, 128), lambda i: (i % n_tab, 0)),
                  full((1, MLA_Q_RANK)), full((MLA_Q_RANK, 384)), full((1, MLA_KV_RANK)),
                  full((MLA_KV_RANK, 256)), full((256, 512)), full((1, 128)),
                  full((384, 128)), full((8, 384))],
        out_specs=[pl.BlockSpec((tm, MLA_KV_RANK), lambda i: (i, 0)),
                   pl.BlockSpec((tm, MLA_ROPE), lambda i: (i, 0)),
                   pl.BlockSpec((tm, 256), lambda i: (i, 0)),
                   pl.BlockSpec((8, tm), lambda i: (0, i)),
                   pl.BlockSpec((N_HEADS, tm, 256), lambda i: (0, i, 0))],
        out_shape=[jax.ShapeDtypeStruct((n, MLA_KV_RANK), F32),
                   jax.ShapeDtypeStruct((n, MLA_ROPE), F32),
                   jax.ShapeDtypeStruct((n, 256), BF16),
                   jax.ShapeDtypeStruct((8, n), F32),
                   jax.ShapeDtypeStruct((N_HEADS, n, 256), BF16)],
        compiler_params=_cparams(("arbitrary",)),
        name="mla_prep",
    )(z2d, cos_t, sin_t, pw["gq_norm"], pw["wuq"], pw["gkv_norm"], pw["wuk"], pw["wabs"], pw["grope"],
      pw["indq"], pw["indk"])


def _mla_prompt_kernel(qp_ref, kv_ref, rk_ref, wuv_ref, y_ref, m_ref, l_ref, acc_ref, *, tq, rb):
    i = pl.program_id(1)
    nrow = N_HEADS * tq
    m_ref[...] = jnp.full((nrow, 128), NEG, F32)
    l_ref[...] = jnp.zeros((nrow, 128), F32)
    acc_ref[...] = jnp.zeros((nrow, MLA_KV_RANK), F32)
    row = lax.broadcasted_iota(jnp.int32, (rb, 128), 0)
    col = lax.broadcasted_iota(jnp.int32, (rb, 128), 1)
    lane_tiles = range(0, tq, 128)

    def block(k0, diagonal):
        kv = kv_ref[pl.ds(k0, tq), :]
        cv = kv[:, 0:MLA_KV_RANK]

        def qk(h):
            return _dot_nt(qp_ref[h], kv)

        def softmax_update(h, s_h):
            p_rows = []
            for r in range(0, tq, rb):
                rows = pl.ds(h * tq + r, rb)
                tiles = []
                for c in lane_tiles:
                    s = s_h[r:r + rb, c:c + 128] * rk_ref[h:h + 1, pl.ds(k0 + c, 128)]
                    if diagonal:
                        s = jnp.where(col + c <= row + r, s, NEG)
                    tiles.append(s)
                m_old = m_ref[rows, :]
                m_new = jnp.maximum(m_old, jnp.max(functools.reduce(jnp.maximum, tiles), axis=-1, keepdims=True))
                alpha = jnp.exp2(m_old - m_new)
                probs = [jnp.exp2(s - m_new) for s in tiles]
                l_ref[rows, :] = alpha * l_ref[rows, :] + jnp.sum(functools.reduce(jnp.add, probs), axis=-1,
                                                                  keepdims=True)
                m_ref[rows, :] = m_new
                acc_ref[rows, :] = alpha * acc_ref[rows, :]
                p_rows.append(jnp.concatenate([p.astype(BF16) for p in probs], axis=1))
            return jnp.concatenate(p_rows, axis=0)

        def pv(h, p_h):
            acc_ref[pl.ds(h * tq, tq), :] += _dot(p_h, cv)

        s0 = qk(0)
        s1 = qk(1)
        p0 = softmax_update(0, s0)
        s2 = qk(2)
        p1 = softmax_update(1, s1)
        pv(0, p0)
        s3 = qk(3)
        p2 = softmax_update(2, s2)
        pv(1, p1)
        p3 = softmax_update(3, s3)
        pv(2, p2)
        pv(3, p3)

    def step(j, carry):
        block(pl.multiple_of(j * tq, tq), False)
        return carry

    lax.fori_loop(0, i, step, 0)
    block(pl.multiple_of(i * tq, tq), True)
    for h in range(N_HEADS):
        rows = pl.ds(h * tq, tq)
        o = (acc_ref[rows, :] / l_ref[rows, :]).astype(BF16)
        y_ref[:, h * HEAD_W:(h + 1) * HEAD_W] = _dot(o, wuv_ref[:, h * HEAD_W:(h + 1) * HEAD_W]).astype(BF16)


def _mla_prompt(qp, kv, rk, wuv, b, t, tq):
    n = b * t
    nq = t // tq
    nrow = N_HEADS * tq
    return pl.pallas_call(
        functools.partial(_mla_prompt_kernel, tq=tq, rb=min(tq, MLA_ROW_GROUP)),
        grid=(b, nq),
        scratch_shapes=[pltpu.VMEM((nrow, 128), F32), pltpu.VMEM((nrow, 128), F32),
                        pltpu.VMEM((nrow, MLA_KV_RANK), F32)],
        in_specs=[pl.BlockSpec((N_HEADS, tq, 256), lambda bi, i: (0, bi * nq + i, 0)),
                  pl.BlockSpec((t, 256), lambda bi, i: (bi, 0)),
                  pl.BlockSpec((8, t), lambda bi, i: (0, bi)),
                  pl.BlockSpec((MLA_KV_RANK, N_HEADS * HEAD_W), lambda bi, i: (0, 0))],
        out_specs=pl.BlockSpec((tq, N_HEADS * HEAD_W), lambda bi, i: (bi * nq + i, 0)),
        out_shape=jax.ShapeDtypeStruct((n, N_HEADS * HEAD_W), BF16),
        compiler_params=_cparams(("arbitrary", "arbitrary")),
        name="mla_prompt",
    )(qp, kv, rk, wuv)


def _mla_sample_kernel(pt_ref, qp_ref, cnew_ref, rnewt_ref, wukt_ref, wuv_ref, lat_hbm, ropet_hbm,
                       y_ref, cbuf, rbuf, cbf, s_scr, part_scr, sem, *, layer, n_pages, ck):
    b = pl.program_id(0)
    n_b = pl.num_programs(0)
    past = n_pages * PAGE
    tk = past + PAGE
    slot = b % 2
    tp = SAMPLE_PAD_T
    nrow = N_HEADS * tp

    def lat_copy(pg, p, sl):
        return pltpu.make_async_copy(lat_hbm.at[layer, pg], cbuf.at[sl, pl.ds(p * PAGE, PAGE), :], sem.at[0, sl])

    def rope_copy(pg, p, sl):
        return pltpu.make_async_copy(ropet_hbm.at[layer, pg], rbuf.at[sl, :, pl.ds(p * PAGE, PAGE)], sem.at[1, sl])

    def issue(seq, sl):
        def body(p, carry):
            pg = pt_ref[seq * n_pages + p]
            lat_copy(pg, p, sl).start()
            rope_copy(pg, p, sl).start()
            return carry
        lax.fori_loop(0, n_pages, body, 0, unroll=4)

    @pl.when(b == 0)
    def _():
        for sl in range(2):
            cbuf[sl, pl.ds(past, PAGE), :] = jnp.zeros((PAGE, MLA_KV_RANK), F32)
        issue(0, 0)

    def wait_all(sl):
        def wait_body(p, carry):
            lat_copy(0, p, sl).wait()
            rope_copy(0, p, sl).wait()
            return carry
        lax.fori_loop(0, n_pages, wait_body, 0, unroll=4)

    wait_all(slot)
    nxt = jnp.minimum(b + 1, n_b - 1)
    for p in range(n_pages):
        pg = pt_ref[nxt * n_pages + p]
        lat_copy(pg, p, 1 - slot).start()
        rope_copy(pg, p, 1 - slot).start()

    cbuf[slot, pl.ds(past, tp), :] = cnew_ref[0]
    rbuf[slot, :, pl.ds(past, PAGE)] = rnewt_ref[0]

    q = qp_ref[0]
    qr = q[:, MLA_KV_RANK:MLA_KV_RANK + MLA_ROPE]
    w_stack = jnp.concatenate([wukt_ref[...], q[:, 0:MLA_KV_RANK]], axis=0)
    n_up = N_HEADS * MLA_NOPE
    qpos = past + lax.broadcasted_iota(jnp.int32, (nrow, 1), 0) % tp
    head_ones = (lax.broadcasted_iota(jnp.int32, (nrow, nrow), 0) // tp
                 == lax.broadcasted_iota(jnp.int32, (nrow, nrow), 1) // 8).astype(BF16)

    for k0 in range(0, tk, ck):
        cb = cbuf[slot, pl.ds(k0, ck), :].astype(BF16)
        cbf[pl.ds(k0, ck), :] = cb
        rt = rbuf[slot, :, pl.ds(k0, ck)]
        big = _dot_nt(w_stack, cb)
        kn2 = big[0:n_up] * big[0:n_up]
        part_r = jnp.sum((rt * rt).reshape(MLA_ROPE // 8, 8, ck), axis=0)
        parts = [jnp.sum(kn2[h * MLA_NOPE:(h + 1) * MLA_NOPE].reshape(MLA_NOPE // 8, 8, ck), axis=0) + part_r
                 for h in range(N_HEADS)]
        part_scr[:, pl.ds(k0, ck)] = jnp.concatenate(parts, axis=0).astype(BF16)
        s_scr[:, pl.ds(k0, ck)] = big[n_up:n_up + nrow]

    rk = lax.rsqrt(_dot(head_ones, part_scr[...]) * (1.0 / MLA_QK) + EPS)
    s = (s_scr[...] + _dot(qr, rbuf[slot].astype(BF16))) * rk
    kpos = lax.broadcasted_iota(jnp.int32, (1, tk), 1)
    s = jnp.where(kpos <= qpos, s, NEG)
    m = jnp.max(s, axis=-1, keepdims=True)
    p = jnp.exp2(s - m)
    l = jnp.sum(p, axis=-1, keepdims=True)
    o = _dot(p.astype(BF16), cbf[...]) / l
    for h in range(N_HEADS):
        y_ref[0, :, h * HEAD_W:(h + 1) * HEAD_W] = _dot(
            o[h * tp:(h + 1) * tp].astype(BF16), wuv_ref[:, h * HEAD_W:(h + 1) * HEAD_W]).astype(BF16)

    @pl.when(b == n_b - 1)
    def _():
        wait_all(1 - slot)


def _mla_sample(page_flat, qp3, c3, rt3, wukt, wuv, lat, ropet, layer, n_pages):
    bs = qp3.shape[0]
    tp = SAMPLE_PAD_T
    nrow = N_HEADS * tp
    tk = n_pages * PAGE + PAGE
    ck = max(d for d in range(PAGE, 768 + 1, PAGE) if tk % d == 0)
    kern = functools.partial(_mla_sample_kernel, layer=layer, n_pages=n_pages, ck=ck)
    grid_spec = pltpu.PrefetchScalarGridSpec(
        num_scalar_prefetch=1,
        grid=(bs,),
        in_specs=[pl.BlockSpec((1, nrow, 256), lambda b, pt: (b, 0, 0)),
                  pl.BlockSpec((1, tp, MLA_KV_RANK), lambda b, pt: (b, 0, 0)),
                  pl.BlockSpec((1, MLA_ROPE, PAGE), lambda b, pt: (b, 0, 0)),
                  pl.BlockSpec((N_HEADS * MLA_NOPE, MLA_KV_RANK), lambda b, pt: (0, 0)),
                  pl.BlockSpec((MLA_KV_RANK, N_HEADS * HEAD_W), lambda b, pt: (0, 0)),
                  pl.BlockSpec(memory_space=pl.ANY),
                  pl.BlockSpec(memory_space=pl.ANY)],
        out_specs=pl.BlockSpec((1, tp, N_HEADS * HEAD_W), lambda b, pt: (b, 0, 0)),
        scratch_shapes=[pltpu.VMEM((2, tk, MLA_KV_RANK), F32),
                        pltpu.VMEM((2, MLA_ROPE, tk), F32),
                        pltpu.VMEM((tk, MLA_KV_RANK), BF16),
                        pltpu.VMEM((nrow, tk), F32),
                        pltpu.VMEM((nrow, tk), BF16),
                        pltpu.SemaphoreType.DMA((2, 2))],
    )
    return pl.pallas_call(
        kern,
        grid_spec=grid_spec,
        out_shape=jax.ShapeDtypeStruct((bs, tp, N_HEADS * HEAD_W), BF16),
        compiler_params=_cparams(("arbitrary",)),
        name="mla_sample",
    )(page_flat, qp3, c3, rt3, wukt, wuv, lat, ropet)


def _memkv_kernel(x_ref, g_ref, wk_ref, wv_ref, gk_ref, k_ref, v_ref):
    mn = _rms(x_ref[...], g_ref[...]).astype(BF16)
    k = _dot(mn, wk_ref[...])
    gk = gk_ref[...]
    tm = x_ref.shape[0]
    v = _dot(mn, wv_ref[...])
    for h in range(N_HEADS):
        sl = slice(h * HEAD_W, (h + 1) * HEAD_W)
        k_ref[pl.ds(h, tm, stride=N_HEADS), :] = _rms(k[:, sl], gk)
        v_ref[pl.ds(h, tm, stride=N_HEADS), :] = v[:, sl]


def _memkv(mem2d, g, wk, wv, gk, tm):
    n, d = mem2d.shape
    br = wk.shape[1]
    return pl.pallas_call(
        _memkv_kernel,
        grid=(n // tm,),
        in_specs=[pl.BlockSpec((tm, d), lambda i: (i, 0)),
                  pl.BlockSpec((1, d), lambda i: (0, 0)),
                  pl.BlockSpec((d, br), lambda i: (0, 0)),
                  pl.BlockSpec((d, br), lambda i: (0, 0)),
                  pl.BlockSpec((1, HEAD_W), lambda i: (0, 0))],
        out_specs=[pl.BlockSpec((tm * N_HEADS, HEAD_W), lambda i: (i, 0)),
                   pl.BlockSpec((tm * N_HEADS, HEAD_W), lambda i: (i, 0))],
        out_shape=[jax.ShapeDtypeStruct((n * N_HEADS, HEAD_W), F32),
                   jax.ShapeDtypeStruct((n * N_HEADS, HEAD_W), F32)],
        compiler_params=_cparams(("arbitrary",)),
        name="memkv",
    )(mem2d, g.reshape(1, d), wk, wv, gk.reshape(1, HEAD_W))


def _mem_attn_kernel(q_ref, k_ref, v_ref, gq_ref, y_ref, *, sb, m):
    gq = gq_ref[...]
    scale = HEAD_W ** -0.5
    heads = range(N_HEADS)
    for s in range(sb):
        q = q_ref[s].astype(F32)
        qh = [(_rms(q[:, h * HEAD_W:(h + 1) * HEAD_W], gq) * (scale * LOG2E)).astype(BF16) for h in heads]
        sc = [_dot_nt(qh[h], k_ref[s, pl.ds(h, m, stride=N_HEADS), :].astype(BF16)) for h in heads]
        ps, ls = [], []
        for h in heads:
            p = jnp.exp2(sc[h] - jnp.max(sc[h], axis=-1, keepdims=True))
            ls.append(jnp.sum(p, axis=-1, keepdims=True))
            ps.append(p.astype(BF16))
        for h in heads:
            o = _dot(ps[h], v_ref[s, pl.ds(h, m, stride=N_HEADS), :].astype(BF16)) / ls[h]
            y_ref[s, :, h * HEAD_W:(h + 1) * HEAD_W] = o.astype(BF16)


def _mem_attn(z3d, k4, v4, gq, layer, sb, tq):
    b, t, _ = z3d.shape
    m = k4.shape[2] // N_HEADS
    br = N_HEADS * HEAD_W
    return pl.pallas_call(
        functools.partial(_mem_attn_kernel, sb=sb, m=m),
        grid=(b // sb, t // tq),
        in_specs=[pl.BlockSpec((sb, tq, br), lambda i, j: (i, j, COL_MEMQ // br)),
                  pl.BlockSpec((None, sb, m * N_HEADS, HEAD_W), lambda i, j: (layer, i, 0, 0)),
                  pl.BlockSpec((None, sb, m * N_HEADS, HEAD_W), lambda i, j: (layer, i, 0, 0)),
                  pl.BlockSpec((1, HEAD_W), lambda i, j: (0, 0))],
        out_specs=pl.BlockSpec((sb, tq, br), lambda i, j: (i, j, 0)),
        out_shape=jax.ShapeDtypeStruct((b, t, br), BF16),
        compiler_params=_cparams(("arbitrary", "arbitrary")),
        name="mem_attn",
    )(z3d, k4, v4, gq.reshape(1, HEAD_W))


def _outproj_kernel(x_ref, mg_ref, sg_ref, y0_ref, y1_ref, y2_ref, y3_ref, wb_ref, wo_ref, o_ref):
    d = x_ref.shape[1]
    br = y0_ref.shape[1]
    merged = None
    for n, y_ref in enumerate((y0_ref, y1_ref, y2_ref, y3_ref)):
        sg = sg_ref[:, n * br:(n + 1) * br].astype(F32)
        ys = (y_ref[...].astype(F32) * (sg * _sigmoid(sg))).astype(BF16)
        proj = _dot(ys, wb_ref[n])
        term = _sigmoid(mg_ref[:, n * d:(n + 1) * d].astype(F32)) * proj
        merged = term if merged is None else merged + term
    o_ref[...] = x_ref[...] + _dot(merged.astype(BF16), wo_ref[...])


def _outproj(x2d, z2d, ys, wb, wo, tm):
    n, d = x2d.shape
    br = ys[0].shape[1]
    nb = wb.shape[0]
    yspec = pl.BlockSpec((tm, br), lambda i: (i, 0))
    return pl.pallas_call(
        _outproj_kernel,
        grid=(n // tm,),
        in_specs=[pl.BlockSpec((tm, d), lambda i: (i, 0)),
                  pl.BlockSpec((tm, nb * d), lambda i: (i, COL_MG // (nb * d))),
                  pl.BlockSpec((tm, nb * br), lambda i: (i, COL_SG // (nb * br))),
                  yspec, yspec, yspec, yspec,
                  pl.BlockSpec((nb, br, d), lambda i: (0, 0, 0)),
                  pl.BlockSpec((d, d), lambda i: (0, 0))],
        out_specs=pl.BlockSpec((tm, d), lambda i: (i, 0)),
        out_shape=jax.ShapeDtypeStruct((n, d), F32),
        compiler_params=_cparams(("arbitrary",)),
        name="outproj",
    )(x2d, z2d, z2d, *ys, wb, wo)


def _rope_tables(pos):
    half = MLA_ROPE // 2
    freqs = ROPE_THETA ** (-np.arange(half, dtype=np.float64) / half)
    ang = np.asarray(pos, np.float64)[:, None] * freqs[None, :]
    cos = np.tile(np.cos(ang), (1, 8))
    sin = np.tile(np.sin(ang), (1, 8))
    sin[:, :64] *= -1.0
    return jnp.asarray(cos, F32), jnp.asarray(sin, F32)


def _relayout_w_in(w):
    d = w.shape[0]
    z = lambda n: jnp.zeros((d, n), BF16)
    c = lambda a, b: w[:, a:b].astype(BF16)
    o_q, o_kv, o_pe, o_mq, o_sg, o_mg = 3072, 3264, 3392, 3424, 3936, 5984
    half = MLA_ROPE // 2
    cols = [c(0, o_q),
            c(o_q, o_kv), z(64),
            c(o_kv, o_pe),
            c(o_pe, o_pe + half), z(48), c(o_pe + half, o_mq), z(48),
            c(o_mq, o_sg),
            c(o_mg, w.shape[1]),
            c(o_sg, o_mg)]
    return jnp.concatenate(cols, axis=1)


def _mla_params(q_norm, w_uq, kv_norm, w_uk, gq, gk):
    half = MLA_ROPE // 2
    wq = w_uq.reshape(MLA_Q_RANK, N_HEADS, MLA_QK)
    wuq = jnp.concatenate([wq[:, :, :MLA_NOPE].reshape(MLA_Q_RANK, -1),
                           wq[:, :, MLA_NOPE:MLA_NOPE + half].reshape(MLA_Q_RANK, -1),
                           wq[:, :, MLA_NOPE + half:].reshape(MLA_Q_RANK, -1)], axis=1).astype(BF16)
    g2 = gq * gk
    wk = w_uk.reshape(MLA_KV_RANK, N_HEADS, MLA_NOPE)
    wabs = jnp.zeros((N_HEADS * MLA_NOPE, N_HEADS * MLA_KV_RANK), F32)
    for h in range(N_HEADS):
        blk = (wk[:, h, :] * g2[None, :MLA_NOPE]).T
        wabs = wabs.at[h * MLA_NOPE:(h + 1) * MLA_NOPE, h * MLA_KV_RANK:(h + 1) * MLA_KV_RANK].set(blk)
    grope = jnp.zeros((1, 128), F32).at[0, :MLA_ROPE].set(g2[MLA_NOPE:])
    j = np.arange(384)
    head_of = np.where(j < 256, j // MLA_NOPE, (j % 64) // half)
    indq = (head_of[:, None] == np.arange(128)[None, :]).astype(np.float32)
    lane = j - 256
    is_rope = (j >= 256) & ((lane < half) | ((lane >= 64) & (lane < 64 + half)))
    indk = np.zeros((8, 384), np.float32)
    for h in range(N_HEADS):
        indk[h] = ((j < 256) & (j // MLA_NOPE == h)) | is_rope
    return dict(gq_norm=q_norm.reshape(1, -1), wuq=wuq, gkv_norm=kv_norm.reshape(1, -1),
                wuk=w_uk.astype(BF16), wukt=w_uk.T.astype(BF16), wabs=wabs.astype(BF16), grope=grope,
                indq=jnp.asarray(indq, BF16), indk=jnp.asarray(indk, BF16))


def _hgrn_lower_bounds(lb_param):
    p = jax.nn.softmax(lb_param.astype(F32), axis=0)
    return jnp.cumsum(p, axis=0) - p[0]


def kernel(x_prompt, x_sample, mem_prompt, cache_mla_latent, cache_mla_rope, page_table, state_hgrn, state_conv, cache_mem_k, cache_mem_v, norm_gain, w_in, conv_w, hgrn_lb, hgrn_norm, mla_q_norm, mla_w_uq, mla_kv_norm, mla_w_uk, mla_w_uv, mla_q_gain, mla_k_gain, mem_norm, mem_w_k, mem_w_v, mem_q_gain, mem_k_gain, w_branch_out, w_out):
    bp, tp, d = x_prompt.shape
    bs, ts, _ = x_sample.shape
    depth = w_in.shape[0]
    br = conv_w.shape[2]
    mem_len = mem_prompt.shape[1]
    n_pages = page_table.shape[1]
    past = n_pages * cache_mla_latent.shape[2]
    tpad = SAMPLE_PAD_T
    n_p = bp * tp
    n_s = bs * tpad

    lbs = _hgrn_lower_bounds(hgrn_lb)
    cos_p, sin_p = _rope_tables(np.arange(tp))
    cos_s, sin_s = _rope_tables(past + np.arange(tpad))
    tm_s = min(n_s, 1024)
    cos_s = jnp.tile(cos_s, (tm_s // tpad, 1))
    sin_s = jnp.tile(sin_s, (tm_s // tpad, 1))
    page_flat = page_table.reshape(-1).astype(jnp.int32)
    rope_t = jnp.swapaxes(cache_mla_rope, 2, 3)
    mem_k4 = cache_mem_k.reshape(depth, bs, mem_len * N_HEADS, HEAD_W)
    mem_v4 = cache_mem_v.reshape(depth, bs, mem_len * N_HEADS, HEAD_W)

    tm_p = min(tp, 512)
    xp = x_prompt.reshape(n_p, d)
    xs = jnp.pad(x_sample, ((0, 0), (0, tpad - ts), (0, 0))).reshape(n_s, d)
    mem2d = mem_prompt.reshape(bp * mem_len, d)
    zero_state = jnp.zeros((1, bp, N_HEADS, HEAD_W, HEAD_W), F32)

    outs = {k: [] for k in ("p_lat", "p_rope", "p_hg", "p_conv", "p_mk", "p_mv", "s_lat", "s_rope", "s_hg", "s_conv")}
    for l in range(depth):
        w_in_l = _relayout_w_in(w_in[l])
        pw = _mla_params(mla_q_norm[l], mla_w_uq[l], mla_kv_norm[l], mla_w_uk[l], mla_q_gain[l], mla_k_gain[l])
        wuv = mla_w_uv[l].astype(BF16)
        wb = w_branch_out[l].astype(BF16)
        wo = w_out[l].astype(BF16)

        z = _inproj(xp, norm_gain[l], w_in_l, min(n_p, 1024), 2048)
        z3 = z.reshape(bp, tp, N_COLS)
        y_conv, tail = _conv(z, None, conv_w[l], tp, tm_p, 8)
        y_hg, s_fin = _hgrn(z3, lbs[l], hgrn_norm[l], zero_state, 0, 1, tm_p, HG_CHUNK, None)
        c, r, kv, rk, qp = _mla_prep(z, cos_p, sin_p, pw, tm_p)
        y_mla = _mla_prompt(qp, kv, rk, wuv, bp, tp, min(tp, MLA_TQ))
        mk, mv = _memkv(mem2d, mem_norm[l], mem_w_k[l].astype(BF16), mem_w_v[l].astype(BF16), mem_k_gain[l],
                        min(bp * mem_len, 512))
        rows_m = mem_len * N_HEADS
        y_mem = _mem_attn(z3, mk.reshape(1, bp, rows_m, HEAD_W), mv.reshape(1, bp, rows_m, HEAD_W), mem_q_gain[l],
                          0, 1, tm_p)
        xp = _outproj(xp, z, (y_conv, y_hg.reshape(n_p, br), y_mla, y_mem.reshape(n_p, br)), wb, wo, tm_p)
        outs["p_lat"].append(c.reshape(bp, tp, MLA_KV_RANK))
        outs["p_rope"].append(r.reshape(bp, tp, MLA_ROPE))
        outs["p_hg"].append(s_fin)
        outs["p_conv"].append(tail.reshape(bp, 8, br)[:, 8 - (CONV_K - 1):])
        outs["p_mk"].append(mk.reshape(bp, mem_len, N_HEADS, HEAD_W))
        outs["p_mv"].append(mv.reshape(bp, mem_len, N_HEADS, HEAD_W))

        z = _inproj(xs, norm_gain[l], w_in_l, tm_s, 2048)
        z3 = z.reshape(bs, tpad, N_COLS)
        hist = jnp.pad(state_conv[l], ((0, 0), (0, tpad - (CONV_K - 1)), (0, 0))).reshape(n_s, br)
        y_conv, u_all = _conv(z, hist, conv_w[l], tpad, tm_s, tm_s)
        sb = 8 if bs % 8 == 0 else 1
        y_hg, s_fin = _hgrn(z3, lbs[l], hgrn_norm[l], state_hgrn, l, sb, tpad, tpad, ts)
        c, r, kv, rk, qp = _mla_prep(z, cos_s, sin_s, pw, tm_s)
        qp3 = qp.reshape(N_HEADS, bs, tpad, 256).transpose(1, 0, 2, 3).reshape(bs, N_HEADS * tpad, 256)
        rt3 = jnp.pad(r.reshape(bs, tpad, MLA_ROPE).transpose(0, 2, 1), ((0, 0), (0, 0), (0, PAGE - tpad)))
        y_mla = _mla_sample(page_flat, qp3, c.reshape(bs, tpad, MLA_KV_RANK), rt3, pw["wukt"], wuv,
                            cache_mla_latent, rope_t, l, n_pages)
        y_mem = _mem_attn(z3, mem_k4, mem_v4, mem_q_gain[l], l, sb, tpad)
        xs = _outproj(xs, z, (y_conv, y_hg.reshape(n_s, br), y_mla.reshape(n_s, br), y_mem.reshape(n_s, br)),
                      wb, wo, tm_s)
        outs["s_lat"].append(c.reshape(bs, tpad, MLA_KV_RANK)[:, :ts])
        outs["s_rope"].append(r.reshape(bs, tpad, MLA_ROPE)[:, :ts])
        outs["s_hg"].append(s_fin)
        outs["s_conv"].append(u_all.reshape(bs, tpad, br)[:, ts - (CONV_K - 1):ts])

    st = lambda k: jnp.stack(outs[k])
    return (xp.reshape(bp, tp, d), xs.reshape(bs, tpad, d)[:, :ts], st("p_lat"), st("p_rope"), st("p_hg"),
            st("p_conv"), st("p_mk"), st("p_mv"), st("s_lat"), st("s_rope"), st("s_hg"), st("s_conv"))
```

```python
import functools

import numpy as np
import jax
import jax.numpy as jnp
from jax import lax
from jax.experimental import pallas as pl
from jax.experimental.pallas import tpu as pltpu

F32 = jnp.float32
BF16 = jnp.bfloat16

N_HEADS = 4
HEAD_W = 128
MLA_NOPE = 64
MLA_ROPE = 32
MLA_QK = MLA_NOPE + MLA_ROPE
MLA_Q_RANK = 192
MLA_KV_RANK = 128
CONV_K = 3
ROPE_THETA = 10000.0
EPS = 1e-6
NEG = -1e30
PAGE = 128
SAMPLE_PAD_T = 8
HG_CHUNK = 64
HG_BLOCK = 8
MLA_ROW_GROUP = 128
MLA_TQ = 512
LOG2E = 1.4426950408889634
VMEM_LIMIT = 56 * 1024 * 1024

COL_CONV = 0
COL_HG = 1536
COL_MLA = 3072
COL_MEMQ = 3584
COL_MG = 4096
COL_SG = 8192
N_COLS = 10240


def _cparams(sem):
    return pltpu.CompilerParams(dimension_semantics=sem, vmem_limit_bytes=VMEM_LIMIT)


def _rms(x, g):
    return x * lax.rsqrt(jnp.mean(x * x, axis=-1, keepdims=True) + EPS) * g


def _sigmoid(x):
    return 0.5 * jnp.tanh(0.5 * x) + 0.5


def _dot(a, b):
    return jnp.dot(a, b, preferred_element_type=F32)


def _dot_nt(a, b):
    return lax.dot_general(a, b, (((1,), (1,)), ((), ())), preferred_element_type=F32)


def _dot_tn(a, b):
    return lax.dot_general(a, b, (((0,), (0,)), ((), ())), preferred_element_type=F32)


def _inproj_kernel(x_ref, g_ref, w_ref, z_ref, hn_ref):
    @pl.when(pl.program_id(1) == 0)
    def _():
        hn_ref[...] = _rms(x_ref[...], g_ref[...]).astype(BF16)

    z_ref[...] = _dot(hn_ref[...], w_ref[...]).astype(BF16)


def _relayout_kernel(w_ref, o_ref):
    w = w_ref[...]
    zeros = lambda n: jnp.zeros((w.shape[0], n), F32)
    o_q, o_kv, o_pe, o_mq, o_sg, o_mg = 3072, 3264, 3392, 3424, 3936, 5984
    half = MLA_ROPE // 2
    cols = [w[:, :o_q],
            w[:, o_q:o_kv], zeros(64),
            w[:, o_kv:o_pe],
            w[:, o_pe:o_pe + half], zeros(48), w[:, o_pe + half:o_mq], zeros(48),
            w[:, o_mq:o_sg],
            w[:, o_mg:] * 0.5,
            w[:, o_sg:o_mg] * 0.5]
    o_ref[...] = jnp.concatenate(cols, axis=1).astype(BF16)


def _relayout_w_in(w_all, tr):
    depth, d, n_in = w_all.shape
    return pl.pallas_call(
        _relayout_kernel,
        grid=(depth, d // tr),
        in_specs=[pl.BlockSpec((None, tr, n_in), lambda l, i: (l, i, 0))],
        out_specs=pl.BlockSpec((None, tr, N_COLS), lambda l, i: (l, i, 0)),
        out_shape=jax.ShapeDtypeStruct((depth, d, N_COLS), BF16),
        compiler_params=_cparams(("arbitrary", "arbitrary")),
        name="w_relayout",
    )(w_all)


def _inproj(x2d, g, w_bf, layer, tm, tn):
    n, d = x2d.shape
    ncol = w_bf.shape[2]
    return pl.pallas_call(
        _inproj_kernel,
        grid=(n // tm, ncol // tn),
        in_specs=[pl.BlockSpec((tm, d), lambda i, j: (i, 0)),
                  pl.BlockSpec((1, d), lambda i, j: (0, 0)),
                  pl.BlockSpec((None, d, tn), lambda i, j: (layer, 0, j))],
        out_specs=pl.BlockSpec((tm, tn), lambda i, j: (i, j)),
        out_shape=jax.ShapeDtypeStruct((n, ncol), BF16),
        scratch_shapes=[pltpu.VMEM((tm, d), BF16)],
        compiler_params=_cparams(("arbitrary", "arbitrary")),
        name="inproj",
    )(x2d, g.reshape(1, d), w_bf)


def _conv_kernel(*refs, seq_t, tr, tail_rows, has_hist):
    if has_hist:
        z_ref, halo_ref, hist_ref, w_ref, y_ref, tail_ref = refs
    else:
        z_ref, halo_ref, w_ref, y_ref, tail_ref = refs
    i = pl.program_id(0)
    br = y_ref.shape[1]
    z = z_ref[...].astype(F32)
    u = z[:, 2 * br:3 * br] * z[:, 0:br]
    zh = halo_ref[...].astype(F32)
    uh = zh[:, 2 * br:3 * br] * zh[:, 0:br]
    loc = lax.broadcasted_iota(jnp.int32, (tr, 1), 0)
    t = (loc + i * tr) % seq_t
    u1 = jnp.where(loc == 0, uh[7:8], pltpu.roll(u, 1, axis=0))
    u2 = jnp.where(loc == 0, uh[6:7], jnp.where(loc == 1, uh[7:8], pltpu.roll(u, 2, axis=0)))
    if has_hist:
        hp = hist_ref[...]
        u1 = jnp.where(t == 0, pltpu.roll(hp, tr - 1, axis=0), u1)
        u2 = jnp.where(t < 2, hp, u2)
    else:
        u1 = jnp.where(t == 0, 0.0, u1)
        u2 = jnp.where(t < 2, 0.0, u2)
    w = w_ref[...]
    conv = w[0:1] * u2 + w[1:2] * u1 + w[2:3] * u
    y_ref[...] = (z[:, br:2 * br] * conv).astype(BF16)

    @pl.when(((i + 1) * tr) % max(seq_t, tr) == 0)
    def _():
        tail_ref[...] = u[tr - tail_rows:, :]


def _conv(z2d, hist_rows, w, seq_t, tr, tail_rows):
    n = z2d.shape[0]
    br = w.shape[1]
    has_hist = hist_rows is not None
    group = max(seq_t, tr)
    n_tail = (n // group) * tail_rows
    in_specs = [pl.BlockSpec((tr, 3 * br), lambda i: (i, 0)),
                pl.BlockSpec((8, 3 * br), lambda i: (jnp.maximum(i * (tr // 8) - 1, 0), 0))]
    args = [z2d, z2d]
    if has_hist:
        in_specs.append(pl.BlockSpec((tr, br), lambda i: (i, 0)))
        args.append(hist_rows)
    in_specs.append(pl.BlockSpec((CONV_K, br), lambda i: (0, 0)))
    args.append(w)
    return pl.pallas_call(
        functools.partial(_conv_kernel, seq_t=seq_t, tr=tr, tail_rows=tail_rows, has_hist=has_hist),
        grid=(n // tr,),
        in_specs=in_specs,
        out_specs=[pl.BlockSpec((tr, br), lambda i: (i, 0)),
                   pl.BlockSpec((tail_rows, br), lambda i: ((i * tr) // group, 0))],
        out_shape=[jax.ShapeDtypeStruct((n, br), BF16), jax.ShapeDtypeStruct((n_tail, br), F32)],
        compiler_params=_cparams(("arbitrary",)),
        name="conv",
    )(*args)


def _hgrn_intra_diag(q, kk, b2, chunk):
    lane = lax.broadcasted_iota(jnp.int32, (HG_BLOCK, chunk), 1)
    srow = lax.broadcasted_iota(jnp.int32, (HG_BLOCK, chunk), 0)
    at_rows = []
    for r in range(0, chunk, HG_BLOCK):
        b_blk = b2[r:r + HG_BLOCK]
        k_blk = kk[r:r + HG_BLOCK]
        acc = jnp.zeros((HG_BLOCK, chunk), F32)
        for t in range(HG_BLOCK):
            p = jnp.exp2(jnp.minimum(b2[r + t:r + t + 1] - b_blk, 0.0)) * (q[r + t:r + t + 1] * k_blk)
            acc = jnp.where(lane == r + t, jnp.sum(p, axis=-1, keepdims=True), acc)
        at_rows.append(jnp.where(lane - r >= srow, acc, 0.0))
    return at_rows[0] if len(at_rows) == 1 else jnp.concatenate(at_rows, axis=0)


def _hgrn_inter_block_factors(q, kk, b2, chunk):
    nb = chunk // HG_BLOCK
    zero = jnp.zeros((HG_BLOCK, HEAD_W), F32)
    q_rows, k_rows = [], []
    for i in range(nb):
        blk = slice(i * HG_BLOCK, (i + 1) * HG_BLOCK)
        q_tiles, k_tiles = [], []
        for j in range(nb - 1):
            rho = b2[(j + 1) * HG_BLOCK - 1:(j + 1) * HG_BLOCK]
            q_tiles.append(q[blk] * jnp.exp2(b2[blk] - rho) if j < i else zero)
            k_tiles.append(kk[blk] * jnp.exp2(rho - b2[blk]) if j == i else zero)
        q_rows.append(jnp.concatenate(q_tiles, axis=1))
        k_rows.append(jnp.concatenate(k_tiles, axis=1))
    return jnp.concatenate(q_rows, axis=0).astype(BF16), jnp.concatenate(k_rows, axis=0).astype(BF16)


def _hgrn_tile(load, store, lb, gain, sts, valid_fn, tril_bf, chunk, n_c):
    units = [(s, c) for s in range(len(sts)) for c in range(n_c)]
    chunks = range(len(units))
    q, v, kk, g_hi, g_lo = [], [], [], [], []
    for s, c in units:
        qc, zf, vc = load(s, c)
        f = lb + (1.0 - lb) * _sigmoid(zf)
        g = jnp.log(f)
        k = 1.0 - f
        valid = valid_fn(c)
        if valid is not None:
            g = jnp.where(valid, g, 0.0)
            k = jnp.where(valid, k, 0.0)
        hi = g.astype(BF16)
        q.append(qc)
        v.append(vc.astype(BF16))
        kk.append(k)
        g_hi.append(hi)
        g_lo.append((g - hi.astype(F32)).astype(BF16))
    b2 = [(_dot(tril_bf, g_hi[c]) + _dot(tril_bf, g_lo[c])) * LOG2E for c in chunks]
    bl2 = [b[chunk - 1:chunk, :] for b in b2]
    upd = [_dot_tn(v[c], (kk[c] * jnp.exp2(bl2[c] - b2[c])).astype(BF16)) for c in chunks]
    at = [_hgrn_intra_diag(q[c], kk[c], b2[c], chunk) for c in chunks]
    if chunk > HG_BLOCK:
        factors = [_hgrn_inter_block_factors(q[c], kk[c], b2[c], chunk) for c in chunks]
        at = [at[c] + _dot_nt(factors[c][1], factors[c][0]) for c in chunks]
    o = [_dot_tn(at[c].astype(BF16), v[c]) for c in chunks]
    sts = list(sts)
    states = []
    for u, (s, _) in enumerate(units):
        states.append(sts[s].astype(BF16))
        sts[s] = sts[s] * jnp.exp2(bl2[u]) + upd[u]
    for u, (s, c) in enumerate(units):
        oc = o[u] + _dot_nt((q[u] * jnp.exp2(b2[u])).astype(BF16), states[u])
        store(s, c, _rms(oc, gain).astype(BF16))
    return sts


def _hgrn_kernel(q_ref, f_ref, i_ref, lb_ref, g_ref, s0_ref, y_ref, sout_ref, st_ref, *, sb, tt, chunk, t_valid):
    tstep = pl.program_id(2)
    n_t = pl.num_programs(2)

    @pl.when(tstep == 0)
    def _():
        for s in range(sb):
            st_ref[s] = s0_ref[s, 0].T

    lb = lb_ref[0]
    gain = g_ref[0]
    row = lax.broadcasted_iota(jnp.int32, (chunk, chunk), 0)
    col = lax.broadcasted_iota(jnp.int32, (chunk, chunk), 1)
    tril_bf = (row >= col).astype(BF16)
    scale = HEAD_W ** -0.5

    def load(s, c):
        rows = pl.ds(c * chunk, chunk)
        return (q_ref[s, rows, :].astype(F32) * scale, f_ref[s, rows, :].astype(F32),
                i_ref[s, rows, :].astype(F32))

    def store(s, c, y):
        y_ref[s, pl.ds(c * chunk, chunk), :] = y

    def valid_fn(c):
        if t_valid is None:
            return None
        return tstep * tt + c * chunk + lax.broadcasted_iota(jnp.int32, (chunk, 1), 0) < t_valid

    new_states = _hgrn_tile(load, store, lb, gain, [st_ref[s] for s in range(sb)], valid_fn, tril_bf, chunk,
                            tt // chunk)
    for s in range(sb):
        st_ref[s] = new_states[s]

    @pl.when(tstep == n_t - 1)
    def _():
        for s in range(sb):
            sout_ref[s, 0] = st_ref[s].T


def _hgrn(z3d, lb, gain, s0, layer, sb, tt, chunk, t_valid):
    b, t, _ = z3d.shape
    cb = COL_HG // HEAD_W
    kern = functools.partial(_hgrn_kernel, sb=sb, tt=tt, chunk=chunk, t_valid=t_valid)
    return pl.pallas_call(
        kern,
        grid=(b // sb, N_HEADS, t // tt),
        in_specs=[pl.BlockSpec((sb, tt, HEAD_W), lambda i, h, k: (i, k, cb + h)),
                  pl.BlockSpec((sb, tt, HEAD_W), lambda i, h, k: (i, k, cb + N_HEADS + h)),
                  pl.BlockSpec((sb, tt, HEAD_W), lambda i, h, k: (i, k, cb + 2 * N_HEADS + h)),
                  pl.BlockSpec((1, 1, HEAD_W), lambda i, h, k: (h, 0, 0)),
                  pl.BlockSpec((1, 1, HEAD_W), lambda i, h, k: (h, 0, 0)),
                  pl.BlockSpec((None, sb, 1, HEAD_W, HEAD_W), lambda i, h, k: (layer, i, h, 0, 0))],
        out_specs=[pl.BlockSpec((sb, tt, HEAD_W), lambda i, h, k: (i, k, h)),
                   pl.BlockSpec((sb, 1, HEAD_W, HEAD_W), lambda i, h, k: (i, h, 0, 0))],
        out_shape=[jax.ShapeDtypeStruct((b, t, N_HEADS * HEAD_W), BF16),
                   jax.ShapeDtypeStruct((b, N_HEADS, HEAD_W, HEAD_W), F32)],
        scratch_shapes=[pltpu.VMEM((sb, HEAD_W, HEAD_W), F32)],
        compiler_params=_cparams(("arbitrary", "arbitrary", "arbitrary")),
        name="hgrn",
    )(z3d, z3d, z3d, lb.reshape(N_HEADS, 1, HEAD_W), gain.reshape(N_HEADS, 1, HEAD_W), s0)


def _mla_prep_kernel(z_ref, cos_ref, sin_ref, gq_ref, wuq_ref, gkv_ref, wuk_ref, wabs_ref, grope_ref,
                     indq_ref, indk_ref, c_ref, r_ref, kv_ref, rk_ref, qp_ref):
    z = z_ref[...].astype(F32)
    ql = z[:, 0:MLA_Q_RANK]
    kvl = z[:, 256:384]
    kpe = z[:, 384:512]
    cos = cos_ref[...]
    sin = sin_ref[...]
    lane = lax.broadcasted_iota(jnp.int32, (1, 128), 1)

    qf = _dot(_rms(ql, gq_ref[...]).astype(BF16), wuq_ref[...])
    q_nope = qf[:, 0:256]
    rq_in = qf[:, 256:384]
    rot_q = rq_in * cos + pltpu.roll(rq_in, 64, axis=1) * sin
    rot_k = kpe * cos + pltpu.roll(kpe, 64, axis=1) * sin

    c = _rms(kvl, gkv_ref[...])
    c_ref[...] = c
    r32 = jnp.where(lane < 16, rot_k, pltpu.roll(rot_k, 80, axis=1))
    r32 = jnp.where(lane < MLA_ROPE, r32, 0.0)
    r_ref[...] = r32[:, 0:MLA_ROPE]
    cb = c.astype(BF16)
    kv_ref[...] = jnp.concatenate([cb, r32.astype(BF16)], axis=1)

    kn = _dot(cb, wuk_ref[...])
    kcat2 = jnp.concatenate([kn * kn, rot_k * rot_k], axis=1).astype(BF16)
    ssk = _dot_nt(indk_ref[...], kcat2)
    rk_ref[...] = lax.rsqrt(ssk * (1.0 / MLA_QK) + EPS)

    qcat2 = jnp.concatenate([q_nope * q_nope, rot_q * rot_q], axis=1).astype(BF16)
    ssq = _dot(qcat2, indq_ref[...])
    rq = lax.rsqrt(ssq * (1.0 / MLA_QK) + EPS) * (MLA_QK ** -0.5 * LOG2E)

    q_abs = _dot(q_nope.astype(BF16), wabs_ref[...])
    grope = grope_ref[...]
    for h in range(N_HEADS):
        a = rot_q if h == 0 else pltpu.roll(rot_q, 128 - 16 * h, axis=1)
        qr = jnp.where(lane < 16, a, pltpu.roll(a, 80, axis=1)) * grope
        rq_h = rq[:, h:h + 1]
        qp_ref[h, :, 0:128] = (q_abs[:, h * 128:(h + 1) * 128] * rq_h).astype(BF16)
        qp_ref[h, :, 128:256] = (qr * rq_h).astype(BF16)


def _mla_prep(z2d, cos_t, sin_t, pw, tm):
    n = z2d.shape[0]
    n_tab = cos_t.shape[0] // tm
    full = lambda shape: pl.BlockSpec(shape, lambda i: (0,) * len(shape))
    return pl.pallas_call(
        _mla_prep_kernel,
        grid=(n // tm,),
        in_specs=[pl.BlockSpec((tm, 512), lambda i: (i, COL_MLA // 512)),
                  pl.BlockSpec((tm, 128), lambda i: (i % n_tab, 0)),
                  pl.BlockSpec((tm, 128), lambda i: (i % n_tab, 0)),
                  full((1, MLA_Q_RANK)), full((MLA_Q_RANK, 384)), full((1, MLA_KV_RANK)),
                  full((MLA_KV_RANK, 256)), full((256, 512)), full((1, 128)),
                  full((384, 128)), full((8, 384))],
        out_specs=[pl.BlockSpec((tm, MLA_KV_RANK), lambda i: (i, 0)),
                   pl.BlockSpec((tm, MLA_ROPE), lambda i: (i, 0)),
                   pl.BlockSpec((tm, 256), lambda i: (i, 0)),
                   pl.BlockSpec((8, tm), lambda i: (0, i)),
                   pl.BlockSpec((N_HEADS, tm, 256), lambda i: (0, i, 0))],
        out_shape=[jax.ShapeDtypeStruct((n, MLA_KV_RANK), F32),
                   jax.ShapeDtypeStruct((n, MLA_ROPE), F32),
                   jax.ShapeDtypeStruct((n, 256), BF16),
                   jax.ShapeDtypeStruct((8, n), F32),
                   jax.ShapeDtypeStruct((N_HEADS, n, 256), BF16)],
        compiler_params=_cparams(("arbitrary",)),
        name="mla_prep",
    )(z2d, cos_t, sin_t, pw["gq_norm"], pw["wuq"], pw["gkv_norm"], pw["wuk"], pw["wabs"], pw["grope"],
      pw["indq"], pw["indk"])


def _mla_prompt_kernel(qp_ref, kv_ref, rk_ref, wuv_ref, y_ref, m_ref, l_ref, acc_ref, *, tq, rb):
    i = pl.program_id(1)
    nrow = N_HEADS * tq
    m_ref[...] = jnp.full((nrow, 128), NEG, F32)
    l_ref[...] = jnp.zeros((nrow, 128), F32)
    acc_ref[...] = jnp.zeros((nrow, MLA_KV_RANK), F32)
    row = lax.broadcasted_iota(jnp.int32, (rb, 128), 0)
    col = lax.broadcasted_iota(jnp.int32, (rb, 128), 1)
    lane_tiles = range(0, tq, 128)

    def block(k0, diagonal):
        kv = kv_ref[pl.ds(k0, tq), :]
        cv = kv[:, 0:MLA_KV_RANK]

        def qk(h):
            return _dot_nt(qp_ref[h], kv)

        def softmax_update(h, s_h):
            p_rows = []
            for r in range(0, tq, rb):
                rows = pl.ds(h * tq + r, rb)
                tiles = []
                for c in lane_tiles:
                    s = s_h[r:r + rb, c:c + 128] * rk_ref[h:h + 1, pl.ds(k0 + c, 128)]
                    if diagonal:
                        s = jnp.where(col + c <= row + r, s, NEG)
                    tiles.append(s)
                m_old = m_ref[rows, :]
                m_new = jnp.maximum(m_old, jnp.max(functools.reduce(jnp.maximum, tiles), axis=-1, keepdims=True))
                alpha = jnp.exp2(m_old - m_new)
                probs = [jnp.exp2(s - m_new) for s in tiles]
                l_ref[rows, :] = alpha * l_ref[rows, :] + jnp.sum(functools.reduce(jnp.add, probs), axis=-1,
                                                                  keepdims=True)
                m_ref[rows, :] = m_new
                acc_ref[rows, :] = alpha * acc_ref[rows, :]
                p_rows.append(jnp.concatenate([p.astype(BF16) for p in probs], axis=1))
            return jnp.concatenate(p_rows, axis=0)

        def pv(h, p_h):
            acc_ref[pl.ds(h * tq, tq), :] += _dot(p_h, cv)

        s0 = qk(0)
        s1 = qk(1)
        p0 = softmax_update(0, s0)
        s2 = qk(2)
        p1 = softmax_update(1, s1)
        pv(0, p0)
        s3 = qk(3)
        p2 = softmax_update(2, s2)
        pv(1, p1)
        p3 = softmax_update(3, s3)
        pv(2, p2)
        pv(3, p3)

    def step(j, carry):
        block(pl.multiple_of(j * tq, tq), False)
        return carry

    lax.fori_loop(0, i, step, 0)
    block(pl.multiple_of(i * tq, tq), True)
    for h in range(N_HEADS):
        rows = pl.ds(h * tq, tq)
        o = (acc_ref[rows, :] / l_ref[rows, :]).astype(BF16)
        y_ref[:, h * HEAD_W:(h + 1) * HEAD_W] = _dot(o, wuv_ref[:, h * HEAD_W:(h + 1) * HEAD_W]).astype(BF16)


def _mla_prompt(qp, kv, rk, wuv, b, t, tq):
    n = b * t
    nq = t // tq
    nrow = N_HEADS * tq
    return pl.pallas_call(
        functools.partial(_mla_prompt_kernel, tq=tq, rb=min(tq, MLA_ROW_GROUP)),
        grid=(b, nq),
        scratch_shapes=[pltpu.VMEM((nrow, 128), F32), pltpu.VMEM((nrow, 128), F32),
                        pltpu.VMEM((nrow, MLA_KV_RANK), F32)],
        in_specs=[pl.BlockSpec((N_HEADS, tq, 256), lambda bi, i: (0, bi * nq + i, 0)),
                  pl.BlockSpec((t, 256), lambda bi, i: (bi, 0)),
                  pl.BlockSpec((8, t), lambda bi, i: (0, bi)),
                  pl.BlockSpec((MLA_KV_RANK, N_HEADS * HEAD_W), lambda bi, i: (0, 0))],
        out_specs=pl.BlockSpec((tq, N_HEADS * HEAD_W), lambda bi, i: (bi * nq + i, 0)),
        out_shape=jax.ShapeDtypeStruct((n, N_HEADS * HEAD_W), BF16),
        compiler_params=_cparams(("arbitrary", "arbitrary")),
        name="mla_prompt",
    )(qp, kv, rk, wuv)


def _mla_sample_kernel(pt_ref, qp_ref, cnew_ref, rnewt_ref, wukt_ref, wuv_ref, lat_hbm, ropet_hbm,
                       y_ref, cbuf, rbuf, cbf, s_scr, part_scr, sem, *, layer, n_pages, ck):
    b = pl.program_id(0)
    n_b = pl.num_programs(0)
    past = n_pages * PAGE
    tk = past + PAGE
    slot = b % 2
    tp = SAMPLE_PAD_T
    nrow = N_HEADS * tp

    def lat_copy(pg, p, sl):
        return pltpu.make_async_copy(lat_hbm.at[layer, pg], cbuf.at[sl, pl.ds(p * PAGE, PAGE), :], sem.at[0, sl])

    def rope_copy(pg, p, sl):
        return pltpu.make_async_copy(ropet_hbm.at[layer, pg], rbuf.at[sl, :, pl.ds(p * PAGE, PAGE)], sem.at[1, sl])

    def issue(seq, sl):
        def body(p, carry):
            pg = pt_ref[seq * n_pages + p]
            lat_copy(pg, p, sl).start()
            rope_copy(pg, p, sl).start()
            return carry
        lax.fori_loop(0, n_pages, body, 0, unroll=4)

    @pl.when(b == 0)
    def _():
        for sl in range(2):
            cbuf[sl, pl.ds(past, PAGE), :] = jnp.zeros((PAGE, MLA_KV_RANK), F32)
        issue(0, 0)

    def wait_all(sl):
        def wait_body(p, carry):
            lat_copy(0, p, sl).wait()
            rope_copy(0, p, sl).wait()
            return carry
        lax.fori_loop(0, n_pages, wait_body, 0, unroll=4)

    wait_all(slot)
    nxt = jnp.minimum(b + 1, n_b - 1)
    n_chunks = tk // ck
    pages_per_chunk = -(-n_pages // n_chunks)

    def prefetch(chunk_idx):
        for p in range(chunk_idx * pages_per_chunk, min((chunk_idx + 1) * pages_per_chunk, n_pages)):
            pg = pt_ref[nxt * n_pages + p]
            lat_copy(pg, p, 1 - slot).start()
            rope_copy(pg, p, 1 - slot).start()

    cbuf[slot, pl.ds(past, tp), :] = cnew_ref[0]
    rbuf[slot, :, pl.ds(past, PAGE)] = rnewt_ref[0]

    q = qp_ref[0]
    qr = q[:, MLA_KV_RANK:MLA_KV_RANK + MLA_ROPE]
    w_stack = jnp.concatenate([wukt_ref[...], q[:, 0:MLA_KV_RANK]], axis=0)
    n_up = N_HEADS * MLA_NOPE
    qpos = past + lax.broadcasted_iota(jnp.int32, (nrow, 1), 0) % tp
    head_ones = (lax.broadcasted_iota(jnp.int32, (nrow, nrow), 0) // tp
                 == lax.broadcasted_iota(jnp.int32, (nrow, nrow), 1) // 8).astype(BF16)

    for k0 in range(0, tk, ck):
        prefetch(k0 // ck)
        cb = cbuf[slot, pl.ds(k0, ck), :].astype(BF16)
        cbf[pl.ds(k0, ck), :] = cb
        rt = rbuf[slot, :, pl.ds(k0, ck)]
        big = _dot_nt(w_stack, cb)
        kn2 = big[0:n_up] * big[0:n_up]
        part_r = jnp.sum((rt * rt).reshape(MLA_ROPE // 8, 8, ck), axis=0)
        parts = [jnp.sum(kn2[h * MLA_NOPE:(h + 1) * MLA_NOPE].reshape(MLA_NOPE // 8, 8, ck), axis=0) + part_r
                 for h in range(N_HEADS)]
        part_scr[:, pl.ds(k0, ck)] = jnp.concatenate(parts, axis=0).astype(BF16)
        s_scr[:, pl.ds(k0, ck)] = big[n_up:n_up + nrow]

    rk = lax.rsqrt(_dot(head_ones, part_scr[...]) * (1.0 / MLA_QK) + EPS)
    s = (s_scr[...] + _dot(qr, rbuf[slot].astype(BF16))) * rk
    kpos = lax.broadcasted_iota(jnp.int32, (1, tk), 1)
    s = jnp.where(kpos <= qpos, s, NEG)
    m = jnp.max(s, axis=-1, keepdims=True)
    p = jnp.exp2(s - m)
    l = jnp.sum(p, axis=-1, keepdims=True)
    o = _dot(p.astype(BF16), cbf[...]) / l
    for h in range(N_HEADS):
        y_ref[0, :, h * HEAD_W:(h + 1) * HEAD_W] = _dot(
            o[h * tp:(h + 1) * tp].astype(BF16), wuv_ref[:, h * HEAD_W:(h + 1) * HEAD_W]).astype(BF16)

    @pl.when(b == n_b - 1)
    def _():
        wait_all(1 - slot)


def _mla_sample(page_flat, qp3, c3, rt3, wukt, wuv, lat, ropet, layer, n_pages):
    bs = qp3.shape[0]
    tp = SAMPLE_PAD_T
    nrow = N_HEADS * tp
    tk = n_pages * PAGE + PAGE
    ck = max(d for d in range(PAGE, 768 + 1, PAGE) if tk % d == 0)
    kern = functools.partial(_mla_sample_kernel, layer=layer, n_pages=n_pages, ck=ck)
    grid_spec = pltpu.PrefetchScalarGridSpec(
        num_scalar_prefetch=1,
        grid=(bs,),
        in_specs=[pl.BlockSpec((1, nrow, 256), lambda b, pt: (b, 0, 0)),
                  pl.BlockSpec((1, tp, MLA_KV_RANK), lambda b, pt: (b, 0, 0)),
                  pl.BlockSpec((1, MLA_ROPE, PAGE), lambda b, pt: (b, 0, 0)),
                  pl.BlockSpec((N_HEADS * MLA_NOPE, MLA_KV_RANK), lambda b, pt: (0, 0)),
                  pl.BlockSpec((MLA_KV_RANK, N_HEADS * HEAD_W), lambda b, pt: (0, 0)),
                  pl.BlockSpec(memory_space=pl.ANY),
                  pl.BlockSpec(memory_space=pl.ANY)],
        out_specs=pl.BlockSpec((1, tp, N_HEADS * HEAD_W), lambda b, pt: (b, 0, 0)),
        scratch_shapes=[pltpu.VMEM((2, tk, MLA_KV_RANK), F32),
                        pltpu.VMEM((2, MLA_ROPE, tk), F32),
                        pltpu.VMEM((tk, MLA_KV_RANK), BF16),
                        pltpu.VMEM((nrow, tk), F32),
                        pltpu.VMEM((nrow, tk), BF16),
                        pltpu.SemaphoreType.DMA((2, 2))],
    )
    return pl.pallas_call(
        kern,
        grid_spec=grid_spec,
        out_shape=jax.ShapeDtypeStruct((bs, tp, N_HEADS * HEAD_W), BF16),
        compiler_params=_cparams(("arbitrary",)),
        name="mla_sample",
    )(page_flat, qp3, c3, rt3, wukt, wuv, lat, ropet)


def _memkv_kernel(x_ref, g_ref, wk_ref, wv_ref, gk_ref, k_ref, v_ref):
    mn = _rms(x_ref[...], g_ref[...]).astype(BF16)
    k = _dot(mn, wk_ref[...])
    gk = gk_ref[...]
    tm = x_ref.shape[0]
    v = _dot(mn, wv_ref[...])
    for h in range(N_HEADS):
        sl = slice(h * HEAD_W, (h + 1) * HEAD_W)
        k_ref[pl.ds(h, tm, stride=N_HEADS), :] = _rms(k[:, sl], gk)
        v_ref[pl.ds(h, tm, stride=N_HEADS), :] = v[:, sl]


def _memkv(mem2d, g, wk, wv, gk, tm):
    n, d = mem2d.shape
    br = wk.shape[1]
    return pl.pallas_call(
        _memkv_kernel,
        grid=(n // tm,),
        in_specs=[pl.BlockSpec((tm, d), lambda i: (i, 0)),
                  pl.BlockSpec((1, d), lambda i: (0, 0)),
                  pl.BlockSpec((d, br), lambda i: (0, 0)),
                  pl.BlockSpec((d, br), lambda i: (0, 0)),
                  pl.BlockSpec((1, HEAD_W), lambda i: (0, 0))],
        out_specs=[pl.BlockSpec((tm * N_HEADS, HEAD_W), lambda i: (i, 0)),
                   pl.BlockSpec((tm * N_HEADS, HEAD_W), lambda i: (i, 0))],
        out_shape=[jax.ShapeDtypeStruct((n * N_HEADS, HEAD_W), F32),
                   jax.ShapeDtypeStruct((n * N_HEADS, HEAD_W), F32)],
        compiler_params=_cparams(("arbitrary",)),
        name="memkv",
    )(mem2d, g.reshape(1, d), wk, wv, gk.reshape(1, HEAD_W))


def _mem_attn_kernel(q_ref, k_ref, v_ref, gq_ref, y_ref, *, sb, m):
    gq = gq_ref[...]
    scale = HEAD_W ** -0.5
    heads = range(N_HEADS)
    for s in range(sb):
        q = q_ref[s].astype(F32)
        qh = [(_rms(q[:, h * HEAD_W:(h + 1) * HEAD_W], gq) * (scale * LOG2E)).astype(BF16) for h in heads]
        sc = [_dot_nt(qh[h], k_ref[s, pl.ds(h, m, stride=N_HEADS), :].astype(BF16)) for h in heads]
        ps, ls = [], []
        for h in heads:
            p = jnp.exp2(sc[h] - jnp.max(sc[h], axis=-1, keepdims=True))
            ls.append(jnp.sum(p, axis=-1, keepdims=True))
            ps.append(p.astype(BF16))
        for h in heads:
            o = _dot(ps[h], v_ref[s, pl.ds(h, m, stride=N_HEADS), :].astype(BF16)) / ls[h]
            y_ref[s, :, h * HEAD_W:(h + 1) * HEAD_W] = o.astype(BF16)


def _mem_attn(z3d, k4, v4, gq, layer, sb, tq):
    b, t, _ = z3d.shape
    m = k4.shape[2] // N_HEADS
    br = N_HEADS * HEAD_W
    return pl.pallas_call(
        functools.partial(_mem_attn_kernel, sb=sb, m=m),
        grid=(b // sb, t // tq),
        in_specs=[pl.BlockSpec((sb, tq, br), lambda i, j: (i, j, COL_MEMQ // br)),
                  pl.BlockSpec((None, sb, m * N_HEADS, HEAD_W), lambda i, j: (layer, i, 0, 0)),
                  pl.BlockSpec((None, sb, m * N_HEADS, HEAD_W), lambda i, j: (layer, i, 0, 0)),
                  pl.BlockSpec((1, HEAD_W), lambda i, j: (0, 0))],
        out_specs=pl.BlockSpec((sb, tq, br), lambda i, j: (i, j, 0)),
        out_shape=jax.ShapeDtypeStruct((b, t, br), BF16),
        compiler_params=_cparams(("arbitrary", "arbitrary")),
        name="mem_attn",
    )(z3d, k4, v4, gq.reshape(1, HEAD_W))


def _outproj_kernel(x_ref, mg_ref, sg_ref, y0_ref, y1_ref, y2_ref, y3_ref, wb_ref, wo_ref, o_ref):
    d = x_ref.shape[1]
    br = y0_ref.shape[1]
    merged = None
    for n, y_ref in enumerate((y0_ref, y1_ref, y2_ref, y3_ref)):
        sg = sg_ref[:, n * br:(n + 1) * br].astype(F32)
        ys = (y_ref[...].astype(F32) * (sg * (1.0 + jnp.tanh(sg)))).astype(BF16)
        proj = _dot(ys, wb_ref[n])
        term = (1.0 + jnp.tanh(mg_ref[:, n * d:(n + 1) * d].astype(F32))) * proj
        merged = term if merged is None else merged + term
    o_ref[...] = x_ref[...] + _dot(merged.astype(BF16), wo_ref[...])


def _outproj(x2d, z2d, ys, wb, wo, tm):
    n, d = x2d.shape
    br = ys[0].shape[1]
    nb = wb.shape[0]
    yspec = pl.BlockSpec((tm, br), lambda i: (i, 0))
    return pl.pallas_call(
        _outproj_kernel,
        grid=(n // tm,),
        in_specs=[pl.BlockSpec((tm, d), lambda i: (i, 0)),
                  pl.BlockSpec((tm, nb * d), lambda i: (i, COL_MG // (nb * d))),
                  pl.BlockSpec((tm, nb * br), lambda i: (i, COL_SG // (nb * br))),
                  yspec, yspec, yspec, yspec,
                  pl.BlockSpec((nb, br, d), lambda i: (0, 0, 0)),
                  pl.BlockSpec((d, d), lambda i: (0, 0))],
        out_specs=pl.BlockSpec((tm, d), lambda i: (i, 0)),
        out_shape=jax.ShapeDtypeStruct((n, d), F32),
        compiler_params=_cparams(("arbitrary",)),
        name="outproj",
    )(x2d, z2d, z2d, *ys, wb, wo)


def _rope_tables(pos):
    half = MLA_ROPE // 2
    freqs = ROPE_THETA ** (-np.arange(half, dtype=np.float64) / half)
    ang = np.asarray(pos, np.float64)[:, None] * freqs[None, :]
    cos = np.tile(np.cos(ang), (1, 8))
    sin = np.tile(np.sin(ang), (1, 8))
    sin[:, :64] *= -1.0
    return jnp.asarray(cos, F32), jnp.asarray(sin, F32)


def _mla_params(q_norm, w_uq, kv_norm, w_uk, gq, gk):
    half = MLA_ROPE // 2
    wq = w_uq.reshape(MLA_Q_RANK, N_HEADS, MLA_QK)
    wuq = jnp.concatenate([wq[:, :, :MLA_NOPE].reshape(MLA_Q_RANK, -1),
                           wq[:, :, MLA_NOPE:MLA_NOPE + half].reshape(MLA_Q_RANK, -1),
                           wq[:, :, MLA_NOPE + half:].reshape(MLA_Q_RANK, -1)], axis=1).astype(BF16)
    g2 = gq * gk
    wk = w_uk.reshape(MLA_KV_RANK, N_HEADS, MLA_NOPE)
    wabs = jnp.zeros((N_HEADS * MLA_NOPE, N_HEADS * MLA_KV_RANK), F32)
    for h in range(N_HEADS):
        blk = (wk[:, h, :] * g2[None, :MLA_NOPE]).T
        wabs = wabs.at[h * MLA_NOPE:(h + 1) * MLA_NOPE, h * MLA_KV_RANK:(h + 1) * MLA_KV_RANK].set(blk)
    grope = jnp.zeros((1, 128), F32).at[0, :MLA_ROPE].set(g2[MLA_NOPE:])
    j = np.arange(384)
    head_of = np.where(j < 256, j // MLA_NOPE, (j % 64) // half)
    indq = (head_of[:, None] == np.arange(128)[None, :]).astype(np.float32)
    lane = j - 256
    is_rope = (j >= 256) & ((lane < half) | ((lane >= 64) & (lane < 64 + half)))
    indk = np.zeros((8, 384), np.float32)
    for h in range(N_HEADS):
        indk[h] = ((j < 256) & (j // MLA_NOPE == h)) | is_rope
    return dict(gq_norm=q_norm.reshape(1, -1), wuq=wuq, gkv_norm=kv_norm.reshape(1, -1),
                wuk=w_uk.astype(BF16), wukt=w_uk.T.astype(BF16), wabs=wabs.astype(BF16), grope=grope,
                indq=jnp.asarray(indq, BF16), indk=jnp.asarray(indk, BF16))


def _hgrn_lower_bounds(lb_param):
    p = jax.nn.softmax(lb_param.astype(F32), axis=0)
    return jnp.cumsum(p, axis=0) - p[0]


def kernel(x_prompt, x_sample, mem_prompt, cache_mla_latent, cache_mla_rope, page_table, state_hgrn, state_conv, cache_mem_k, cache_mem_v, norm_gain, w_in, conv_w, hgrn_lb, hgrn_norm, mla_q_norm, mla_w_uq, mla_kv_norm, mla_w_uk, mla_w_uv, mla_q_gain, mla_k_gain, mem_norm, mem_w_k, mem_w_v, mem_q_gain, mem_k_gain, w_branch_out, w_out):
    bp, tp, d = x_prompt.shape
    bs, ts, _ = x_sample.shape
    depth = w_in.shape[0]
    br = conv_w.shape[2]
    mem_len = mem_prompt.shape[1]
    n_pages = page_table.shape[1]
    past = n_pages * cache_mla_latent.shape[2]
    tpad = SAMPLE_PAD_T
    n_p = bp * tp
    n_s = bs * tpad

    lbs = _hgrn_lower_bounds(hgrn_lb)
    cos_p, sin_p = _rope_tables(np.arange(tp))
    cos_s, sin_s = _rope_tables(past + np.arange(tpad))
    tm_s = min(n_s, 1024)
    cos_s = jnp.tile(cos_s, (tm_s // tpad, 1))
    sin_s = jnp.tile(sin_s, (tm_s // tpad, 1))
    page_flat = page_table.reshape(-1).astype(jnp.int32)
    rope_t = jnp.swapaxes(cache_mla_rope, 2, 3)
    mem_k4 = cache_mem_k.reshape(depth, bs, mem_len * N_HEADS, HEAD_W)
    mem_v4 = cache_mem_v.reshape(depth, bs, mem_len * N_HEADS, HEAD_W)

    tm_p = min(tp, 512)
    xp = x_prompt.reshape(n_p, d)
    xs = jnp.pad(x_sample, ((0, 0), (0, tpad - ts), (0, 0))).reshape(n_s, d)
    mem2d = mem_prompt.reshape(bp * mem_len, d)
    zero_state = jnp.zeros((1, bp, N_HEADS, HEAD_W, HEAD_W), F32)

    outs = {k: [] for k in ("p_lat", "p_rope", "p_hg", "p_conv", "p_mk", "p_mv", "s_lat", "s_rope", "s_hg", "s_conv")}
    w_in_r = _relayout_w_in(w_in, 128)
    for l in range(depth):
        pw = _mla_params(mla_q_norm[l], mla_w_uq[l], mla_kv_norm[l], mla_w_uk[l], mla_q_gain[l], mla_k_gain[l])
        wuv = mla_w_uv[l].astype(BF16)
        wb = (0.5 * w_branch_out[l]).astype(BF16)
        wo = w_out[l].astype(BF16)

        z = _inproj(xp, norm_gain[l], w_in_r, l, min(n_p, 1024), 2048)
        z3 = z.reshape(bp, tp, N_COLS)
        y_conv, tail = _conv(z, None, conv_w[l], tp, tm_p, 8)
        y_hg, s_fin = _hgrn(z3, lbs[l], hgrn_norm[l], zero_state, 0, 1, tm_p, HG_CHUNK, None)
        c, r, kv, rk, qp = _mla_prep(z, cos_p, sin_p, pw, tm_p)
        y_mla = _mla_prompt(qp, kv, rk, wuv, bp, tp, min(tp, MLA_TQ))
        mk, mv = _memkv(mem2d, mem_norm[l], mem_w_k[l].astype(BF16), mem_w_v[l].astype(BF16), mem_k_gain[l],
                        min(bp * mem_len, 512))
        rows_m = mem_len * N_HEADS
        y_mem = _mem_attn(z3, mk.reshape(1, bp, rows_m, HEAD_W), mv.reshape(1, bp, rows_m, HEAD_W), mem_q_gain[l],
                          0, 1, tm_p)
        xp = _outproj(xp, z, (y_conv, y_hg.reshape(n_p, br), y_mla, y_mem.reshape(n_p, br)), wb, wo, tm_p)
        outs["p_lat"].append(c.reshape(bp, tp, MLA_KV_RANK))
        outs["p_rope"].append(r.reshape(bp, tp, MLA_ROPE))
        outs["p_hg"].append(s_fin)
        outs["p_conv"].append(tail.reshape(bp, 8, br)[:, 8 - (CONV_K - 1):])
        outs["p_mk"].append(mk.reshape(bp, mem_len, N_HEADS, HEAD_W))
        outs["p_mv"].append(mv.reshape(bp, mem_len, N_HEADS, HEAD_W))

        z = _inproj(xs, norm_gain[l], w_in_r, l, tm_s, 2048)
        z3 = z.reshape(bs, tpad, N_COLS)
        hist = jnp.pad(state_conv[l], ((0, 0), (0, tpad - (CONV_K - 1)), (0, 0))).reshape(n_s, br)
        y_conv, u_all = _conv(z, hist, conv_w[l], tpad, tm_s, tm_s)
        sb = 8 if bs % 8 == 0 else 1
        y_hg, s_fin = _hgrn(z3, lbs[l], hgrn_norm[l], state_hgrn, l, sb, tpad, tpad, ts)
        c, r, kv, rk, qp = _mla_prep(z, cos_s, sin_s, pw, tm_s)
        qp3 = qp.reshape(N_HEADS, bs, tpad, 256).transpose(1, 0, 2, 3).reshape(bs, N_HEADS * tpad, 256)
        rt3 = jnp.pad(r.reshape(bs, tpad, MLA_ROPE).transpose(0, 2, 1), ((0, 0), (0, 0), (0, PAGE - tpad)))
        y_mla = _mla_sample(page_flat, qp3, c.reshape(bs, tpad, MLA_KV_RANK), rt3, pw["wukt"], wuv,
                            cache_mla_latent, rope_t, l, n_pages)
        y_mem = _mem_attn(z3, mem_k4, mem_v4, mem_q_gain[l], l, sb, tpad)
        xs = _outproj(xs, z, (y_conv, y_hg.reshape(n_s, br), y_mla.reshape(n_s, br), y_mem.reshape(n_s, br)),
                      wb, wo, tm_s)
        outs["s_lat"].append(c.reshape(bs, tpad, MLA_KV_RANK)[:, :ts])
        outs["s_rope"].append(r.reshape(bs, tpad, MLA_ROPE)[:, :ts])
        outs["s_hg"].append(s_fin)
        outs["s_conv"].append(u_all.reshape(bs, tpad, br)[:, ts - (CONV_K - 1):ts])

    st = lambda k: jnp.stack(outs[k])
    return (xp.reshape(bp, tp, d), xs.reshape(bs, tpad, d)[:, :ts], st("p_lat"), st("p_rope"), st("p_hg"),
            st("p_conv"), st("p_mk"), st("p_mv"), st("s_lat"), st("s_rope"), st("s_hg"), st("s_conv"))
```

```python
import functools

import numpy as np
import jax
import jax.numpy as jnp
from jax import lax
from jax.experimental import pallas as pl
from jax.experimental.pallas import tpu as pltpu

F32 = jnp.float32
BF16 = jnp.bfloat16

N_HEADS = 4
HEAD_W = 128
MLA_NOPE = 64
MLA_ROPE = 32
MLA_QK = MLA_NOPE + MLA_ROPE
MLA_Q_RANK = 192
MLA_KV_RANK = 128
CONV_K = 3
ROPE_THETA = 10000.0
EPS = 1e-6
NEG = -1e30
PAGE = 128
SAMPLE_PAD_T = 8
HG_CHUNK = 64
HG_BLOCK = 8
MLA_ROW_GROUP = 128
MLA_TQ = 512
LOG2E = 1.4426950408889634
VMEM_LIMIT = 56 * 1024 * 1024

COL_CONV = 0
COL_HG = 1536
COL_MLA = 3072
COL_MEMQ = 3584
COL_MG = 4096
COL_SG = 8192
N_COLS = 10240


def _cparams(sem):
    return pltpu.CompilerParams(dimension_semantics=sem, vmem_limit_bytes=VMEM_LIMIT)


def _rms(x, g):
    return x * lax.rsqrt(jnp.mean(x * x, axis=-1, keepdims=True) + EPS) * g


def _sigmoid(x):
    return 0.5 * jnp.tanh(0.5 * x) + 0.5


def _dot(a, b):
    return jnp.dot(a, b, preferred_element_type=F32)


def _dot_nt(a, b):
    return lax.dot_general(a, b, (((1,), (1,)), ((), ())), preferred_element_type=F32)


def _dot_tn(a, b):
    return lax.dot_general(a, b, (((0,), (0,)), ((), ())), preferred_element_type=F32)


def _inproj_kernel(x_ref, g_ref, w_ref, z_ref, hn_ref):
    @pl.when(pl.program_id(1) == 0)
    def _():
        hn_ref[...] = _rms(x_ref[...], g_ref[...]).astype(BF16)

    z_ref[...] = _dot_nt(hn_ref[...], w_ref[...]).astype(BF16)


def _relayout_w_in(w):
    wt = jnp.swapaxes(w, 0, 1)
    d = w.shape[0]
    rows = lambda a, b, scale=None: (wt[a:b] if scale is None else wt[a:b] * scale).astype(BF16)
    zeros = lambda n: jnp.zeros((n, d), BF16)
    o_q, o_kv, o_pe, o_mq, o_sg, o_mg = 3072, 3264, 3392, 3424, 3936, 5984
    half = MLA_ROPE // 2
    return jnp.concatenate([rows(0, o_q),
                            rows(o_q, o_kv), zeros(64),
                            rows(o_kv, o_pe),
                            rows(o_pe, o_pe + half), zeros(48), rows(o_pe + half, o_mq), zeros(48),
                            rows(o_mq, o_sg),
                            rows(o_mg, w.shape[1], 0.5),
                            rows(o_sg, o_mg, 0.5)], axis=0)


def _inproj(x2d, g, w_t, tm, tn):
    n, d = x2d.shape
    ncol = w_t.shape[0]
    return pl.pallas_call(
        _inproj_kernel,
        grid=(n // tm, ncol // tn),
        in_specs=[pl.BlockSpec((tm, d), lambda i, j: (i, 0)),
                  pl.BlockSpec((1, d), lambda i, j: (0, 0)),
                  pl.BlockSpec((tn, d), lambda i, j: (j, 0))],
        out_specs=pl.BlockSpec((tm, tn), lambda i, j: (i, j)),
        out_shape=jax.ShapeDtypeStruct((n, ncol), BF16),
        scratch_shapes=[pltpu.VMEM((tm, d), BF16)],
        compiler_params=_cparams(("arbitrary", "arbitrary")),
        name="inproj",
    )(x2d, g.reshape(1, d), w_t)


def _conv_kernel(*refs, seq_t, tr, tail_rows, has_hist):
    if has_hist:
        z_ref, halo_ref, hist_ref, w_ref, y_ref, tail_ref = refs
    else:
        z_ref, halo_ref, w_ref, y_ref, tail_ref = refs
    i = pl.program_id(0)
    br = y_ref.shape[1]
    z = z_ref[...].astype(F32)
    u = z[:, 2 * br:3 * br] * z[:, 0:br]
    zh = halo_ref[...].astype(F32)
    uh = zh[:, 2 * br:3 * br] * zh[:, 0:br]
    loc = lax.broadcasted_iota(jnp.int32, (tr, 1), 0)
    t = (loc + i * tr) % seq_t
    u1 = jnp.where(loc == 0, uh[7:8], pltpu.roll(u, 1, axis=0))
    u2 = jnp.where(loc == 0, uh[6:7], jnp.where(loc == 1, uh[7:8], pltpu.roll(u, 2, axis=0)))
    if has_hist:
        hp = hist_ref[...]
        u1 = jnp.where(t == 0, pltpu.roll(hp, tr - 1, axis=0), u1)
        u2 = jnp.where(t < 2, hp, u2)
    else:
        u1 = jnp.where(t == 0, 0.0, u1)
        u2 = jnp.where(t < 2, 0.0, u2)
    w = w_ref[...]
    conv = w[0:1] * u2 + w[1:2] * u1 + w[2:3] * u
    y_ref[...] = (z[:, br:2 * br] * conv).astype(BF16)

    @pl.when(((i + 1) * tr) % max(seq_t, tr) == 0)
    def _():
        tail_ref[...] = u[tr - tail_rows:, :]


def _conv(z2d, hist_rows, w, seq_t, tr, tail_rows):
    n = z2d.shape[0]
    br = w.shape[1]
    has_hist = hist_rows is not None
    group = max(seq_t, tr)
    n_tail = (n // group) * tail_rows
    in_specs = [pl.BlockSpec((tr, 3 * br), lambda i: (i, 0)),
                pl.BlockSpec((8, 3 * br), lambda i: (jnp.maximum(i * (tr // 8) - 1, 0), 0))]
    args = [z2d, z2d]
    if has_hist:
        in_specs.append(pl.BlockSpec((tr, br), lambda i: (i, 0)))
        args.append(hist_rows)
    in_specs.append(pl.BlockSpec((CONV_K, br), lambda i: (0, 0)))
    args.append(w)
    return pl.pallas_call(
        functools.partial(_conv_kernel, seq_t=seq_t, tr=tr, tail_rows=tail_rows, has_hist=has_hist),
        grid=(n // tr,),
        in_specs=in_specs,
        out_specs=[pl.BlockSpec((tr, br), lambda i: (i, 0)),
                   pl.BlockSpec((tail_rows, br), lambda i: ((i * tr) // group, 0))],
        out_shape=[jax.ShapeDtypeStruct((n, br), BF16), jax.ShapeDtypeStruct((n_tail, br), F32)],
        compiler_params=_cparams(("arbitrary",)),
        name="conv",
    )(*args)


def _hgrn_intra_diag(q, kk, b2, chunk):
    lane = lax.broadcasted_iota(jnp.int32, (HG_BLOCK, chunk), 1)
    srow = lax.broadcasted_iota(jnp.int32, (HG_BLOCK, chunk), 0)
    at_rows = []
    for r in range(0, chunk, HG_BLOCK):
        b_blk = b2[r:r + HG_BLOCK]
        k_blk = kk[r:r + HG_BLOCK]
        acc = jnp.zeros((HG_BLOCK, chunk), F32)
        for t in range(HG_BLOCK):
            p = jnp.exp2(jnp.minimum(b2[r + t:r + t + 1] - b_blk, 0.0)) * (q[r + t:r + t + 1] * k_blk)
            acc = jnp.where(lane == r + t, jnp.sum(p, axis=-1, keepdims=True), acc)
        at_rows.append(jnp.where(lane - r >= srow, acc, 0.0))
    return at_rows[0] if len(at_rows) == 1 else jnp.concatenate(at_rows, axis=0)


def _hgrn_inter_block_factors(q, kk, b2, chunk):
    nb = chunk // HG_BLOCK
    zero = jnp.zeros((HG_BLOCK, HEAD_W), F32)
    q_rows, k_rows = [], []
    for i in range(nb):
        blk = slice(i * HG_BLOCK, (i + 1) * HG_BLOCK)
        q_tiles, k_tiles = [], []
        for j in range(nb - 1):
            rho = b2[(j + 1) * HG_BLOCK - 1:(j + 1) * HG_BLOCK]
            q_tiles.append(q[blk] * jnp.exp2(b2[blk] - rho) if j < i else zero)
            k_tiles.append(kk[blk] * jnp.exp2(rho - b2[blk]) if j == i else zero)
        q_rows.append(jnp.concatenate(q_tiles, axis=1))
        k_rows.append(jnp.concatenate(k_tiles, axis=1))
    return jnp.concatenate(q_rows, axis=0).astype(BF16), jnp.concatenate(k_rows, axis=0).astype(BF16)


def _hgrn_tile(load, store, lb, gain, sts, valid_fn, tril_bf, chunk, n_c):
    units = [(s, c) for s in range(len(sts)) for c in range(n_c)]
    chunks = range(len(units))
    q, v, kk, g_hi, g_lo = [], [], [], [], []
    for s, c in units:
        qc, zf, vc = load(s, c)
        f = lb + (1.0 - lb) * _sigmoid(zf)
        g = jnp.log(f)
        k = 1.0 - f
        valid = valid_fn(c)
        if valid is not None:
            g = jnp.where(valid, g, 0.0)
            k = jnp.where(valid, k, 0.0)
        hi = g.astype(BF16)
        q.append(qc)
        v.append(vc.astype(BF16))
        kk.append(k)
        g_hi.append(hi)
        g_lo.append((g - hi.astype(F32)).astype(BF16))
    b2 = [(_dot(tril_bf, g_hi[c]) + _dot(tril_bf, g_lo[c])) * LOG2E for c in chunks]
    bl2 = [b[chunk - 1:chunk, :] for b in b2]
    upd = [_dot_tn(v[c], (kk[c] * jnp.exp2(bl2[c] - b2[c])).astype(BF16)) for c in chunks]
    at = [_hgrn_intra_diag(q[c], kk[c], b2[c], chunk) for c in chunks]
    if chunk > HG_BLOCK:
        factors = [_hgrn_inter_block_factors(q[c], kk[c], b2[c], chunk) for c in chunks]
        at = [at[c] + _dot_nt(factors[c][1], factors[c][0]) for c in chunks]
    o = [_dot_tn(at[c].astype(BF16), v[c]) for c in chunks]
    sts = list(sts)
    states = []
    for u, (s, _) in enumerate(units):
        states.append(sts[s].astype(BF16))
        sts[s] = sts[s] * jnp.exp2(bl2[u]) + upd[u]
    for u, (s, c) in enumerate(units):
        oc = o[u] + _dot_nt((q[u] * jnp.exp2(b2[u])).astype(BF16), states[u])
        store(s, c, _rms(oc, gain).astype(BF16))
    return sts


def _hgrn_kernel(*refs, sb, tt, chunk, t_valid, fill_slots):
    q_ref, f_ref, i_ref, lb_ref, g_ref, s0_ref = refs[:6]
    y_ref, sout_ref, st_ref = refs[-3:]
    tstep = pl.program_id(2)
    n_t = pl.num_programs(2)

    @pl.when(tstep == 0)
    def _():
        for s in range(sb):
            st_ref[s] = s0_ref[s, 0].T

    lb = lb_ref[0]
    gain = g_ref[0]
    row = lax.broadcasted_iota(jnp.int32, (chunk, chunk), 0)
    col = lax.broadcasted_iota(jnp.int32, (chunk, chunk), 1)
    tril_bf = (row >= col).astype(BF16)
    scale = HEAD_W ** -0.5

    def load(s, c):
        rows = pl.ds(c * chunk, chunk)
        return (q_ref[s, rows, :].astype(F32) * scale, f_ref[s, rows, :].astype(F32),
                i_ref[s, rows, :].astype(F32))

    def store(s, c, y):
        y_ref[s, pl.ds(c * chunk, chunk), :] = y

    def valid_fn(c):
        if t_valid is None:
            return None
        return tstep * tt + c * chunk + lax.broadcasted_iota(jnp.int32, (chunk, 1), 0) < t_valid

    new_states = _hgrn_tile(load, store, lb, gain, [st_ref[s] for s in range(sb)], valid_fn, tril_bf, chunk,
                            tt // chunk)
    for s in range(sb):
        st_ref[s] = new_states[s]

    @pl.when(tstep == n_t - 1)
    def _():
        for s in range(sb):
            final = st_ref[s].T
            if fill_slots is None:
                sout_ref[s, 0] = final
            else:
                for slot in range(fill_slots):
                    sout_ref[slot, s, 0] = final


def _hgrn(z3d, lb, gain, s0, s0_layer, layer, sb, tt, chunk, t_valid, depth, stacked_prev):
    b, t, _ = z3d.shape
    cb = COL_HG // HEAD_W
    first = stacked_prev is None
    assert first == (layer == 0)
    kern = functools.partial(_hgrn_kernel, sb=sb, tt=tt, chunk=chunk, t_valid=t_valid,
                             fill_slots=depth if first else None)
    state_spec = (pl.BlockSpec((depth, sb, 1, HEAD_W, HEAD_W), lambda i, h, k: (0, i, h, 0, 0)) if first else
                  pl.BlockSpec((None, sb, 1, HEAD_W, HEAD_W), lambda i, h, k: (layer, i, h, 0, 0)))
    in_specs = [pl.BlockSpec((sb, tt, HEAD_W), lambda i, h, k: (i, k, cb + h)),
                pl.BlockSpec((sb, tt, HEAD_W), lambda i, h, k: (i, k, cb + N_HEADS + h)),
                pl.BlockSpec((sb, tt, HEAD_W), lambda i, h, k: (i, k, cb + 2 * N_HEADS + h)),
                pl.BlockSpec((1, 1, HEAD_W), lambda i, h, k: (h, 0, 0)),
                pl.BlockSpec((1, 1, HEAD_W), lambda i, h, k: (h, 0, 0)),
                pl.BlockSpec((None, sb, 1, HEAD_W, HEAD_W), lambda i, h, k: (s0_layer, i, h, 0, 0))]
    args = [z3d, z3d, z3d, lb.reshape(N_HEADS, 1, HEAD_W), gain.reshape(N_HEADS, 1, HEAD_W), s0]
    aliases = {}
    if stacked_prev is not None:
        in_specs.append(pl.BlockSpec(memory_space=pl.ANY))
        args.append(stacked_prev)
        aliases = {len(args) - 1: 1}
    return pl.pallas_call(
        kern,
        grid=(b // sb, N_HEADS, t // tt),
        in_specs=in_specs,
        out_specs=[pl.BlockSpec((sb, tt, HEAD_W), lambda i, h, k: (i, k, h)),
                   state_spec],
        out_shape=[jax.ShapeDtypeStruct((b, t, N_HEADS * HEAD_W), BF16),
                   jax.ShapeDtypeStruct((depth, b, N_HEADS, HEAD_W, HEAD_W), F32)],
        scratch_shapes=[pltpu.VMEM((sb, HEAD_W, HEAD_W), F32)],
        input_output_aliases=aliases,
        compiler_params=_cparams(("arbitrary", "arbitrary", "arbitrary")),
        name="hgrn",
    )(*args)


def _mla_prep_kernel(z_ref, cos_ref, sin_ref, gq_ref, wuq_ref, gkv_ref, wuk_ref, wabs_ref, grope_ref,
                     indq_ref, indk_ref, c_ref, r_ref, kv_ref, rk_ref, qp_ref):
    z = z_ref[...].astype(F32)
    ql = z[:, 0:MLA_Q_RANK]
    kvl = z[:, 256:384]
    kpe = z[:, 384:512]
    cos = cos_ref[...]
    sin = sin_ref[...]
    lane = lax.broadcasted_iota(jnp.int32, (1, 128), 1)

    qf = _dot(_rms(ql, gq_ref[...]).astype(BF16), wuq_ref[...])
    q_nope = qf[:, 0:256]
    rq_in = qf[:, 256:384]
    rot_q = rq_in * cos + pltpu.roll(rq_in, 64, axis=1) * sin
    rot_k = kpe * cos + pltpu.roll(kpe, 64, axis=1) * sin

    c = _rms(kvl, gkv_ref[...])
    c_ref[...] = c
    r32 = jnp.where(lane < 16, rot_k, pltpu.roll(rot_k, 80, axis=1))
    r32 = jnp.where(lane < MLA_ROPE, r32, 0.0)
    r_ref[...] = r32[:, 0:MLA_ROPE]
    cb = c.astype(BF16)
    kv_ref[...] = jnp.concatenate([cb, r32.astype(BF16)], axis=1)

    kn = _dot(cb, wuk_ref[...])
    kcat2 = jnp.concatenate([kn * kn, rot_k * rot_k], axis=1).astype(BF16)
    ssk = _dot_nt(indk_ref[...], kcat2)
    rk_ref[...] = lax.rsqrt(ssk * (1.0 / MLA_QK) + EPS)

    qcat2 = jnp.concatenate([q_nope * q_nope, rot_q * rot_q], axis=1).astype(BF16)
    ssq = _dot(qcat2, indq_ref[...])
    rq = lax.rsqrt(ssq * (1.0 / MLA_QK) + EPS) * (MLA_QK ** -0.5 * LOG2E)

    q_abs = _dot(q_nope.astype(BF16), wabs_ref[...])
    grope = grope_ref[...]
    for h in range(N_HEADS):
        a = rot_q if h == 0 else pltpu.roll(rot_q, 128 - 16 * h, axis=1)
        qr = jnp.where(lane < 16, a, pltpu.roll(a, 80, axis=1)) * grope
        rq_h = rq[:, h:h + 1]
        qp_ref[h, :, 0:128] = (q_abs[:, h * 128:(h + 1) * 128] * rq_h).astype(BF16)
        qp_ref[h, :, 128:256] = (qr * rq_h).astype(BF16)


def _mla_prep(z2d, cos_t, sin_t, pw, tm):
    n = z2d.shape[0]
    n_tab = cos_t.shape[0] // tm
    full = lambda shape: pl.BlockSpec(shape, lambda i: (0,) * len(shape))
    return pl.pallas_call(
        _mla_prep_kernel,
        grid=(n // tm,),
        in_specs=[pl.BlockSpec((tm, 512), lambda i: (i, COL_MLA // 512)),
                  pl.BlockSpec((tm, 128), lambda i: (i % n_tab, 0)),
                  pl.BlockSpec((tm, 128), lambda i: (i % n_tab, 0)),
                  full((1, MLA_Q_RANK)), full((MLA_Q_RANK, 384)), full((1, MLA_KV_RANK)),
                  full((MLA_KV_RANK, 256)), full((256, 512)), full((1, 128)),
                  full((384, 128)), full((8, 384))],
        out_specs=[pl.BlockSpec((tm, MLA_KV_RANK), lambda i: (i, 0)),
                   pl.BlockSpec((tm, MLA_ROPE), lambda i: (i, 0)),
                   pl.BlockSpec((tm, 256), lambda i: (i, 0)),
                   pl.BlockSpec((8, tm), lambda i: (0, i)),
                   pl.BlockSpec((N_HEADS, tm, 256), lambda i: (0, i, 0))],
        out_shape=[jax.ShapeDtypeStruct((n, MLA_KV_RANK), F32),
                   jax.ShapeDtypeStruct((n, MLA_ROPE), F32),
                   jax.ShapeDtypeStruct((n, 256), BF16),
                   jax.ShapeDtypeStruct((8, n), F32),
                   jax.ShapeDtypeStruct((N_HEADS, n, 256), BF16)],
        compiler_params=_cparams(("arbitrary",)),
        name="mla_prep",
    )(z2d, cos_t, sin_t, pw["gq_norm"], pw["wuq"], pw["gkv_norm"], pw["wuk"], pw["wabs"], pw["grope"],
      pw["indq"], pw["indk"])


def _mla_prompt_kernel(qp_ref, kv_ref, rk_ref, wuv_ref, y_ref, m_ref, l_ref, acc_ref, *, tq, rb):
    i = pl.program_id(1)
    nrow = N_HEADS * tq
    m_ref[...] = jnp.full((nrow, 128), NEG, F32)
    l_ref[...] = jnp.zeros((nrow, 128), F32)
    acc_ref[...] = jnp.zeros((nrow, MLA_KV_RANK), F32)
    row = lax.broadcasted_iota(jnp.int32, (rb, 128), 0)
    col = lax.broadcasted_iota(jnp.int32, (rb, 128), 1)
    lane_tiles = range(0, tq, 128)

    def block(k0, diagonal):
        kv = kv_ref[pl.ds(k0, tq), :]
        cv = kv[:, 0:MLA_KV_RANK]

        def qk(h):
            return _dot_nt(qp_ref[h], kv)

        def softmax_update(h, s_h):
            p_rows = []
            for r in range(0, tq, rb):
                rows = pl.ds(h * tq + r, rb)
                tiles = []
                for c in lane_tiles:
                    s = s_h[r:r + rb, c:c + 128] * rk_ref[h:h + 1, pl.ds(k0 + c, 128)]
                    if diagonal:
                        s = jnp.where(col + c <= row + r, s, NEG)
                    tiles.append(s)
                m_old = m_ref[rows, :]
                m_new = jnp.maximum(m_old, jnp.max(functools.reduce(jnp.maximum, tiles), axis=-1, keepdims=True))
                alpha = jnp.exp2(m_old - m_new)
                probs = [jnp.exp2(s - m_new) for s in tiles]
                l_ref[rows, :] = alpha * l_ref[rows, :] + jnp.sum(functools.reduce(jnp.add, probs), axis=-1,
                                                                  keepdims=True)
                m_ref[rows, :] = m_new
                acc_ref[rows, :] = alpha * acc_ref[rows, :]
                p_rows.append(jnp.concatenate([p.astype(BF16) for p in probs], axis=1))
            return jnp.concatenate(p_rows, axis=0)

        def pv(h, p_h):
            acc_ref[pl.ds(h * tq, tq), :] += _dot(p_h, cv)

        s0 = qk(0)
        s1 = qk(1)
        p0 = softmax_update(0, s0)
        s2 = qk(2)
        p1 = softmax_update(1, s1)
        pv(0, p0)
        s3 = qk(3)
        p2 = softmax_update(2, s2)
        pv(1, p1)
        p3 = softmax_update(3, s3)
        pv(2, p2)
        pv(3, p3)

    def step(j, carry):
        block(pl.multiple_of(j * tq, tq), False)
        return carry

    lax.fori_loop(0, i, step, 0)
    block(pl.multiple_of(i * tq, tq), True)
    for h in range(N_HEADS):
        rows = pl.ds(h * tq, tq)
        o = (acc_ref[rows, :] / l_ref[rows, :]).astype(BF16)
        y_ref[:, h * HEAD_W:(h + 1) * HEAD_W] = _dot(o, wuv_ref[:, h * HEAD_W:(h + 1) * HEAD_W]).astype(BF16)


def _mla_prompt(qp, kv, rk, wuv, b, t, tq):
    n = b * t
    nq = t // tq
    nrow = N_HEADS * tq
    return pl.pallas_call(
        functools.partial(_mla_prompt_kernel, tq=tq, rb=min(tq, MLA_ROW_GROUP)),
        grid=(b, nq),
        scratch_shapes=[pltpu.VMEM((nrow, 128), F32), pltpu.VMEM((nrow, 128), F32),
                        pltpu.VMEM((nrow, MLA_KV_RANK), F32)],
        in_specs=[pl.BlockSpec((N_HEADS, tq, 256), lambda bi, i: (0, bi * nq + i, 0)),
                  pl.BlockSpec((t, 256), lambda bi, i: (bi, 0)),
                  pl.BlockSpec((8, t), lambda bi, i: (0, bi)),
                  pl.BlockSpec((MLA_KV_RANK, N_HEADS * HEAD_W), lambda bi, i: (0, 0))],
        out_specs=pl.BlockSpec((tq, N_HEADS * HEAD_W), lambda bi, i: (bi * nq + i, 0)),
        out_shape=jax.ShapeDtypeStruct((n, N_HEADS * HEAD_W), BF16),
        compiler_params=_cparams(("arbitrary", "arbitrary")),
        name="mla_prompt",
    )(qp, kv, rk, wuv)


def _mla_sample_kernel(pt_ref, qp_ref, cnew_ref, rnewt_ref, wukt_ref, wuv_ref, lat_hbm, ropet_hbm,
                       y_ref, cbuf, rbuf, cbf, s_scr, part_scr, sem, *, layer, n_pages, ck):
    b = pl.program_id(0)
    n_b = pl.num_programs(0)
    past = n_pages * PAGE
    tk = past + PAGE
    slot = b % 2
    tp = SAMPLE_PAD_T
    nrow = N_HEADS * tp

    def lat_copy(pg, p, sl):
        return pltpu.make_async_copy(lat_hbm.at[layer, pg], cbuf.at[sl, pl.ds(p * PAGE, PAGE), :], sem.at[0, sl])

    def rope_copy(pg, p, sl):
        return pltpu.make_async_copy(ropet_hbm.at[layer, pg], rbuf.at[sl, :, pl.ds(p * PAGE, PAGE)], sem.at[1, sl])

    def issue(seq, sl):
        def body(p, carry):
            pg = pt_ref[seq * n_pages + p]
            lat_copy(pg, p, sl).start()
            rope_copy(pg, p, sl).start()
            return carry
        lax.fori_loop(0, n_pages, body, 0, unroll=4)

    @pl.when(b == 0)
    def _():
        for sl in range(2):
            cbuf[sl, pl.ds(past, PAGE), :] = jnp.zeros((PAGE, MLA_KV_RANK), F32)
        issue(0, 0)

    def wait_all(sl):
        def wait_body(p, carry):
            lat_copy(0, p, sl).wait()
            rope_copy(0, p, sl).wait()
            return carry
        lax.fori_loop(0, n_pages, wait_body, 0, unroll=4)

    wait_all(slot)
    nxt = jnp.minimum(b + 1, n_b - 1)
    n_chunks = tk // ck
    pages_per_chunk = -(-n_pages // n_chunks)

    def prefetch(chunk_idx):
        for p in range(chunk_idx * pages_per_chunk, min((chunk_idx + 1) * pages_per_chunk, n_pages)):
            pg = pt_ref[nxt * n_pages + p]
            lat_copy(pg, p, 1 - slot).start()
            rope_copy(pg, p, 1 - slot).start()

    cbuf[slot, pl.ds(past, tp), :] = cnew_ref[0]
    rbuf[slot, :, pl.ds(past, PAGE)] = rnewt_ref[0]

    q = qp_ref[0]
    qr = q[:, MLA_KV_RANK:MLA_KV_RANK + MLA_ROPE]
    w_stack = jnp.concatenate([wukt_ref[...], q[:, 0:MLA_KV_RANK]], axis=0)
    n_up = N_HEADS * MLA_NOPE
    qpos = past + lax.broadcasted_iota(jnp.int32, (nrow, 1), 0) % tp
    head_ones = (lax.broadcasted_iota(jnp.int32, (nrow, nrow), 0) // tp
                 == lax.broadcasted_iota(jnp.int32, (nrow, nrow), 1) // 8).astype(BF16)

    for k0 in range(0, tk, ck):
        prefetch(k0 // ck)
        cb = cbuf[slot, pl.ds(k0, ck), :].astype(BF16)
        cbf[pl.ds(k0, ck), :] = cb
        rt = rbuf[slot, :, pl.ds(k0, ck)]
        big = _dot_nt(w_stack, cb)
        kn2 = big[0:n_up] * big[0:n_up]
        part_r = jnp.sum((rt * rt).reshape(MLA_ROPE // 8, 8, ck), axis=0)
        parts = [jnp.sum(kn2[h * MLA_NOPE:(h + 1) * MLA_NOPE].reshape(MLA_NOPE // 8, 8, ck), axis=0) + part_r
                 for h in range(N_HEADS)]
        part_scr[:, pl.ds(k0, ck)] = jnp.concatenate(parts, axis=0).astype(BF16)
        s_scr[:, pl.ds(k0, ck)] = big[n_up:n_up + nrow]

    rk = lax.rsqrt(_dot(head_ones, part_scr[...]) * (1.0 / MLA_QK) + EPS)
    s = (s_scr[...] + _dot(qr, rbuf[slot].astype(BF16))) * rk
    kpos = lax.broadcasted_iota(jnp.int32, (1, tk), 1)
    s = jnp.where(kpos <= qpos, s, NEG)
    m = jnp.max(s, axis=-1, keepdims=True)
    p = jnp.exp2(s - m)
    l = jnp.sum(p, axis=-1, keepdims=True)
    o = _dot(p.astype(BF16), cbf[...]) / l
    for h in range(N_HEADS):
        y_ref[0, :, h * HEAD_W:(h + 1) * HEAD_W] = _dot(
            o[h * tp:(h + 1) * tp].astype(BF16), wuv_ref[:, h * HEAD_W:(h + 1) * HEAD_W]).astype(BF16)

    @pl.when(b == n_b - 1)
    def _():
        wait_all(1 - slot)


def _mla_sample(page_flat, qp3, c3, rt3, wukt, wuv, lat, ropet, layer, n_pages):
    bs = qp3.shape[0]
    tp = SAMPLE_PAD_T
    nrow = N_HEADS * tp
    tk = n_pages * PAGE + PAGE
    ck = max(d for d in range(PAGE, 768 + 1, PAGE) if tk % d == 0)
    kern = functools.partial(_mla_sample_kernel, layer=layer, n_pages=n_pages, ck=ck)
    grid_spec = pltpu.PrefetchScalarGridSpec(
        num_scalar_prefetch=1,
        grid=(bs,),
        in_specs=[pl.BlockSpec((1, nrow, 256), lambda b, pt: (b, 0, 0)),
                  pl.BlockSpec((1, tp, MLA_KV_RANK), lambda b, pt: (b, 0, 0)),
                  pl.BlockSpec((1, MLA_ROPE, PAGE), lambda b, pt: (b, 0, 0)),
                  pl.BlockSpec((N_HEADS * MLA_NOPE, MLA_KV_RANK), lambda b, pt: (0, 0)),
                  pl.BlockSpec((MLA_KV_RANK, N_HEADS * HEAD_W), lambda b, pt: (0, 0)),
                  pl.BlockSpec(memory_space=pl.ANY),
                  pl.BlockSpec(memory_space=pl.ANY)],
        out_specs=pl.BlockSpec((1, tp, N_HEADS * HEAD_W), lambda b, pt: (b, 0, 0)),
        scratch_shapes=[pltpu.VMEM((2, tk, MLA_KV_RANK), F32),
                        pltpu.VMEM((2, MLA_ROPE, tk), F32),
                        pltpu.VMEM((tk, MLA_KV_RANK), BF16),
                        pltpu.VMEM((nrow, tk), F32),
                        pltpu.VMEM((nrow, tk), BF16),
                        pltpu.SemaphoreType.DMA((2, 2))],
    )
    return pl.pallas_call(
        kern,
        grid_spec=grid_spec,
        out_shape=jax.ShapeDtypeStruct((bs, tp, N_HEADS * HEAD_W), BF16),
        compiler_params=_cparams(("arbitrary",)),
        name="mla_sample",
    )(page_flat, qp3, c3, rt3, wukt, wuv, lat, ropet)


def _memkv_kernel(x_ref, g_ref, wk_ref, wv_ref, gk_ref, k_ref, v_ref):
    mn = _rms(x_ref[...], g_ref[...]).astype(BF16)
    k = _dot(mn, wk_ref[...])
    gk = gk_ref[...]
    tm = x_ref.shape[0]
    v = _dot(mn, wv_ref[...])
    for h in range(N_HEADS):
        sl = slice(h * HEAD_W, (h + 1) * HEAD_W)
        k_ref[pl.ds(h, tm, stride=N_HEADS), :] = _rms(k[:, sl], gk)
        v_ref[pl.ds(h, tm, stride=N_HEADS), :] = v[:, sl]


def _memkv(mem2d, g, wk, wv, gk, tm):
    n, d = mem2d.shape
    br = wk.shape[1]
    return pl.pallas_call(
        _memkv_kernel,
        grid=(n // tm,),
        in_specs=[pl.BlockSpec((tm, d), lambda i: (i, 0)),
                  pl.BlockSpec((1, d), lambda i: (0, 0)),
                  pl.BlockSpec((d, br), lambda i: (0, 0)),
                  pl.BlockSpec((d, br), lambda i: (0, 0)),
                  pl.BlockSpec((1, HEAD_W), lambda i: (0, 0))],
        out_specs=[pl.BlockSpec((tm * N_HEADS, HEAD_W), lambda i: (i, 0)),
                   pl.BlockSpec((tm * N_HEADS, HEAD_W), lambda i: (i, 0))],
        out_shape=[jax.ShapeDtypeStruct((n * N_HEADS, HEAD_W), F32),
                   jax.ShapeDtypeStruct((n * N_HEADS, HEAD_W), F32)],
        compiler_params=_cparams(("arbitrary",)),
        name="memkv",
    )(mem2d, g.reshape(1, d), wk, wv, gk.reshape(1, HEAD_W))


def _mem_attn_kernel(q_ref, k_ref, v_ref, gq_ref, y_ref, *, sb, m):
    gq = gq_ref[...]
    scale = HEAD_W ** -0.5
    heads = range(N_HEADS)
    for s in range(sb):
        q = q_ref[s].astype(F32)
        qh = [(_rms(q[:, h * HEAD_W:(h + 1) * HEAD_W], gq) * (scale * LOG2E)).astype(BF16) for h in heads]
        sc = [_dot_nt(qh[h], k_ref[s, pl.ds(h, m, stride=N_HEADS), :].astype(BF16)) for h in heads]
        ps, ls = [], []
        for h in heads:
            p = jnp.exp2(sc[h] - jnp.max(sc[h], axis=-1, keepdims=True))
            ls.append(jnp.sum(p, axis=-1, keepdims=True))
            ps.append(p.astype(BF16))
        for h in heads:
            o = _dot(ps[h], v_ref[s, pl.ds(h, m, stride=N_HEADS), :].astype(BF16)) / ls[h]
            y_ref[s, :, h * HEAD_W:(h + 1) * HEAD_W] = o.astype(BF16)


def _mem_attn(z3d, k4, v4, gq, layer, sb, tq):
    b, t, _ = z3d.shape
    m = k4.shape[2] // N_HEADS
    br = N_HEADS * HEAD_W
    return pl.pallas_call(
        functools.partial(_mem_attn_kernel, sb=sb, m=m),
        grid=(b // sb, t // tq),
        in_specs=[pl.BlockSpec((sb, tq, br), lambda i, j: (i, j, COL_MEMQ // br)),
                  pl.BlockSpec((None, sb, m * N_HEADS, HEAD_W), lambda i, j: (layer, i, 0, 0)),
                  pl.BlockSpec((None, sb, m * N_HEADS, HEAD_W), lambda i, j: (layer, i, 0, 0)),
                  pl.BlockSpec((1, HEAD_W), lambda i, j: (0, 0))],
        out_specs=pl.BlockSpec((sb, tq, br), lambda i, j: (i, j, 0)),
        out_shape=jax.ShapeDtypeStruct((b, t, br), BF16),
        compiler_params=_cparams(("arbitrary", "arbitrary")),
        name="mem_attn",
    )(z3d, k4, v4, gq.reshape(1, HEAD_W))


def _outproj_kernel(x_ref, mg_ref, sg_ref, y0_ref, y1_ref, y2_ref, y3_ref, wb_ref, wo_ref, o_ref):
    d = x_ref.shape[1]
    br = y0_ref.shape[1]
    merged = None
    for n, y_ref in enumerate((y0_ref, y1_ref, y2_ref, y3_ref)):
        sg = sg_ref[:, n * br:(n + 1) * br].astype(F32)
        ys = (y_ref[...].astype(F32) * (sg * (1.0 + jnp.tanh(sg)))).astype(BF16)
        proj = _dot(ys, wb_ref[n])
        term = (1.0 + jnp.tanh(mg_ref[:, n * d:(n + 1) * d].astype(F32))) * proj
        merged = term if merged is None else merged + term
    o_ref[...] = x_ref[...] + _dot(merged.astype(BF16), wo_ref[...])


def _outproj(x2d, z2d, ys, wb, wo, tm):
    n, d = x2d.shape
    br = ys[0].shape[1]
    nb = wb.shape[0]
    yspec = pl.BlockSpec((tm, br), lambda i: (i, 0))
    return pl.pallas_call(
        _outproj_kernel,
        grid=(n // tm,),
        in_specs=[pl.BlockSpec((tm, d), lambda i: (i, 0)),
                  pl.BlockSpec((tm, nb * d), lambda i: (i, COL_MG // (nb * d))),
                  pl.BlockSpec((tm, nb * br), lambda i: (i, COL_SG // (nb * br))),
                  yspec, yspec, yspec, yspec,
                  pl.BlockSpec((nb, br, d), lambda i: (0, 0, 0)),
                  pl.BlockSpec((d, d), lambda i: (0, 0))],
        out_specs=pl.BlockSpec((tm, d), lambda i: (i, 0)),
        out_shape=jax.ShapeDtypeStruct((n, d), F32),
        compiler_params=_cparams(("arbitrary",)),
        name="outproj",
    )(x2d, z2d, z2d, *ys, wb, wo)


def _rope_tables(pos):
    half = MLA_ROPE // 2
    freqs = ROPE_THETA ** (-np.arange(half, dtype=np.float64) / half)
    ang = np.asarray(pos, np.float64)[:, None] * freqs[None, :]
    cos = np.tile(np.cos(ang), (1, 8))
    sin = np.tile(np.sin(ang), (1, 8))
    sin[:, :64] *= -1.0
    return jnp.asarray(cos, F32), jnp.asarray(sin, F32)


def _mla_params(q_norm, w_uq, kv_norm, w_uk, gq, gk):
    half = MLA_ROPE // 2
    wq = w_uq.reshape(MLA_Q_RANK, N_HEADS, MLA_QK)
    wuq = jnp.concatenate([wq[:, :, :MLA_NOPE].reshape(MLA_Q_RANK, -1),
                           wq[:, :, MLA_NOPE:MLA_NOPE + half].reshape(MLA_Q_RANK, -1),
                           wq[:, :, MLA_NOPE + half:].reshape(MLA_Q_RANK, -1)], axis=1).astype(BF16)
    g2 = gq * gk
    wk = w_uk.reshape(MLA_KV_RANK, N_HEADS, MLA_NOPE)
    wabs = jnp.zeros((N_HEADS * MLA_NOPE, N_HEADS * MLA_KV_RANK), F32)
    for h in range(N_HEADS):
        blk = (wk[:, h, :] * g2[None, :MLA_NOPE]).T
        wabs = wabs.at[h * MLA_NOPE:(h + 1) * MLA_NOPE, h * MLA_KV_RANK:(h + 1) * MLA_KV_RANK].set(blk)
    grope = jnp.zeros((1, 128), F32).at[0, :MLA_ROPE].set(g2[MLA_NOPE:])
    j = np.arange(384)
    head_of = np.where(j < 256, j // MLA_NOPE, (j % 64) // half)
    indq = (head_of[:, None] == np.arange(128)[None, :]).astype(np.float32)
    lane = j - 256
    is_rope = (j >= 256) & ((lane < half) | ((lane >= 64) & (lane < 64 + half)))
    indk = np.zeros((8, 384), np.float32)
    for h in range(N_HEADS):
        indk[h] = ((j < 256) & (j // MLA_NOPE == h)) | is_rope
    return dict(gq_norm=q_norm.reshape(1, -1), wuq=wuq, gkv_norm=kv_norm.reshape(1, -1),
                wuk=w_uk.astype(BF16), wukt=w_uk.T.astype(BF16), wabs=wabs.astype(BF16), grope=grope,
                indq=jnp.asarray(indq, BF16), indk=jnp.asarray(indk, BF16))


def _hgrn_lower_bounds(lb_param):
    p = jax.nn.softmax(lb_param.astype(F32), axis=0)
    return jnp.cumsum(p, axis=0) - p[0]


def kernel(x_prompt, x_sample, mem_prompt, cache_mla_latent, cache_mla_rope, page_table, state_hgrn, state_conv, cache_mem_k, cache_mem_v, norm_gain, w_in, conv_w, hgrn_lb, hgrn_norm, mla_q_norm, mla_w_uq, mla_kv_norm, mla_w_uk, mla_w_uv, mla_q_gain, mla_k_gain, mem_norm, mem_w_k, mem_w_v, mem_q_gain, mem_k_gain, w_branch_out, w_out):
    bp, tp, d = x_prompt.shape
    bs, ts, _ = x_sample.shape
    depth = w_in.shape[0]
    br = conv_w.shape[2]
    mem_len = mem_prompt.shape[1]
    n_pages = page_table.shape[1]
    past = n_pages * cache_mla_latent.shape[2]
    tpad = SAMPLE_PAD_T
    n_p = bp * tp
    n_s = bs * tpad

    lbs = _hgrn_lower_bounds(hgrn_lb)
    cos_p, sin_p = _rope_tables(np.arange(tp))
    cos_s, sin_s = _rope_tables(past + np.arange(tpad))
    tm_s = min(n_s, 1024)
    cos_s = jnp.tile(cos_s, (tm_s // tpad, 1))
    sin_s = jnp.tile(sin_s, (tm_s // tpad, 1))
    page_flat = page_table.reshape(-1).astype(jnp.int32)
    rope_t = jnp.swapaxes(cache_mla_rope, 2, 3)
    mem_k4 = cache_mem_k.reshape(depth, bs, mem_len * N_HEADS, HEAD_W)
    mem_v4 = cache_mem_v.reshape(depth, bs, mem_len * N_HEADS, HEAD_W)

    tm_p = min(tp, 512)
    xp = x_prompt.reshape(n_p, d)
    xs = jnp.pad(x_sample, ((0, 0), (0, tpad - ts), (0, 0))).reshape(n_s, d)
    mem2d = mem_prompt.reshape(bp * mem_len, d)
    zero_state = jnp.zeros((1, bp, N_HEADS, HEAD_W, HEAD_W), F32)

    outs = {k: [] for k in ("p_lat", "p_rope", "p_conv", "p_mk", "p_mv", "s_lat", "s_rope", "s_conv")}
    p_hg = s_hg = None
    for l in range(depth):
        w_in_t = _relayout_w_in(w_in[l])
        pw = _mla_params(mla_q_norm[l], mla_w_uq[l], mla_kv_norm[l], mla_w_uk[l], mla_q_gain[l], mla_k_gain[l])
        wuv = mla_w_uv[l].astype(BF16)
        wb = (0.5 * w_branch_out[l]).astype(BF16)
        wo = w_out[l].astype(BF16)

        z = _inproj(xp, norm_gain[l], w_in_t, min(n_p, 1024), 2048)
        z3 = z.reshape(bp, tp, N_COLS)
        y_conv, tail = _conv(z, None, conv_w[l], tp, tm_p, 8)
        y_hg, p_hg = _hgrn(z3, lbs[l], hgrn_norm[l], zero_state, 0, l, 1, tm_p, HG_CHUNK, None, depth, p_hg)
        c, r, kv, rk, qp = _mla_prep(z, cos_p, sin_p, pw, tm_p)
        y_mla = _mla_prompt(qp, kv, rk, wuv, bp, tp, min(tp, MLA_TQ))
        mk, mv = _memkv(mem2d, mem_norm[l], mem_w_k[l].astype(BF16), mem_w_v[l].astype(BF16), mem_k_gain[l],
                        min(bp * mem_len, 512))
        rows_m = mem_len * N_HEADS
        y_mem = _mem_attn(z3, mk.reshape(1, bp, rows_m, HEAD_W), mv.reshape(1, bp, rows_m, HEAD_W), mem_q_gain[l],
                          0, 1, tm_p)
        xp = _outproj(xp, z, (y_conv, y_hg.reshape(n_p, br), y_mla, y_mem.reshape(n_p, br)), wb, wo, tm_p)
        outs["p_lat"].append(c.reshape(bp, tp, MLA_KV_RANK))
        outs["p_rope"].append(r.reshape(bp, tp, MLA_ROPE))
        outs["p_conv"].append(tail.reshape(bp, 8, br)[:, 8 - (CONV_K - 1):])
        outs["p_mk"].append(mk.reshape(bp, mem_len, N_HEADS, HEAD_W))
        outs["p_mv"].append(mv.reshape(bp, mem_len, N_HEADS, HEAD_W))

        z = _inproj(xs, norm_gain[l], w_in_t, tm_s, 2048)
        z3 = z.reshape(bs, tpad, N_COLS)
        hist = jnp.pad(state_conv[l], ((0, 0), (0, tpad - (CONV_K - 1)), (0, 0))).reshape(n_s, br)
        y_conv, u_all = _conv(z, hist, conv_w[l], tpad, tm_s, tm_s)
        sb = 8 if bs % 8 == 0 else 1
        y_hg, s_hg = _hgrn(z3, lbs[l], hgrn_norm[l], state_hgrn, l, l, sb, tpad, tpad, ts, depth, s_hg)
        c, r, kv, rk, qp = _mla_prep(z, cos_s, sin_s, pw, tm_s)
        qp3 = qp.reshape(N_HEADS, bs, tpad, 256).transpose(1, 0, 2, 3).reshape(bs, N_HEADS * tpad, 256)
        rt3 = jnp.pad(r.reshape(bs, tpad, MLA_ROPE).transpose(0, 2, 1), ((0, 0), (0, 0), (0, PAGE - tpad)))
        y_mla = _mla_sample(page_flat, qp3, c.reshape(bs, tpad, MLA_KV_RANK), rt3, pw["wukt"], wuv,
                            cache_mla_latent, rope_t, l, n_pages)
        y_mem = _mem_attn(z3, mem_k4, mem_v4, mem_q_gain[l], l, sb, tpad)
        xs = _outproj(xs, z, (y_conv, y_hg.reshape(n_s, br), y_mla.reshape(n_s, br), y_mem.reshape(n_s, br)),
                      wb, wo, tm_s)
        outs["s_lat"].append(c.reshape(bs, tpad, MLA_KV_RANK)[:, :ts])
        outs["s_rope"].append(r.reshape(bs, tpad, MLA_ROPE)[:, :ts])
        outs["s_conv"].append(u_all.reshape(bs, tpad, br)[:, ts - (CONV_K - 1):ts])

    st = lambda k: jnp.stack(outs[k])
    return (xp.reshape(bp, tp, d), xs.reshape(bs, tpad, d)[:, :ts], st("p_lat"), st("p_rope"), p_hg,
            st("p_conv"), st("p_mk"), st("p_mv"), st("s_lat"), st("s_rope"), s_hg, st("s_conv"))
```

```python
import functools

import numpy as np
import jax
import jax.numpy as jnp
from jax import lax
from jax.experimental import pallas as pl
from jax.experimental.pallas import tpu as pltpu

F32 = jnp.float32
BF16 = jnp.bfloat16

N_HEADS = 4
HEAD_W = 128
MLA_NOPE = 64
MLA_ROPE = 32
MLA_QK = MLA_NOPE + MLA_ROPE
MLA_Q_RANK = 192
MLA_KV_RANK = 128
CONV_K = 3
ROPE_THETA = 10000.0
EPS = 1e-6
NEG = -1e30
PAGE = 128
SAMPLE_PAD_T = 8
HG_CHUNK = 64
HG_BLOCK = 8
HG_TILE = 1024
MLA_ROW_GROUP = 128
MLA_TQ = 512
LOG2E = 1.4426950408889634
VMEM_LIMIT = 56 * 1024 * 1024

COL_CONV = 0
COL_HG = 1536
COL_MLA = 3072
COL_MEMQ = 3584
COL_MG = 4096
COL_SG = 8192
N_COLS = 10240


def _cparams(sem):
    return pltpu.CompilerParams(dimension_semantics=sem, vmem_limit_bytes=VMEM_LIMIT)


def _rms(x, g):
    return x * lax.rsqrt(jnp.mean(x * x, axis=-1, keepdims=True) + EPS) * g


def _sigmoid(x):
    return 0.5 * jnp.tanh(0.5 * x) + 0.5


def _dot(a, b):
    return jnp.dot(a, b, preferred_element_type=F32)


def _dot_nt(a, b):
    return lax.dot_general(a, b, (((1,), (1,)), ((), ())), preferred_element_type=F32)


def _dot_tn(a, b):
    return lax.dot_general(a, b, (((0,), (0,)), ((), ())), preferred_element_type=F32)


def _inproj_kernel(x_ref, g_ref, w_ref, z_ref, hn_ref):
    @pl.when(pl.program_id(1) == 0)
    def _():
        hn_ref[...] = _rms(x_ref[...], g_ref[...]).astype(BF16)

    z_ref[...] = _dot_nt(hn_ref[...], w_ref[...]).astype(BF16)


def _relayout_w_in(w):
    wt = jnp.swapaxes(w, 0, 1)
    d = w.shape[0]
    rows = lambda a, b, scale=None: (wt[a:b] if scale is None else wt[a:b] * scale).astype(BF16)
    zeros = lambda n: jnp.zeros((n, d), BF16)
    o_q, o_kv, o_pe, o_mq, o_sg, o_mg = 3072, 3264, 3392, 3424, 3936, 5984
    half = MLA_ROPE // 2
    return jnp.concatenate([rows(0, o_q),
                            rows(o_q, o_kv), zeros(64),
                            rows(o_kv, o_pe),
                            rows(o_pe, o_pe + half), zeros(48), rows(o_pe + half, o_mq), zeros(48),
                            rows(o_mq, o_sg),
                            rows(o_mg, w.shape[1], 0.5),
                            rows(o_sg, o_mg, 0.5)], axis=0)


def _inproj(x2d, g, w_t, tm, tn):
    n, d = x2d.shape
    ncol = w_t.shape[0]
    return pl.pallas_call(
        _inproj_kernel,
        grid=(n // tm, ncol // tn),
        in_specs=[pl.BlockSpec((tm, d), lambda i, j: (i, 0)),
                  pl.BlockSpec((1, d), lambda i, j: (0, 0)),
                  pl.BlockSpec((tn, d), lambda i, j: (j, 0))],
        out_specs=pl.BlockSpec((tm, tn), lambda i, j: (i, j)),
        out_shape=jax.ShapeDtypeStruct((n, ncol), BF16),
        scratch_shapes=[pltpu.VMEM((tm, d), BF16)],
        compiler_params=_cparams(("arbitrary", "arbitrary")),
        name="inproj",
    )(x2d, g.reshape(1, d), w_t)


def _conv_kernel(*refs, seq_t, tr, tail_rows, has_hist):
    if has_hist:
        z_ref, halo_ref, hist_ref, w_ref, y_ref, tail_ref = refs
    else:
        z_ref, halo_ref, w_ref, y_ref, tail_ref = refs
    i = pl.program_id(0)
    br = y_ref.shape[1]
    z = z_ref[...].astype(F32)
    u = z[:, 2 * br:3 * br] * z[:, 0:br]
    zh = halo_ref[...].astype(F32)
    uh = zh[:, 2 * br:3 * br] * zh[:, 0:br]
    loc = lax.broadcasted_iota(jnp.int32, (tr, 1), 0)
    t = (loc + i * tr) % seq_t
    u1 = jnp.where(loc == 0, uh[7:8], pltpu.roll(u, 1, axis=0))
    u2 = jnp.where(loc == 0, uh[6:7], jnp.where(loc == 1, uh[7:8], pltpu.roll(u, 2, axis=0)))
    if has_hist:
        hp = hist_ref[...]
        u1 = jnp.where(t == 0, pltpu.roll(hp, tr - 1, axis=0), u1)
        u2 = jnp.where(t < 2, hp, u2)
    else:
        u1 = jnp.where(t == 0, 0.0, u1)
        u2 = jnp.where(t < 2, 0.0, u2)
    w = w_ref[...]
    conv = w[0:1] * u2 + w[1:2] * u1 + w[2:3] * u
    y_ref[...] = (z[:, br:2 * br] * conv).astype(BF16)

    @pl.when(((i + 1) * tr) % max(seq_t, tr) == 0)
    def _():
        tail_ref[...] = u[tr - tail_rows:, :]


def _conv(z2d, hist_rows, w, seq_t, tr, tail_rows):
    n = z2d.shape[0]
    br = w.shape[1]
    has_hist = hist_rows is not None
    group = max(seq_t, tr)
    n_tail = (n // group) * tail_rows
    in_specs = [pl.BlockSpec((tr, 3 * br), lambda i: (i, 0)),
                pl.BlockSpec((8, 3 * br), lambda i: (jnp.maximum(i * (tr // 8) - 1, 0), 0))]
    args = [z2d, z2d]
    if has_hist:
        in_specs.append(pl.BlockSpec((tr, br), lambda i: (i, 0)))
        args.append(hist_rows)
    in_specs.append(pl.BlockSpec((CONV_K, br), lambda i: (0, 0)))
    args.append(w)
    return pl.pallas_call(
        functools.partial(_conv_kernel, seq_t=seq_t, tr=tr, tail_rows=tail_rows, has_hist=has_hist),
        grid=(n // tr,),
        in_specs=in_specs,
        out_specs=[pl.BlockSpec((tr, br), lambda i: (i, 0)),
                   pl.BlockSpec((tail_rows, br), lambda i: ((i * tr) // group, 0))],
        out_shape=[jax.ShapeDtypeStruct((n, br), BF16), jax.ShapeDtypeStruct((n_tail, br), F32)],
        compiler_params=_cparams(("arbitrary",)),
        name="conv",
    )(*args)


def _hgrn_intra_diag(q, kk, b2, chunk):
    lane = lax.broadcasted_iota(jnp.int32, (HG_BLOCK, chunk), 1)
    trow = lax.broadcasted_iota(jnp.int32, (HG_BLOCK, chunk), 0)
    a_rows = []
    for r in range(0, chunk, HG_BLOCK):
        b_blk = b2[r:r + HG_BLOCK]
        q_blk = q[r:r + HG_BLOCK]
        acc = jnp.zeros((HG_BLOCK, chunk), F32)
        for s in range(HG_BLOCK):
            p = jnp.exp2(jnp.minimum(b_blk - b2[r + s:r + s + 1], 0.0)) * (q_blk * kk[r + s:r + s + 1])
            acc = jnp.where(lane == r + s, jnp.sum(p, axis=-1, keepdims=True), acc)
        a_rows.append(jnp.where(lane - r <= trow, acc, 0.0))
    return a_rows[0] if len(a_rows) == 1 else jnp.concatenate(a_rows, axis=0)


def _hgrn_inter_block_factors(q, kk, b2, chunk):
    nb = chunk // HG_BLOCK
    zero = jnp.zeros((HG_BLOCK, HEAD_W), F32)
    q_rows, k_rows = [], []
    for i in range(nb):
        blk = slice(i * HG_BLOCK, (i + 1) * HG_BLOCK)
        q_tiles, k_tiles = [], []
        for j in range(nb - 1):
            rho = b2[(j + 1) * HG_BLOCK - 1:(j + 1) * HG_BLOCK]
            q_tiles.append(q[blk] * jnp.exp2(b2[blk] - rho) if j < i else zero)
            k_tiles.append(kk[blk] * jnp.exp2(rho - b2[blk]) if j == i else zero)
        q_rows.append(jnp.concatenate(q_tiles, axis=1))
        k_rows.append(jnp.concatenate(k_tiles, axis=1))
    return jnp.concatenate(q_rows, axis=0).astype(BF16), jnp.concatenate(k_rows, axis=0).astype(BF16)


def _hgrn_tile(load, store, lb, gain, sts, valid_fn, tril_bf, chunk, n_c):
    units = [(s, c) for s in range(len(sts)) for c in range(n_c)]
    chunks = range(len(units))
    q, v, v_t, kk, g_hi, g_lo = [], [], [], [], [], []
    for s, c in units:
        qc, zf, vc = load(s, c)
        f = lb + (1.0 - lb) * _sigmoid(zf)
        g = jnp.log(f)
        k = 1.0 - f
        valid = valid_fn(c)
        if valid is not None:
            g = jnp.where(valid, g, 0.0)
            k = jnp.where(valid, k, 0.0)
        hi = g.astype(BF16)
        q.append(qc)
        v.append(vc.astype(BF16))
        v_t.append(vc.T.astype(BF16))
        kk.append(k)
        g_hi.append(hi)
        g_lo.append((g - hi.astype(F32)).astype(BF16))
    b2 = [(_dot(tril_bf, g_hi[c]) + _dot(tril_bf, g_lo[c])) * LOG2E for c in chunks]
    bl2 = [b[chunk - 1:chunk, :] for b in b2]
    upd = [_dot(v_t[c], (kk[c] * jnp.exp2(bl2[c] - b2[c])).astype(BF16)) for c in chunks]
    a = [_hgrn_intra_diag(q[c], kk[c], b2[c], chunk) for c in chunks]
    if chunk > HG_BLOCK:
        factors = [_hgrn_inter_block_factors(q[c], kk[c], b2[c], chunk) for c in chunks]
        a = [a[c] + _dot_nt(factors[c][0], factors[c][1]) for c in chunks]
    o = [_dot(a[c].astype(BF16), v[c]) for c in chunks]
    sts = list(sts)
    states = []
    for u, (s, _) in enumerate(units):
        states.append(sts[s].astype(BF16))
        sts[s] = sts[s] * jnp.exp2(bl2[u]) + upd[u]
    for u, (s, c) in enumerate(units):
        oc = o[u] + _dot_nt((q[u] * jnp.exp2(b2[u])).astype(BF16), states[u])
        store(s, c, _rms(oc, gain).astype(BF16))
    return sts


def _hgrn_kernel(*refs, sb, tt, chunk, t_valid, fill_slots):
    q_ref, f_ref, i_ref, lb_ref, g_ref, s0_ref = refs[:6]
    y_ref, sout_ref, st_ref = refs[-3:]
    tstep = pl.program_id(2)
    n_t = pl.num_programs(2)

    @pl.when(tstep == 0)
    def _():
        for s in range(sb):
            st_ref[s] = s0_ref[s, 0].T

    lb = lb_ref[0]
    gain = g_ref[0]
    row = lax.broadcasted_iota(jnp.int32, (chunk, chunk), 0)
    col = lax.broadcasted_iota(jnp.int32, (chunk, chunk), 1)
    tril_bf = (row >= col).astype(BF16)
    scale = HEAD_W ** -0.5

    def load(s, c):
        rows = pl.ds(c * chunk, chunk)
        return (q_ref[s, rows, :].astype(F32) * scale, f_ref[s, rows, :].astype(F32),
                i_ref[s, rows, :].astype(F32))

    def store(s, c, y):
        y_ref[s, pl.ds(c * chunk, chunk), :] = y

    def valid_fn(c):
        if t_valid is None:
            return None
        return tstep * tt + c * chunk + lax.broadcasted_iota(jnp.int32, (chunk, 1), 0) < t_valid

    new_states = _hgrn_tile(load, store, lb, gain, [st_ref[s] for s in range(sb)], valid_fn, tril_bf, chunk,
                            tt // chunk)
    for s in range(sb):
        st_ref[s] = new_states[s]

    @pl.when(tstep == n_t - 1)
    def _():
        for s in range(sb):
            final = st_ref[s].T
            if fill_slots is None:
                sout_ref[s, 0] = final
            else:
                for slot in range(fill_slots):
                    sout_ref[slot, s, 0] = final


def _hgrn(z3d, lb, gain, s0, s0_layer, layer, sb, tt, chunk, t_valid, depth, stacked_prev):
    b, t, _ = z3d.shape
    cb = COL_HG // HEAD_W
    first = stacked_prev is None
    assert first == (layer == 0)
    kern = functools.partial(_hgrn_kernel, sb=sb, tt=tt, chunk=chunk, t_valid=t_valid,
                             fill_slots=depth if first else None)
    state_spec = (pl.BlockSpec((depth, sb, 1, HEAD_W, HEAD_W), lambda i, h, k: (0, i, h, 0, 0)) if first else
                  pl.BlockSpec((None, sb, 1, HEAD_W, HEAD_W), lambda i, h, k: (layer, i, h, 0, 0)))
    in_specs = [pl.BlockSpec((sb, tt, HEAD_W), lambda i, h, k: (i, k, cb + h)),
                pl.BlockSpec((sb, tt, HEAD_W), lambda i, h, k: (i, k, cb + N_HEADS + h)),
                pl.BlockSpec((sb, tt, HEAD_W), lambda i, h, k: (i, k, cb + 2 * N_HEADS + h)),
                pl.BlockSpec((1, 1, HEAD_W), lambda i, h, k: (h, 0, 0)),
                pl.BlockSpec((1, 1, HEAD_W), lambda i, h, k: (h, 0, 0)),
                pl.BlockSpec((None, sb, 1, HEAD_W, HEAD_W), lambda i, h, k: (s0_layer, i, h, 0, 0))]
    args = [z3d, z3d, z3d, lb.reshape(N_HEADS, 1, HEAD_W), gain.reshape(N_HEADS, 1, HEAD_W), s0]
    aliases = {}
    if stacked_prev is not None:
        in_specs.append(pl.BlockSpec(memory_space=pl.ANY))
        args.append(stacked_prev)
        aliases = {len(args) - 1: 1}
    return pl.pallas_call(
        kern,
        grid=(b // sb, N_HEADS, t // tt),
        in_specs=in_specs,
        out_specs=[pl.BlockSpec((sb, tt, HEAD_W), lambda i, h, k: (i, k, h)),
                   state_spec],
        out_shape=[jax.ShapeDtypeStruct((b, t, N_HEADS * HEAD_W), BF16),
                   jax.ShapeDtypeStruct((depth, b, N_HEADS, HEAD_W, HEAD_W), F32)],
        scratch_shapes=[pltpu.VMEM((sb, HEAD_W, HEAD_W), F32)],
        input_output_aliases=aliases,
        compiler_params=_cparams(("arbitrary", "arbitrary", "arbitrary")),
        name="hgrn",
    )(*args)


def _mla_prep_kernel(z_ref, cos_ref, sin_ref, gq_ref, wuq_ref, gkv_ref, wuk_ref, wabs_ref, grope_ref,
                     indq_ref, indk_ref, c_ref, r_ref, kv_ref, rk_ref, qp_ref):
    z = z_ref[...].astype(F32)
    ql = z[:, 0:MLA_Q_RANK]
    kvl = z[:, 256:384]
    kpe = z[:, 384:512]
    cos = cos_ref[...]
    sin = sin_ref[...]
    lane = lax.broadcasted_iota(jnp.int32, (1, 128), 1)

    qf = _dot(_rms(ql, gq_ref[...]).astype(BF16), wuq_ref[...])
    q_nope = qf[:, 0:256]
    rq_in = qf[:, 256:384]
    rot_q = rq_in * cos + pltpu.roll(rq_in, 64, axis=1) * sin
    rot_k = kpe * cos + pltpu.roll(kpe, 64, axis=1) * sin

    c = _rms(kvl, gkv_ref[...])
    c_ref[...] = c
    r32 = jnp.where(lane < 16, rot_k, pltpu.roll(rot_k, 80, axis=1))
    r32 = jnp.where(lane < MLA_ROPE, r32, 0.0)
    r_ref[...] = r32[:, 0:MLA_ROPE]
    cb = c.astype(BF16)
    kv_ref[...] = jnp.concatenate([cb, r32.astype(BF16)], axis=1)

    kn = _dot(cb, wuk_ref[...])
    kcat2 = jnp.concatenate([kn * kn, rot_k * rot_k], axis=1).astype(BF16)
    ssk = _dot_nt(indk_ref[...], kcat2)
    rk_ref[...] = lax.rsqrt(ssk * (1.0 / MLA_QK) + EPS)

    qcat2 = jnp.concatenate([q_nope * q_nope, rot_q * rot_q], axis=1).astype(BF16)
    ssq = _dot(qcat2, indq_ref[...])
    rq = lax.rsqrt(ssq * (1.0 / MLA_QK) + EPS) * (MLA_QK ** -0.5 * LOG2E)

    q_abs = _dot(q_nope.astype(BF16), wabs_ref[...])
    grope = grope_ref[...]
    for h in range(N_HEADS):
        a = rot_q if h == 0 else pltpu.roll(rot_q, 128 - 16 * h, axis=1)
        qr = jnp.where(lane < 16, a, pltpu.roll(a, 80, axis=1)) * grope
        rq_h = rq[:, h:h + 1]
        qp_ref[h, :, 0:128] = (q_abs[:, h * 128:(h + 1) * 128] * rq_h).astype(BF16)
        qp_ref[h, :, 128:256] = (qr * rq_h).astype(BF16)


def _mla_prep(z2d, cos_t, sin_t, pw, tm):
    n = z2d.shape[0]
    n_tab = cos_t.shape[0] // tm
    full = lambda shape: pl.BlockSpec(shape, lambda i: (0,) * len(shape))
    return pl.pallas_call(
        _mla_prep_kernel,
        grid=(n // tm,),
        in_specs=[pl.BlockSpec((tm, 512), lambda i: (i, COL_MLA // 512)),
                  pl.BlockSpec((tm, 128), lambda i: (i % n_tab, 0)),
                  pl.BlockSpec((tm, 128), lambda i: (i % n_tab, 0)),
                  full((1, MLA_Q_RANK)), full((MLA_Q_RANK, 384)), full((1, MLA_KV_RANK)),
                  full((MLA_KV_RANK, 256)), full((256, 512)), full((1, 128)),
                  full((384, 128)), full((8, 384))],
        out_specs=[pl.BlockSpec((tm, MLA_KV_RANK), lambda i: (i, 0)),
                   pl.BlockSpec((tm, MLA_ROPE), lambda i: (i, 0)),
                   pl.BlockSpec((tm, 256), lambda i: (i, 0)),
                   pl.BlockSpec((8, tm), lambda i: (0, i)),
                   pl.BlockSpec((N_HEADS, tm, 256), lambda i: (0, i, 0))],
        out_shape=[jax.ShapeDtypeStruct((n, MLA_KV_RANK), F32),
                   jax.ShapeDtypeStruct((n, MLA_ROPE), F32),
                   jax.ShapeDtypeStruct((n, 256), BF16),
                   jax.ShapeDtypeStruct((8, n), F32),
                   jax.ShapeDtypeStruct((N_HEADS, n, 256), BF16)],
        compiler_params=_cparams(("arbitrary",)),
        name="mla_prep",
    )(z2d, cos_t, sin_t, pw["gq_norm"], pw["wuq"], pw["gkv_norm"], pw["wuk"], pw["wabs"], pw["grope"],
      pw["indq"], pw["indk"])


def _mla_prompt_kernel(qp_ref, kv_ref, rk_ref, wuv_ref, y_ref, m_ref, l_ref, acc_ref, *, tq, rb):
    i = pl.program_id(1)
    nrow = N_HEADS * tq
    m_ref[...] = jnp.full((nrow, 128), NEG, F32)
    l_ref[...] = jnp.zeros((nrow, 128), F32)
    acc_ref[...] = jnp.zeros((nrow, MLA_KV_RANK), F32)
    row = lax.broadcasted_iota(jnp.int32, (rb, 128), 0)
    col = lax.broadcasted_iota(jnp.int32, (rb, 128), 1)
    lane_tiles = range(0, tq, 128)

    def block(k0, diagonal):
        kv = kv_ref[pl.ds(k0, tq), :]
        cv = kv[:, 0:MLA_KV_RANK]

        def qk(h):
            return _dot_nt(qp_ref[h], kv)

        def softmax_update(h, s_h):
            p_rows = []
            for r in range(0, tq, rb):
                rows = pl.ds(h * tq + r, rb)
                tiles = []
                for c in lane_tiles:
                    s = s_h[r:r + rb, c:c + 128] * rk_ref[h:h + 1, pl.ds(k0 + c, 128)]
                    if diagonal:
                        s = jnp.where(col + c <= row + r, s, NEG)
                    tiles.append(s)
                m_old = m_ref[rows, :]
                m_new = jnp.maximum(m_old, jnp.max(functools.reduce(jnp.maximum, tiles), axis=-1, keepdims=True))
                alpha = jnp.exp2(m_old - m_new)
                probs = [jnp.exp2(s - m_new) for s in tiles]
                l_ref[rows, :] = alpha * l_ref[rows, :] + jnp.sum(functools.reduce(jnp.add, probs), axis=-1,
                                                                  keepdims=True)
                m_ref[rows, :] = m_new
                acc_ref[rows, :] = alpha * acc_ref[rows, :]
                p_rows.append(jnp.concatenate([p.astype(BF16) for p in probs], axis=1))
            return jnp.concatenate(p_rows, axis=0)

        def pv(h, p_h):
            acc_ref[pl.ds(h * tq, tq), :] += _dot(p_h, cv)

        s0 = qk(0)
        s1 = qk(1)
        p0 = softmax_update(0, s0)
        s2 = qk(2)
        p1 = softmax_update(1, s1)
        pv(0, p0)
        s3 = qk(3)
        p2 = softmax_update(2, s2)
        pv(1, p1)
        p3 = softmax_update(3, s3)
        pv(2, p2)
        pv(3, p3)

    def step(j, carry):
        block(pl.multiple_of(j * tq, tq), False)
        return carry

    lax.fori_loop(0, i, step, 0)
    block(pl.multiple_of(i * tq, tq), True)
    for h in range(N_HEADS):
        rows = pl.ds(h * tq, tq)
        o = (acc_ref[rows, :] / l_ref[rows, :]).astype(BF16)
        y_ref[:, h * HEAD_W:(h + 1) * HEAD_W] = _dot(o, wuv_ref[:, h * HEAD_W:(h + 1) * HEAD_W]).astype(BF16)


def _mla_prompt(qp, kv, rk, wuv, b, t, tq):
    n = b * t
    nq = t // tq
    nrow = N_HEADS * tq
    return pl.pallas_call(
        functools.partial(_mla_prompt_kernel, tq=tq, rb=min(tq, MLA_ROW_GROUP)),
        grid=(b, nq),
        scratch_shapes=[pltpu.VMEM((nrow, 128), F32), pltpu.VMEM((nrow, 128), F32),
                        pltpu.VMEM((nrow, MLA_KV_RANK), F32)],
        in_specs=[pl.BlockSpec((N_HEADS, tq, 256), lambda bi, i: (0, bi * nq + i, 0)),
                  pl.BlockSpec((t, 256), lambda bi, i: (bi, 0)),
                  pl.BlockSpec((8, t), lambda bi, i: (0, bi)),
                  pl.BlockSpec((MLA_KV_RANK, N_HEADS * HEAD_W), lambda bi, i: (0, 0))],
        out_specs=pl.BlockSpec((tq, N_HEADS * HEAD_W), lambda bi, i: (bi * nq + i, 0)),
        out_shape=jax.ShapeDtypeStruct((n, N_HEADS * HEAD_W), BF16),
        compiler_params=_cparams(("arbitrary", "arbitrary")),
        name="mla_prompt",
    )(qp, kv, rk, wuv)


def _mla_sample_kernel(pt_ref, qp_ref, cnew_ref, rnewt_ref, wukt_ref, wuv_ref, lat_hbm, ropet_hbm,
                       y_ref, cbuf, rbuf, cbf, s_scr, part_scr, sem, *, layer, n_pages, ck, tv):
    b = pl.program_id(0)
    n_b = pl.num_programs(0)
    past = n_pages * PAGE
    tk = past + PAGE
    slot = b % 2
    tp = SAMPLE_PAD_T
    nrow = N_HEADS * tv
    npart = N_HEADS * 8

    def lat_copy(pg, p, sl):
        return pltpu.make_async_copy(lat_hbm.at[layer, pg], cbuf.at[sl, pl.ds(p * PAGE, PAGE), :], sem.at[0, sl])

    def rope_copy(pg, p, sl):
        return pltpu.make_async_copy(ropet_hbm.at[layer, pg], rbuf.at[sl, :, pl.ds(p * PAGE, PAGE)], sem.at[1, sl])

    def issue(seq, sl):
        def body(p, carry):
            pg = pt_ref[seq * n_pages + p]
            lat_copy(pg, p, sl).start()
            rope_copy(pg, p, sl).start()
            return carry
        lax.fori_loop(0, n_pages, body, 0, unroll=4)

    @pl.when(b == 0)
    def _():
        for sl in range(2):
            cbuf[sl, pl.ds(past, PAGE), :] = jnp.zeros((PAGE, MLA_KV_RANK), F32)
        issue(0, 0)

    def wait_all(sl):
        def wait_body(p, carry):
            lat_copy(0, p, sl).wait()
            rope_copy(0, p, sl).wait()
            return carry
        lax.fori_loop(0, n_pages, wait_body, 0, unroll=4)

    wait_all(slot)
    nxt = jnp.minimum(b + 1, n_b - 1)
    n_chunks = tk // ck
    pages_per_chunk = -(-n_pages // n_chunks)

    def prefetch(chunk_idx):
        for p in range(chunk_idx * pages_per_chunk, min((chunk_idx + 1) * pages_per_chunk, n_pages)):
            pg = pt_ref[nxt * n_pages + p]
            lat_copy(pg, p, 1 - slot).start()
            rope_copy(pg, p, 1 - slot).start()

    cbuf[slot, pl.ds(past, tp), :] = cnew_ref[0]
    rbuf[slot, :, pl.ds(past, PAGE)] = rnewt_ref[0]

    q = qp_ref[0]
    qr = q[:, MLA_KV_RANK:MLA_KV_RANK + MLA_ROPE]
    w_stack = jnp.concatenate([wukt_ref[...], q[:, 0:MLA_KV_RANK]], axis=0)
    n_up = N_HEADS * MLA_NOPE
    qpos = past + lax.broadcasted_iota(jnp.int32, (nrow, 1), 0) % tv
    head_ones = (lax.broadcasted_iota(jnp.int32, (nrow, npart), 0) // tv
                 == lax.broadcasted_iota(jnp.int32, (nrow, npart), 1) // 8).astype(BF16)

    for k0 in range(0, tk, ck):
        prefetch(k0 // ck)
        cb = cbuf[slot, pl.ds(k0, ck), :].astype(BF16)
        cbf[pl.ds(k0, ck), :] = cb
        rt = rbuf[slot, :, pl.ds(k0, ck)]
        big = _dot_nt(w_stack, cb)
        kn2 = big[0:n_up] * big[0:n_up]
        part_r = jnp.sum((rt * rt).reshape(MLA_ROPE // 8, 8, ck), axis=0)
        parts = [jnp.sum(kn2[h * MLA_NOPE:(h + 1) * MLA_NOPE].reshape(MLA_NOPE // 8, 8, ck), axis=0) + part_r
                 for h in range(N_HEADS)]
        part_scr[:, pl.ds(k0, ck)] = jnp.concatenate(parts, axis=0).astype(BF16)
        s_scr[:, pl.ds(k0, ck)] = big[n_up:n_up + nrow]

    rk = lax.rsqrt(_dot(head_ones, part_scr[...]) * (1.0 / MLA_QK) + EPS)
    s = (s_scr[...] + _dot(qr, rbuf[slot].astype(BF16))) * rk
    kpos = lax.broadcasted_iota(jnp.int32, (1, tk), 1)
    s = jnp.where(kpos <= qpos, s, NEG)
    m = jnp.max(s, axis=-1, keepdims=True)
    p = jnp.exp2(s - m)
    l = jnp.sum(p, axis=-1, keepdims=True)
    o = _dot(p.astype(BF16), cbf[...]) / l
    full = _dot(o.astype(BF16), wuv_ref[...])
    lane_head = lax.broadcasted_iota(jnp.int32, (tp, N_HEADS * HEAD_W), 1) // HEAD_W
    token = lax.broadcasted_iota(jnp.int32, (tp, N_HEADS * HEAD_W), 0)
    y = jnp.zeros((tp, N_HEADS * HEAD_W), F32)
    for h in range(N_HEADS):
        tile = full[(h * tv) // tp * tp:(h * tv) // tp * tp + tp]
        if (h * tv) % tp:
            tile = pltpu.roll(tile, tp - (h * tv) % tp, axis=0)
        y = jnp.where(lane_head == h, tile, y)
    y_ref[0] = jnp.where(token < tv, y, 0.0).astype(BF16)

    @pl.when(b == n_b - 1)
    def _():
        wait_all(1 - slot)


def _mla_sample(page_flat, qp3, c3, rt3, wukt, wuv, lat, ropet, layer, n_pages):
    bs, nrow, _ = qp3.shape
    tp = SAMPLE_PAD_T
    tv = nrow // N_HEADS
    assert tp % tv == 0 and nrow % 8 == 0
    tk = n_pages * PAGE + PAGE
    ck = max(d for d in range(PAGE, 768 + 1, PAGE) if tk % d == 0)
    kern = functools.partial(_mla_sample_kernel, layer=layer, n_pages=n_pages, ck=ck, tv=tv)
    grid_spec = pltpu.PrefetchScalarGridSpec(
        num_scalar_prefetch=1,
        grid=(bs,),
        in_specs=[pl.BlockSpec((1, nrow, 256), lambda b, pt: (b, 0, 0)),
                  pl.BlockSpec((1, tp, MLA_KV_RANK), lambda b, pt: (b, 0, 0)),
                  pl.BlockSpec((1, MLA_ROPE, PAGE), lambda b, pt: (b, 0, 0)),
                  pl.BlockSpec((N_HEADS * MLA_NOPE, MLA_KV_RANK), lambda b, pt: (0, 0)),
                  pl.BlockSpec((MLA_KV_RANK, N_HEADS * HEAD_W), lambda b, pt: (0, 0)),
                  pl.BlockSpec(memory_space=pl.ANY),
                  pl.BlockSpec(memory_space=pl.ANY)],
        out_specs=pl.BlockSpec((1, tp, N_HEADS * HEAD_W), lambda b, pt: (b, 0, 0)),
        scratch_shapes=[pltpu.VMEM((2, tk, MLA_KV_RANK), F32),
                        pltpu.VMEM((2, MLA_ROPE, tk), F32),
                        pltpu.VMEM((tk, MLA_KV_RANK), BF16),
                        pltpu.VMEM((nrow, tk), F32),
                        pltpu.VMEM((N_HEADS * 8, tk), BF16),
                        pltpu.SemaphoreType.DMA((2, 2))],
    )
    return pl.pallas_call(
        kern,
        grid_spec=grid_spec,
        out_shape=jax.ShapeDtypeStruct((bs, tp, N_HEADS * HEAD_W), BF16),
        compiler_params=_cparams(("arbitrary",)),
        name="mla_sample",
    )(page_flat, qp3, c3, rt3, wukt, wuv, lat, ropet)


def _memkv_kernel(x_ref, g_ref, wk_ref, wv_ref, gk_ref, k_ref, v_ref):
    mn = _rms(x_ref[...], g_ref[...]).astype(BF16)
    k = _dot(mn, wk_ref[...])
    gk = gk_ref[...]
    tm = x_ref.shape[0]
    v = _dot(mn, wv_ref[...])
    for h in range(N_HEADS):
        sl = slice(h * HEAD_W, (h + 1) * HEAD_W)
        k_ref[pl.ds(h, tm, stride=N_HEADS), :] = _rms(k[:, sl], gk)
        v_ref[pl.ds(h, tm, stride=N_HEADS), :] = v[:, sl]


def _memkv(mem2d, g, wk, wv, gk, tm):
    n, d = mem2d.shape
    br = wk.shape[1]
    return pl.pallas_call(
        _memkv_kernel,
        grid=(n // tm,),
        in_specs=[pl.BlockSpec((tm, d), lambda i: (i, 0)),
                  pl.BlockSpec((1, d), lambda i: (0, 0)),
                  pl.BlockSpec((d, br), lambda i: (0, 0)),
                  pl.BlockSpec((d, br), lambda i: (0, 0)),
                  pl.BlockSpec((1, HEAD_W), lambda i: (0, 0))],
        out_specs=[pl.BlockSpec((tm * N_HEADS, HEAD_W), lambda i: (i, 0)),
                   pl.BlockSpec((tm * N_HEADS, HEAD_W), lambda i: (i, 0))],
        out_shape=[jax.ShapeDtypeStruct((n * N_HEADS, HEAD_W), F32),
                   jax.ShapeDtypeStruct((n * N_HEADS, HEAD_W), F32)],
        compiler_params=_cparams(("arbitrary",)),
        name="memkv",
    )(mem2d, g.reshape(1, d), wk, wv, gk.reshape(1, HEAD_W))


def _mem_attn_kernel(q_ref, k_ref, v_ref, gq_ref, y_ref, *, sb, m):
    gq = gq_ref[...]
    scale = HEAD_W ** -0.5
    heads = range(N_HEADS)
    for s in range(sb):
        q = q_ref[s].astype(F32)
        qh = [(_rms(q[:, h * HEAD_W:(h + 1) * HEAD_W], gq) * (scale * LOG2E)).astype(BF16) for h in heads]
        sc = [_dot_nt(qh[h], k_ref[s, pl.ds(h, m, stride=N_HEADS), :].astype(BF16)) for h in heads]
        ps, ls = [], []
        for h in heads:
            p = jnp.exp2(sc[h] - jnp.max(sc[h], axis=-1, keepdims=True))
            ls.append(jnp.sum(p, axis=-1, keepdims=True))
            ps.append(p.astype(BF16))
        for h in heads:
            o = _dot(ps[h], v_ref[s, pl.ds(h, m, stride=N_HEADS), :].astype(BF16)) / ls[h]
            y_ref[s, :, h * HEAD_W:(h + 1) * HEAD_W] = o.astype(BF16)


def _mem_attn(z3d, k4, v4, gq, layer, sb, tq):
    b, t, _ = z3d.shape
    m = k4.shape[2] // N_HEADS
    br = N_HEADS * HEAD_W
    return pl.pallas_call(
        functools.partial(_mem_attn_kernel, sb=sb, m=m),
        grid=(b // sb, t // tq),
        in_specs=[pl.BlockSpec((sb, tq, br), lambda i, j: (i, j, COL_MEMQ // br)),
                  pl.BlockSpec((None, sb, m * N_HEADS, HEAD_W), lambda i, j: (layer, i, 0, 0)),
                  pl.BlockSpec((None, sb, m * N_HEADS, HEAD_W), lambda i, j: (layer, i, 0, 0)),
                  pl.BlockSpec((1, HEAD_W), lambda i, j: (0, 0))],
        out_specs=pl.BlockSpec((sb, tq, br), lambda i, j: (i, j, 0)),
        out_shape=jax.ShapeDtypeStruct((b, t, br), BF16),
        compiler_params=_cparams(("arbitrary", "arbitrary")),
        name="mem_attn",
    )(z3d, k4, v4, gq.reshape(1, HEAD_W))


def _outproj_kernel(x_ref, mg_ref, sg_ref, y0_ref, y1_ref, y2_ref, y3_ref, wb_ref, wo_ref, o_ref):
    d = x_ref.shape[1]
    br = y0_ref.shape[1]
    merged = None
    for n, y_ref in enumerate((y0_ref, y1_ref, y2_ref, y3_ref)):
        sg = sg_ref[:, n * br:(n + 1) * br].astype(F32)
        ys = (y_ref[...].astype(F32) * (sg * (1.0 + jnp.tanh(sg)))).astype(BF16)
        proj = _dot(ys, wb_ref[n])
        term = (1.0 + jnp.tanh(mg_ref[:, n * d:(n + 1) * d].astype(F32))) * proj
        merged = term if merged is None else merged + term
    o_ref[...] = x_ref[...] + _dot(merged.astype(BF16), wo_ref[...])


def _outproj(x2d, z2d, ys, wb, wo, tm):
    n, d = x2d.shape
    br = ys[0].shape[1]
    nb = wb.shape[0]
    yspec = pl.BlockSpec((tm, br), lambda i: (i, 0))
    return pl.pallas_call(
        _outproj_kernel,
        grid=(n // tm,),
        in_specs=[pl.BlockSpec((tm, d), lambda i: (i, 0)),
                  pl.BlockSpec((tm, nb * d), lambda i: (i, COL_MG // (nb * d))),
                  pl.BlockSpec((tm, nb * br), lambda i: (i, COL_SG // (nb * br))),
                  yspec, yspec, yspec, yspec,
                  pl.BlockSpec((nb, br, d), lambda i: (0, 0, 0)),
                  pl.BlockSpec((d, d), lambda i: (0, 0))],
        out_specs=pl.BlockSpec((tm, d), lambda i: (i, 0)),
        out_shape=jax.ShapeDtypeStruct((n, d), F32),
        compiler_params=_cparams(("arbitrary",)),
        name="outproj",
    )(x2d, z2d, z2d, *ys, wb, wo)


def _rope_tables(pos):
    half = MLA_ROPE // 2
    freqs = ROPE_THETA ** (-np.arange(half, dtype=np.float64) / half)
    ang = np.asarray(pos, np.float64)[:, None] * freqs[None, :]
    cos = np.tile(np.cos(ang), (1, 8))
    sin = np.tile(np.sin(ang), (1, 8))
    sin[:, :64] *= -1.0
    return jnp.asarray(cos, F32), jnp.asarray(sin, F32)


def _mla_params(q_norm, w_uq, kv_norm, w_uk, gq, gk):
    half = MLA_ROPE // 2
    wq = w_uq.reshape(MLA_Q_RANK, N_HEADS, MLA_QK)
    wuq = jnp.concatenate([wq[:, :, :MLA_NOPE].reshape(MLA_Q_RANK, -1),
                           wq[:, :, MLA_NOPE:MLA_NOPE + half].reshape(MLA_Q_RANK, -1),
                           wq[:, :, MLA_NOPE + half:].reshape(MLA_Q_RANK, -1)], axis=1).astype(BF16)
    g2 = gq * gk
    wk = w_uk.reshape(MLA_KV_RANK, N_HEADS, MLA_NOPE)
    wabs = jnp.zeros((N_HEADS * MLA_NOPE, N_HEADS * MLA_KV_RANK), F32)
    for h in range(N_HEADS):
        blk = (wk[:, h, :] * g2[None, :MLA_NOPE]).T
        wabs = wabs.at[h * MLA_NOPE:(h + 1) * MLA_NOPE, h * MLA_KV_RANK:(h + 1) * MLA_KV_RANK].set(blk)
    grope = jnp.zeros((1, 128), F32).at[0, :MLA_ROPE].set(g2[MLA_NOPE:])
    j = np.arange(384)
    head_of = np.where(j < 256, j // MLA_NOPE, (j % 64) // half)
    indq = (head_of[:, None] == np.arange(128)[None, :]).astype(np.float32)
    lane = j - 256
    is_rope = (j >= 256) & ((lane < half) | ((lane >= 64) & (lane < 64 + half)))
    indk = np.zeros((8, 384), np.float32)
    for h in range(N_HEADS):
        indk[h] = ((j < 256) & (j // MLA_NOPE == h)) | is_rope
    return dict(gq_norm=q_norm.reshape(1, -1), wuq=wuq, gkv_norm=kv_norm.reshape(1, -1),
                wuk=w_uk.astype(BF16), wukt=w_uk.T.astype(BF16), wabs=wabs.astype(BF16), grope=grope,
                indq=jnp.asarray(indq, BF16), indk=jnp.asarray(indk, BF16))


def _hgrn_lower_bounds(lb_param):
    p = jax.nn.softmax(lb_param.astype(F32), axis=0)
    return jnp.cumsum(p, axis=0) - p[0]


def kernel(x_prompt, x_sample, mem_prompt, cache_mla_latent, cache_mla_rope, page_table, state_hgrn, state_conv, cache_mem_k, cache_mem_v, norm_gain, w_in, conv_w, hgrn_lb, hgrn_norm, mla_q_norm, mla_w_uq, mla_kv_norm, mla_w_uk, mla_w_uv, mla_q_gain, mla_k_gain, mem_norm, mem_w_k, mem_w_v, mem_q_gain, mem_k_gain, w_branch_out, w_out):
    bp, tp, d = x_prompt.shape
    bs, ts, _ = x_sample.shape
    depth = w_in.shape[0]
    br = conv_w.shape[2]
    mem_len = mem_prompt.shape[1]
    n_pages = page_table.shape[1]
    past = n_pages * cache_mla_latent.shape[2]
    tpad = SAMPLE_PAD_T
    n_p = bp * tp
    n_s = bs * tpad

    lbs = _hgrn_lower_bounds(hgrn_lb)
    cos_p, sin_p = _rope_tables(np.arange(tp))
    cos_s, sin_s = _rope_tables(past + np.arange(tpad))
    tm_s = min(n_s, 1024)
    cos_s = jnp.tile(cos_s, (tm_s // tpad, 1))
    sin_s = jnp.tile(sin_s, (tm_s // tpad, 1))
    page_flat = page_table.reshape(-1).astype(jnp.int32)
    rope_t = jnp.swapaxes(cache_mla_rope, 2, 3)
    mem_k4 = cache_mem_k.reshape(depth, bs, mem_len * N_HEADS, HEAD_W)
    mem_v4 = cache_mem_v.reshape(depth, bs, mem_len * N_HEADS, HEAD_W)

    tm_p = min(tp, 512)
    xp = x_prompt.reshape(n_p, d)
    xs = jnp.pad(x_sample, ((0, 0), (0, tpad - ts), (0, 0))).reshape(n_s, d)
    mem2d = mem_prompt.reshape(bp * mem_len, d)
    zero_state = jnp.zeros((1, bp, N_HEADS, HEAD_W, HEAD_W), F32)

    outs = {k: [] for k in ("p_lat", "p_rope", "p_conv", "p_mk", "p_mv", "s_lat", "s_rope", "s_conv")}
    p_hg = s_hg = None
    for l in range(depth):
        w_in_t = _relayout_w_in(w_in[l])
        pw = _mla_params(mla_q_norm[l], mla_w_uq[l], mla_kv_norm[l], mla_w_uk[l], mla_q_gain[l], mla_k_gain[l])
        wuv = mla_w_uv[l].astype(BF16)
        wb = (0.5 * w_branch_out[l]).astype(BF16)
        wo = w_out[l].astype(BF16)

        z = _inproj(xp, norm_gain[l], w_in_t, min(n_p, 1024), 2048)
        z3 = z.reshape(bp, tp, N_COLS)
        y_conv, tail = _conv(z, None, conv_w[l], tp, tm_p, 8)
        y_hg, p_hg = _hgrn(z3, lbs[l], hgrn_norm[l], zero_state, 0, l, 1, min(tp, HG_TILE), HG_CHUNK, None, depth,
                           p_hg)
        c, r, kv, rk, qp = _mla_prep(z, cos_p, sin_p, pw, tm_p)
        y_mla = _mla_prompt(qp, kv, rk, wuv, bp, tp, min(tp, MLA_TQ))
        mk, mv = _memkv(mem2d, mem_norm[l], mem_w_k[l].astype(BF16), mem_w_v[l].astype(BF16), mem_k_gain[l],
                        min(bp * mem_len, 512))
        rows_m = mem_len * N_HEADS
        y_mem = _mem_attn(z3, mk.reshape(1, bp, rows_m, HEAD_W), mv.reshape(1, bp, rows_m, HEAD_W), mem_q_gain[l],
                          0, 1, tm_p)
        xp = _outproj(xp, z, (y_conv, y_hg.reshape(n_p, br), y_mla, y_mem.reshape(n_p, br)), wb, wo, tm_p)
        outs["p_lat"].append(c.reshape(bp, tp, MLA_KV_RANK))
        outs["p_rope"].append(r.reshape(bp, tp, MLA_ROPE))
        outs["p_conv"].append(tail.reshape(bp, 8, br)[:, 8 - (CONV_K - 1):])
        outs["p_mk"].append(mk.reshape(bp, mem_len, N_HEADS, HEAD_W))
        outs["p_mv"].append(mv.reshape(bp, mem_len, N_HEADS, HEAD_W))

        z = _inproj(xs, norm_gain[l], w_in_t, tm_s, 2048)
        z3 = z.reshape(bs, tpad, N_COLS)
        hist = jnp.pad(state_conv[l], ((0, 0), (0, tpad - (CONV_K - 1)), (0, 0))).reshape(n_s, br)
        y_conv, u_all = _conv(z, hist, conv_w[l], tpad, tm_s, tm_s)
        sb = 8 if bs % 8 == 0 else 1
        y_hg, s_hg = _hgrn(z3, lbs[l], hgrn_norm[l], state_hgrn, l, l, sb, tpad, tpad, ts, depth, s_hg)
        c, r, kv, rk, qp = _mla_prep(z, cos_s, sin_s, pw, tm_s)
        qp3 = qp.reshape(N_HEADS, bs, tpad, 256)[:, :, :ts].transpose(1, 0, 2, 3).reshape(bs, N_HEADS * ts, 256)
        rt3 = jnp.pad(r.reshape(bs, tpad, MLA_ROPE).transpose(0, 2, 1), ((0, 0), (0, 0), (0, PAGE - tpad)))
        y_mla = _mla_sample(page_flat, qp3, c.reshape(bs, tpad, MLA_KV_RANK), rt3, pw["wukt"], wuv,
                            cache_mla_latent, rope_t, l, n_pages)
        y_mem = _mem_attn(z3, mem_k4, mem_v4, mem_q_gain[l], l, sb, tpad)
        xs = _outproj(xs, z, (y_conv, y_hg.reshape(n_s, br), y_mla.reshape(n_s, br), y_mem.reshape(n_s, br)),
                      wb, wo, tm_s)
        outs["s_lat"].append(c.reshape(bs, tpad, MLA_KV_RANK)[:, :ts])
        outs["s_rope"].append(r.reshape(bs, tpad, MLA_ROPE)[:, :ts])
        outs["s_conv"].append(u_all.reshape(bs, tpad, br)[:, ts - (CONV_K - 1):ts])

    st = lambda k: jnp.stack(outs[k])
    return (xp.reshape(bp, tp, d), xs.reshape(bs, tpad, d)[:, :ts], st("p_lat"), st("p_rope"), p_hg,
            st("p_conv"), st("p_mk"), st("p_mv"), st("s_lat"), st("s_rope"), s_hg, st("s_conv"))
```

```python
import functools

import numpy as np
import jax
import jax.numpy as jnp
from jax import lax
from jax.experimental import pallas as pl
from jax.experimental.pallas import tpu as pltpu

F32 = jnp.float32
BF16 = jnp.bfloat16

N_HEADS = 4
HEAD_W = 128
MLA_NOPE = 64
MLA_ROPE = 32
MLA_QK = MLA_NOPE + MLA_ROPE
MLA_Q_RANK = 192
MLA_KV_RANK = 128
CONV_K = 3
ROPE_THETA = 10000.0
EPS = 1e-6
NEG = -1e30
PAGE = 128
SAMPLE_PAD_T = 8
HG_CHUNK = 64
HG_BLOCK = 8
HG_TILE = 2048
MLA_ROW_GROUP = 128
MLA_TQ = 512
MLA_KEY_CHUNK = 1664
LOG2E = 1.4426950408889634
VMEM_LIMIT = 56 * 1024 * 1024

COL_CONV = 0
COL_HG = 1536
COL_MLA = 3072
COL_MEMQ = 3584
COL_MG = 4096
COL_SG = 8192
N_COLS = 10240


def _cparams(sem):
    return pltpu.CompilerParams(dimension_semantics=sem, vmem_limit_bytes=VMEM_LIMIT)


def _rms(x, g):
    return x * lax.rsqrt(jnp.mean(x * x, axis=-1, keepdims=True) + EPS) * g


def _dot(a, b):
    return jnp.dot(a, b, preferred_element_type=F32)


def _dot_nt(a, b):
    return lax.dot_general(a, b, (((1,), (1,)), ((), ())), preferred_element_type=F32)


def _dot_tn(a, b):
    return lax.dot_general(a, b, (((0,), (0,)), ((), ())), preferred_element_type=F32)


def _inproj_kernel(x_ref, g_ref, w_ref, z_ref, hn_ref):
    @pl.when(pl.program_id(1) == 0)
    def _():
        hn_ref[...] = _rms(x_ref[...], g_ref[...]).astype(BF16)

    z_ref[...] = _dot_nt(hn_ref[...], w_ref[...]).astype(BF16)


def _relayout_w_in(w):
    wt = jnp.swapaxes(w, 0, 1)
    d = w.shape[0]
    rows = lambda a, b, scale=None: (wt[a:b] if scale is None else wt[a:b] * scale).astype(BF16)
    zeros = lambda n: jnp.zeros((n, d), BF16)
    o_q, o_kv, o_pe, o_mq, o_sg, o_mg = 3072, 3264, 3392, 3424, 3936, 5984
    half = MLA_ROPE // 2
    return jnp.concatenate([rows(0, o_q),
                            rows(o_q, o_kv), zeros(64),
                            rows(o_kv, o_pe),
                            rows(o_pe, o_pe + half), zeros(48), rows(o_pe + half, o_mq), zeros(48),
                            rows(o_mq, o_sg),
                            rows(o_mg, w.shape[1], 0.5),
                            rows(o_sg, o_mg, 0.5)], axis=0)


def _inproj(x2d, g, w_t, tm, tn):
    n, d = x2d.shape
    ncol = w_t.shape[0]
    return pl.pallas_call(
        _inproj_kernel,
        grid=(n // tm, ncol // tn),
        in_specs=[pl.BlockSpec((tm, d), lambda i, j: (i, 0)),
                  pl.BlockSpec((1, d), lambda i, j: (0, 0)),
                  pl.BlockSpec((tn, d), lambda i, j: (j, 0))],
        out_specs=pl.BlockSpec((tm, tn), lambda i, j: (i, j)),
        out_shape=jax.ShapeDtypeStruct((n, ncol), BF16),
        scratch_shapes=[pltpu.VMEM((tm, d), BF16)],
        compiler_params=_cparams(("arbitrary", "arbitrary")),
        name="inproj",
    )(x2d, g.reshape(1, d), w_t)


def _conv_kernel(*refs, seq_t, tr, tail_rows, has_hist):
    if has_hist:
        z_ref, halo_ref, hist_ref, w_ref, y_ref, tail_ref = refs
    else:
        z_ref, halo_ref, w_ref, y_ref, tail_ref = refs
    i = pl.program_id(0)
    br = y_ref.shape[1]
    z = z_ref[...].astype(F32)
    u = z[:, 2 * br:3 * br] * z[:, 0:br]
    zh = halo_ref[...].astype(F32)
    uh = zh[:, 2 * br:3 * br] * zh[:, 0:br]
    loc = lax.broadcasted_iota(jnp.int32, (tr, 1), 0)
    t = (loc + i * tr) % seq_t
    u1 = jnp.where(loc == 0, uh[7:8], pltpu.roll(u, 1, axis=0))
    u2 = jnp.where(loc == 0, uh[6:7], jnp.where(loc == 1, uh[7:8], pltpu.roll(u, 2, axis=0)))
    if has_hist:
        hp = hist_ref[...]
        u1 = jnp.where(t == 0, pltpu.roll(hp, tr - 1, axis=0), u1)
        u2 = jnp.where(t < 2, hp, u2)
    else:
        u1 = jnp.where(t == 0, 0.0, u1)
        u2 = jnp.where(t < 2, 0.0, u2)
    w = w_ref[...]
    conv = w[0:1] * u2 + w[1:2] * u1 + w[2:3] * u
    y_ref[...] = (z[:, br:2 * br] * conv).astype(BF16)

    @pl.when(((i + 1) * tr) % max(seq_t, tr) == 0)
    def _():
        tail_ref[...] = u[tr - tail_rows:, :]


def _conv(z2d, hist_rows, w, seq_t, tr, tail_rows):
    n = z2d.shape[0]
    br = w.shape[1]
    has_hist = hist_rows is not None
    group = max(seq_t, tr)
    n_tail = (n // group) * tail_rows
    in_specs = [pl.BlockSpec((tr, 3 * br), lambda i: (i, 0)),
                pl.BlockSpec((8, 3 * br), lambda i: (jnp.maximum(i * (tr // 8) - 1, 0), 0))]
    args = [z2d, z2d]
    if has_hist:
        in_specs.append(pl.BlockSpec((tr, br), lambda i: (i, 0)))
        args.append(hist_rows)
    in_specs.append(pl.BlockSpec((CONV_K, br), lambda i: (0, 0)))
    args.append(w)
    return pl.pallas_call(
        functools.partial(_conv_kernel, seq_t=seq_t, tr=tr, tail_rows=tail_rows, has_hist=has_hist),
        grid=(n // tr,),
        in_specs=in_specs,
        out_specs=[pl.BlockSpec((tr, br), lambda i: (i, 0)),
                   pl.BlockSpec((tail_rows, br), lambda i: ((i * tr) // group, 0))],
        out_shape=[jax.ShapeDtypeStruct((n, br), BF16), jax.ShapeDtypeStruct((n_tail, br), F32)],
        compiler_params=_cparams(("arbitrary",)),
        name="conv",
    )(*args)


def _hgrn_intra_diag(q, bk, b2, chunk):
    lane = lax.broadcasted_iota(jnp.int32, (HG_BLOCK, chunk), 1)
    trow = lax.broadcasted_iota(jnp.int32, (HG_BLOCK, chunk), 0)
    a_rows = []
    for r in range(0, chunk, HG_BLOCK):
        b_blk = b2[r:r + HG_BLOCK]
        q_blk = q[r:r + HG_BLOCK]
        acc = jnp.zeros((HG_BLOCK, chunk), F32)
        for s in range(HG_BLOCK):
            p = jnp.exp2(b_blk - bk[r + s:r + s + 1]) * q_blk
            acc = jnp.where(lane == r + s, jnp.sum(p, axis=-1, keepdims=True), acc)
        a_rows.append(jnp.where(lane - r <= trow, acc, 0.0))
    return a_rows[0] if len(a_rows) == 1 else jnp.concatenate(a_rows, axis=0)


def _hgrn_inter_block_factors(q, bk, b2, chunk):
    nb = chunk // HG_BLOCK
    zero = jnp.zeros((HG_BLOCK, HEAD_W), F32)
    q_rows, k_rows = [], []
    for i in range(nb):
        blk = slice(i * HG_BLOCK, (i + 1) * HG_BLOCK)
        q_tiles, k_tiles = [], []
        for j in range(nb - 1):
            rho = b2[(j + 1) * HG_BLOCK - 1:(j + 1) * HG_BLOCK]
            q_tiles.append(q[blk] * jnp.exp2(b2[blk] - rho) if j < i else zero)
            k_tiles.append(jnp.exp2(rho - bk[blk]) if j == i else zero)
        q_rows.append(jnp.concatenate(q_tiles, axis=1))
        k_rows.append(jnp.concatenate(k_tiles, axis=1))
    return jnp.concatenate(q_rows, axis=0).astype(BF16), jnp.concatenate(k_rows, axis=0).astype(BF16)


def _hgrn_tile(load, store, lb, gain, sts, valid_fn, tril_bf, chunk, n_c):
    units = [(s, c) for s in range(len(sts)) for c in range(n_c)]
    chunks = range(len(units))
    q, v, v_t, lk, g_hi, g_lo = [], [], [], [], [], []
    for s, c in units:
        qc, zf, vc = load(s, c)
        f = lb + (1.0 - lb) * jax.nn.sigmoid(zf)
        g = jnp.log(f)
        k = 1.0 - f
        valid = valid_fn(c)
        if valid is not None:
            g = jnp.where(valid, g, 0.0)
            k = jnp.where(valid, k, 0.0)
        hi = g.astype(BF16)
        q.append(qc)
        v.append(vc.astype(BF16))
        v_t.append(vc.T.astype(BF16))
        lk.append(jnp.log2(k))
        g_hi.append(hi)
        g_lo.append((g - hi.astype(F32)).astype(BF16))
    b2 = [(_dot(tril_bf, g_hi[c]) + _dot(tril_bf, g_lo[c])) * LOG2E for c in chunks]
    bk = [b2[c] - lk[c] for c in chunks]
    bl2 = [b[chunk - 1:chunk, :] for b in b2]
    upd = [_dot(v_t[c], jnp.exp2(bl2[c] - bk[c]).astype(BF16)) for c in chunks]
    a = [_hgrn_intra_diag(q[c], bk[c], b2[c], chunk) for c in chunks]
    if chunk > HG_BLOCK:
        factors = [_hgrn_inter_block_factors(q[c], bk[c], b2[c], chunk) for c in chunks]
        a = [a[c] + _dot_nt(factors[c][0], factors[c][1]) for c in chunks]
    o = [_dot(a[c].astype(BF16), v[c]) for c in chunks]
    sts = list(sts)
    states = []
    for u, (s, _) in enumerate(units):
        states.append(sts[s].astype(BF16))
        sts[s] = sts[s] * jnp.exp2(bl2[u]) + upd[u]
    for u, (s, c) in enumerate(units):
        oc = o[u] + _dot_nt((q[u] * jnp.exp2(b2[u])).astype(BF16), states[u])
        store(s, c, _rms(oc, gain).astype(BF16))
    return sts


def _hgrn_kernel(*refs, sb, tt, chunk, t_valid, fill_slots):
    q_ref, f_ref, i_ref, lb_ref, g_ref, s0_ref = refs[:6]
    y_ref, sout_ref, st_ref = refs[-3:]
    tstep = pl.program_id(2)
    n_t = pl.num_programs(2)

    @pl.when(tstep == 0)
    def _():
        for s in range(sb):
            st_ref[s] = s0_ref[s, 0].T

    lb = lb_ref[0]
    gain = g_ref[0]
    row = lax.broadcasted_iota(jnp.int32, (chunk, chunk), 0)
    col = lax.broadcasted_iota(jnp.int32, (chunk, chunk), 1)
    tril_bf = (row >= col).astype(BF16)
    scale = HEAD_W ** -0.5

    def load(s, c):
        rows = pl.ds(c * chunk, chunk)
        return (q_ref[s, rows, :].astype(F32) * scale, f_ref[s, rows, :].astype(F32),
                i_ref[s, rows, :].astype(F32))

    def store(s, c, y):
        y_ref[s, pl.ds(c * chunk, chunk), :] = y

    def valid_fn(c):
        if t_valid is None:
            return None
        return tstep * tt + c * chunk + lax.broadcasted_iota(jnp.int32, (chunk, 1), 0) < t_valid

    new_states = _hgrn_tile(load, store, lb, gain, [st_ref[s] for s in range(sb)], valid_fn, tril_bf, chunk,
                            tt // chunk)
    for s in range(sb):
        st_ref[s] = new_states[s]

    @pl.when(tstep == n_t - 1)
    def _():
        for s in range(sb):
            final = st_ref[s].T
            if fill_slots is None:
                sout_ref[s, 0] = final
            else:
                for slot in range(fill_slots):
                    sout_ref[slot, s, 0] = final


def _hgrn(z3d, lb, gain, s0, s0_layer, layer, sb, tt, chunk, t_valid, depth, stacked_prev):
    b, t, _ = z3d.shape
    cb = COL_HG // HEAD_W
    first = stacked_prev is None
    assert first == (layer == 0)
    kern = functools.partial(_hgrn_kernel, sb=sb, tt=tt, chunk=chunk, t_valid=t_valid,
                             fill_slots=depth if first else None)
    state_spec = (pl.BlockSpec((depth, sb, 1, HEAD_W, HEAD_W), lambda i, h, k: (0, i, h, 0, 0)) if first else
                  pl.BlockSpec((None, sb, 1, HEAD_W, HEAD_W), lambda i, h, k: (layer, i, h, 0, 0)))
    in_specs = [pl.BlockSpec((sb, tt, HEAD_W), lambda i, h, k: (i, k, cb + h)),
                pl.BlockSpec((sb, tt, HEAD_W), lambda i, h, k: (i, k, cb + N_HEADS + h)),
                pl.BlockSpec((sb, tt, HEAD_W), lambda i, h, k: (i, k, cb + 2 * N_HEADS + h)),
                pl.BlockSpec((1, 1, HEAD_W), lambda i, h, k: (h, 0, 0)),
                pl.BlockSpec((1, 1, HEAD_W), lambda i, h, k: (h, 0, 0)),
                pl.BlockSpec((None, sb, 1, HEAD_W, HEAD_W), lambda i, h, k: (s0_layer, i, h, 0, 0))]
    args = [z3d, z3d, z3d, lb.reshape(N_HEADS, 1, HEAD_W), gain.reshape(N_HEADS, 1, HEAD_W), s0]
    aliases = {}
    if stacked_prev is not None:
        in_specs.append(pl.BlockSpec(memory_space=pl.ANY))
        args.append(stacked_prev)
        aliases = {len(args) - 1: 1}
    return pl.pallas_call(
        kern,
        grid=(b // sb, N_HEADS, t // tt),
        in_specs=in_specs,
        out_specs=[pl.BlockSpec((sb, tt, HEAD_W), lambda i, h, k: (i, k, h)),
                   state_spec],
        out_shape=[jax.ShapeDtypeStruct((b, t, N_HEADS * HEAD_W), BF16),
                   jax.ShapeDtypeStruct((depth, b, N_HEADS, HEAD_W, HEAD_W), F32)],
        scratch_shapes=[pltpu.VMEM((sb, HEAD_W, HEAD_W), F32)],
        input_output_aliases=aliases,
        compiler_params=_cparams(("arbitrary", "arbitrary", "arbitrary")),
        name="hgrn",
    )(*args)


def _mla_prep_kernel(z_ref, cos_ref, sin_ref, gq_ref, wuq_ref, gkv_ref, wuk_ref, wabs_ref, grope_ref,
                     indq_ref, indk_ref, c_ref, r_ref, kv_ref, rk_ref, qp_ref):
    z = z_ref[...].astype(F32)
    ql = z[:, 0:MLA_Q_RANK]
    kvl = z[:, 256:384]
    kpe = z[:, 384:512]
    cos = cos_ref[...]
    sin = sin_ref[...]
    lane = lax.broadcasted_iota(jnp.int32, (1, 128), 1)

    qf = _dot(_rms(ql, gq_ref[...]).astype(BF16), wuq_ref[...])
    q_nope = qf[:, 0:256]
    rq_in = qf[:, 256:384]
    rot_q = rq_in * cos + pltpu.roll(rq_in, 64, axis=1) * sin
    rot_k = kpe * cos + pltpu.roll(kpe, 64, axis=1) * sin

    c = _rms(kvl, gkv_ref[...])
    c_ref[...] = c
    r32 = jnp.where(lane < 16, rot_k, pltpu.roll(rot_k, 80, axis=1))
    r32 = jnp.where(lane < MLA_ROPE, r32, 0.0)
    r_ref[...] = r32[:, 0:MLA_ROPE]
    cb = c.astype(BF16)
    kv_ref[...] = jnp.concatenate([cb, r32.astype(BF16)], axis=1)

    kn = _dot(cb, wuk_ref[...])
    kcat2 = jnp.concatenate([kn * kn, rot_k * rot_k], axis=1).astype(BF16)
    ssk = _dot_nt(indk_ref[...], kcat2)
    rk_ref[...] = lax.rsqrt(ssk * (1.0 / MLA_QK) + EPS)

    qcat2 = jnp.concatenate([q_nope * q_nope, rot_q * rot_q], axis=1).astype(BF16)
    ssq = _dot(qcat2, indq_ref[...])
    rq = lax.rsqrt(ssq * (1.0 / MLA_QK) + EPS) * (MLA_QK ** -0.5 * LOG2E)

    q_abs = _dot(q_nope.astype(BF16), wabs_ref[...])
    grope = grope_ref[...]
    half = MLA_ROPE // 2
    rot_q_hi = pltpu.roll(rot_q, 80, axis=1)
    for h in range(N_HEADS):
        lo = half * h
        pair = jnp.where((lane >= lo) & (lane < lo + half), rot_q,
                         jnp.where((lane >= lo + half) & (lane < lo + 2 * half), rot_q_hi, 0.0))
        qr = (pair if h == 0 else pltpu.roll(pair, 128 - lo, axis=1)) * grope
        rq_h = rq[:, h * 128:(h + 1) * 128]
        qp_ref[h, :, 0:128] = (q_abs[:, h * 128:(h + 1) * 128] * rq_h).astype(BF16)
        qp_ref[h, :, 128:256] = (qr * rq_h).astype(BF16)


def _mla_prep(z2d, cos_t, sin_t, pw, tm):
    n = z2d.shape[0]
    n_tab = cos_t.shape[0] // tm
    full = lambda shape: pl.BlockSpec(shape, lambda i: (0,) * len(shape))
    return pl.pallas_call(
        _mla_prep_kernel,
        grid=(n // tm,),
        in_specs=[pl.BlockSpec((tm, 512), lambda i: (i, COL_MLA // 512)),
                  pl.BlockSpec((tm, 128), lambda i: (i % n_tab, 0)),
                  pl.BlockSpec((tm, 128), lambda i: (i % n_tab, 0)),
                  full((1, MLA_Q_RANK)), full((MLA_Q_RANK, 384)), full((1, MLA_KV_RANK)),
                  full((MLA_KV_RANK, 256)), full((256, 512)), full((1, 128)),
                  full((384, N_HEADS * 128)), full((8, 384))],
        out_specs=[pl.BlockSpec((tm, MLA_KV_RANK), lambda i: (i, 0)),
                   pl.BlockSpec((tm, MLA_ROPE), lambda i: (i, 0)),
                   pl.BlockSpec((tm, 256), lambda i: (i, 0)),
                   pl.BlockSpec((8, tm), lambda i: (0, i)),
                   pl.BlockSpec((N_HEADS, tm, 256), lambda i: (0, i, 0))],
        out_shape=[jax.ShapeDtypeStruct((n, MLA_KV_RANK), F32),
                   jax.ShapeDtypeStruct((n, MLA_ROPE), F32),
                   jax.ShapeDtypeStruct((n, 256), BF16),
                   jax.ShapeDtypeStruct((8, n), F32),
                   jax.ShapeDtypeStruct((N_HEADS, n, 256), BF16)],
        compiler_params=_cparams(("arbitrary",)),
        name="mla_prep",
    )(z2d, cos_t, sin_t, pw["gq_norm"], pw["wuq"], pw["gkv_norm"], pw["wuk"], pw["wabs"], pw["grope"],
      pw["indq"], pw["indk"])


def _mla_prompt_kernel(qp_ref, kv_ref, rk_ref, wuv_ref, y_ref, m_ref, l_ref, acc_ref, *, tq, rb):
    i = pl.program_id(1)
    nrow = N_HEADS * tq
    m_ref[...] = jnp.full((nrow, 128), NEG, F32)
    l_ref[...] = jnp.zeros((nrow, 128), F32)
    acc_ref[...] = jnp.zeros((nrow, MLA_KV_RANK), F32)
    row = lax.broadcasted_iota(jnp.int32, (rb, 128), 0)
    col = lax.broadcasted_iota(jnp.int32, (rb, 128), 1)
    def block(k0, diagonal):
        kv = kv_ref[pl.ds(k0, tq), :]
        cv = kv[:, 0:MLA_KV_RANK]
        units = [(h, 0, tq, tq) for h in range(N_HEADS)]

        def qk(unit):
            h, r0, nr, nk = unit
            return _dot_nt(qp_ref[h, pl.ds(r0, nr), :], kv[0:nk])

        def softmax_update(unit, s_u):
            h, r0, nr, nk = unit
            p_rows = []
            for r in range(0, nr, rb):
                rows = pl.ds(h * tq + r0 + r, rb)
                tiles = []
                for c in range(0, nk, 128):
                    s = s_u[r:r + rb, c:c + 128] * rk_ref[h:h + 1, pl.ds(k0 + c, 128)]
                    if diagonal:
                        s = jnp.where(col + c <= row + (r0 + r), s, NEG)
                    tiles.append(s)
                m_old = m_ref[rows, :]
                m_new = jnp.maximum(m_old, jnp.max(functools.reduce(jnp.maximum, tiles), axis=-1, keepdims=True))
                alpha = jnp.exp2(m_old - m_new)
                probs = [jnp.exp2(s - m_new) for s in tiles]
                l_ref[rows, :] = alpha * l_ref[rows, :] + jnp.sum(functools.reduce(jnp.add, probs), axis=-1,
                                                                  keepdims=True)
                m_ref[rows, :] = m_new
                acc_ref[rows, :] = alpha * acc_ref[rows, :]
                p_rows.append(jnp.concatenate([p.astype(BF16) for p in probs], axis=1))
            return jnp.concatenate(p_rows, axis=0)

        def pv(unit, p_u):
            h, r0, nr, nk = unit
            acc_ref[pl.ds(h * tq + r0, nr), :] += _dot(p_u, cv[0:nk])

        n_u = len(units)
        scores = {0: qk(units[0])}
        probs = {}
        for u in range(n_u):
            if u + 1 < n_u:
                scores[u + 1] = qk(units[u + 1])
            probs[u] = softmax_update(units[u], scores.pop(u))
            if u >= 1:
                pv(units[u - 1], probs.pop(u - 1))
        pv(units[n_u - 1], probs.pop(n_u - 1))

    def step(j, carry):
        block(pl.multiple_of(j * tq, tq), False)
        return carry

    lax.fori_loop(0, i, step, 0)
    block(pl.multiple_of(i * tq, tq), True)
    for h in range(N_HEADS):
        rows = pl.ds(h * tq, tq)
        o = (acc_ref[rows, :] / l_ref[rows, :]).astype(BF16)
        y_ref[:, h * HEAD_W:(h + 1) * HEAD_W] = _dot(o, wuv_ref[:, h * HEAD_W:(h + 1) * HEAD_W]).astype(BF16)


def _mla_prompt(qp, kv, rk, wuv, b, t, tq):
    n = b * t
    nq = t // tq
    nrow = N_HEADS * tq
    return pl.pallas_call(
        functools.partial(_mla_prompt_kernel, tq=tq, rb=min(tq, MLA_ROW_GROUP)),
        grid=(b, nq),
        scratch_shapes=[pltpu.VMEM((nrow, 128), F32), pltpu.VMEM((nrow, 128), F32),
                        pltpu.VMEM((nrow, MLA_KV_RANK), F32)],
        in_specs=[pl.BlockSpec((N_HEADS, tq, 256), lambda bi, i: (0, bi * nq + i, 0)),
                  pl.BlockSpec((t, 256), lambda bi, i: (bi, 0)),
                  pl.BlockSpec((8, t), lambda bi, i: (0, bi)),
                  pl.BlockSpec((MLA_KV_RANK, N_HEADS * HEAD_W), lambda bi, i: (0, 0))],
        out_specs=pl.BlockSpec((tq, N_HEADS * HEAD_W), lambda bi, i: (bi * nq + i, 0)),
        out_shape=jax.ShapeDtypeStruct((n, N_HEADS * HEAD_W), BF16),
        compiler_params=_cparams(("arbitrary", "arbitrary")),
        name="mla_prompt",
    )(qp, kv, rk, wuv)


def _mla_sample_kernel(pt_ref, qp_ref, cnew_ref, rnewt_ref, wukt_ref, wuv_ref, lat_hbm, ropet_hbm,
                       y_ref, cbuf, rbuf, cbf, s_scr, part_scr, sem, *, layer, n_pages, ck, tv):
    b = pl.program_id(0)
    n_b = pl.num_programs(0)
    past = n_pages * PAGE
    tk = past + PAGE
    slot = b % 2
    tp = SAMPLE_PAD_T
    nrow = N_HEADS * tv
    npart = N_HEADS * 8

    def lat_copy(pg, p, sl):
        return pltpu.make_async_copy(lat_hbm.at[layer, pg], cbuf.at[sl, pl.ds(p * PAGE, PAGE), :], sem.at[0, sl])

    def rope_copy(pg, p, sl):
        return pltpu.make_async_copy(ropet_hbm.at[layer, pg], rbuf.at[sl, :, pl.ds(p * PAGE, PAGE)], sem.at[1, sl])

    def issue(seq, sl):
        def body(p, carry):
            pg = pt_ref[seq * n_pages + p]
            lat_copy(pg, p, sl).start()
            rope_copy(pg, p, sl).start()
            return carry
        lax.fori_loop(0, n_pages, body, 0, unroll=4)

    @pl.when(b == 0)
    def _():
        for sl in range(2):
            cbuf[sl, pl.ds(past, PAGE), :] = jnp.zeros((PAGE, MLA_KV_RANK), F32)
        issue(0, 0)

    def wait_all(sl):
        def wait_body(p, carry):
            lat_copy(0, p, sl).wait()
            rope_copy(0, p, sl).wait()
            return carry
        lax.fori_loop(0, n_pages, wait_body, 0, unroll=4)

    wait_all(slot)
    nxt = jnp.minimum(b + 1, n_b - 1)
    n_chunks = tk // ck
    pages_per_chunk = -(-n_pages // n_chunks)

    def prefetch(chunk_idx):
        for p in range(chunk_idx * pages_per_chunk, min((chunk_idx + 1) * pages_per_chunk, n_pages)):
            pg = pt_ref[nxt * n_pages + p]
            lat_copy(pg, p, 1 - slot).start()
            rope_copy(pg, p, 1 - slot).start()

    cbuf[slot, pl.ds(past, tp), :] = cnew_ref[0]
    rbuf[slot, :, pl.ds(past, PAGE)] = rnewt_ref[0]

    q = qp_ref[0]
    qr = q[:, MLA_KV_RANK:MLA_KV_RANK + MLA_ROPE]
    w_stack = jnp.concatenate([wukt_ref[...], q[:, 0:MLA_KV_RANK]], axis=0)
    n_up = N_HEADS * MLA_NOPE
    qpos = past + lax.broadcasted_iota(jnp.int32, (nrow, 1), 0) % tv
    head_ones = (lax.broadcasted_iota(jnp.int32, (nrow, npart), 0) // tv
                 == lax.broadcasted_iota(jnp.int32, (nrow, npart), 1) // 8).astype(BF16)

    for k0 in range(0, tk, ck):
        prefetch(k0 // ck)
        cb = cbuf[slot, pl.ds(k0, ck), :].astype(BF16)
        cbf[pl.ds(k0, ck), :] = cb
        rt = rbuf[slot, :, pl.ds(k0, ck)]
        big = _dot_nt(w_stack, cb)
        kn2 = big[0:n_up] * big[0:n_up]
        part_r = jnp.sum((rt * rt).reshape(MLA_ROPE // 8, 8, ck), axis=0)
        parts = [jnp.sum(kn2[h * MLA_NOPE:(h + 1) * MLA_NOPE].reshape(MLA_NOPE // 8, 8, ck), axis=0) + part_r
                 for h in range(N_HEADS)]
        part_scr[:, pl.ds(k0, ck)] = jnp.concatenate(parts, axis=0).astype(BF16)
        s_scr[:, pl.ds(k0, ck)] = big[n_up:n_up + nrow]

    rk = lax.rsqrt(_dot(head_ones, part_scr[...]) * (1.0 / MLA_QK) + EPS)
    s = (s_scr[...] + _dot(qr, rbuf[slot].astype(BF16))) * rk
    kpos = lax.broadcasted_iota(jnp.int32, (1, tk), 1)
    s = jnp.where(kpos <= qpos, s, NEG)
    m = jnp.max(s, axis=-1, keepdims=True)
    p = jnp.exp2(s - m)
    l = jnp.sum(p, axis=-1, keepdims=True)
    o = _dot(p.astype(BF16), cbf[...]) / l
    full = _dot(o.astype(BF16), wuv_ref[...])
    lane_head = lax.broadcasted_iota(jnp.int32, (tp, N_HEADS * HEAD_W), 1) // HEAD_W
    token = lax.broadcasted_iota(jnp.int32, (tp, N_HEADS * HEAD_W), 0)
    y = jnp.zeros((tp, N_HEADS * HEAD_W), F32)
    for h in range(N_HEADS):
        tile = full[(h * tv) // tp * tp:(h * tv) // tp * tp + tp]
        if (h * tv) % tp:
            tile = pltpu.roll(tile, tp - (h * tv) % tp, axis=0)
        y = jnp.where(lane_head == h, tile, y)
    y_ref[0] = jnp.where(token < tv, y, 0.0).astype(BF16)

    @pl.when(b == n_b - 1)
    def _():
        wait_all(1 - slot)


def _mla_sample(page_flat, qp3, c3, rt3, wukt, wuv, lat, ropet, layer, n_pages):
    bs, nrow, _ = qp3.shape
    tp = SAMPLE_PAD_T
    tv = nrow // N_HEADS
    assert tp % tv == 0 and nrow % 8 == 0
    tk = n_pages * PAGE + PAGE
    ck = max(d for d in range(PAGE, MLA_KEY_CHUNK + 1, PAGE) if tk % d == 0)
    kern = functools.partial(_mla_sample_kernel, layer=layer, n_pages=n_pages, ck=ck, tv=tv)
    grid_spec = pltpu.PrefetchScalarGridSpec(
        num_scalar_prefetch=1,
        grid=(bs,),
        in_specs=[pl.BlockSpec((1, nrow, 256), lambda b, pt: (b, 0, 0)),
                  pl.BlockSpec((1, tp, MLA_KV_RANK), lambda b, pt: (b, 0, 0)),
                  pl.BlockSpec((1, MLA_ROPE, PAGE), lambda b, pt: (b, 0, 0)),
                  pl.BlockSpec((N_HEADS * MLA_NOPE, MLA_KV_RANK), lambda b, pt: (0, 0)),
                  pl.BlockSpec((MLA_KV_RANK, N_HEADS * HEAD_W), lambda b, pt: (0, 0)),
                  pl.BlockSpec(memory_space=pl.ANY),
                  pl.BlockSpec(memory_space=pl.ANY)],
        out_specs=pl.BlockSpec((1, tp, N_HEADS * HEAD_W), lambda b, pt: (b, 0, 0)),
        scratch_shapes=[pltpu.VMEM((2, tk, MLA_KV_RANK), F32),
                        pltpu.VMEM((2, MLA_ROPE, tk), F32),
                        pltpu.VMEM((tk, MLA_KV_RANK), BF16),
                        pltpu.VMEM((nrow, tk), F32),
                        pltpu.VMEM((N_HEADS * 8, tk), BF16),
                        pltpu.SemaphoreType.DMA((2, 2))],
    )
    return pl.pallas_call(
        kern,
        grid_spec=grid_spec,
        out_shape=jax.ShapeDtypeStruct((bs, tp, N_HEADS * HEAD_W), BF16),
        compiler_params=_cparams(("arbitrary",)),
        name="mla_sample",
    )(page_flat, qp3, c3, rt3, wukt, wuv, lat, ropet)


def _memkv_kernel(x_ref, g_ref, wk_ref, wv_ref, gk_ref, k_ref, v_ref):
    mn = _rms(x_ref[...], g_ref[...]).astype(BF16)
    k = _dot(mn, wk_ref[...])
    gk = gk_ref[...]
    tm = x_ref.shape[0]
    v = _dot(mn, wv_ref[...])
    for h in range(N_HEADS):
        sl = slice(h * HEAD_W, (h + 1) * HEAD_W)
        k_ref[pl.ds(h, tm, stride=N_HEADS), :] = _rms(k[:, sl], gk)
        v_ref[pl.ds(h, tm, stride=N_HEADS), :] = v[:, sl]


def _memkv(mem2d, g, wk, wv, gk, tm):
    n, d = mem2d.shape
    br = wk.shape[1]
    return pl.pallas_call(
        _memkv_kernel,
        grid=(n // tm,),
        in_specs=[pl.BlockSpec((tm, d), lambda i: (i, 0)),
                  pl.BlockSpec((1, d), lambda i: (0, 0)),
                  pl.BlockSpec((d, br), lambda i: (0, 0)),
                  pl.BlockSpec((d, br), lambda i: (0, 0)),
                  pl.BlockSpec((1, HEAD_W), lambda i: (0, 0))],
        out_specs=[pl.BlockSpec((tm * N_HEADS, HEAD_W), lambda i: (i, 0)),
                   pl.BlockSpec((tm * N_HEADS, HEAD_W), lambda i: (i, 0))],
        out_shape=[jax.ShapeDtypeStruct((n * N_HEADS, HEAD_W), F32),
                   jax.ShapeDtypeStruct((n * N_HEADS, HEAD_W), F32)],
        compiler_params=_cparams(("arbitrary",)),
        name="memkv",
    )(mem2d, g.reshape(1, d), wk, wv, gk.reshape(1, HEAD_W))


def _mem_attn_kernel(q_ref, k_ref, v_ref, gq_ref, y_ref, *, sb, m):
    gq = gq_ref[...]
    scale = HEAD_W ** -0.5
    heads = range(N_HEADS)
    for s in range(sb):
        q = q_ref[s].astype(F32)
        qh = [(_rms(q[:, h * HEAD_W:(h + 1) * HEAD_W], gq) * (scale * LOG2E)).astype(BF16) for h in heads]
        sc = [_dot_nt(qh[h], k_ref[s, pl.ds(h, m, stride=N_HEADS), :].astype(BF16)) for h in heads]
        ps, ls = [], []
        for h in heads:
            p = jnp.exp2(sc[h] - jnp.max(sc[h], axis=-1, keepdims=True))
            ls.append(jnp.sum(p, axis=-1, keepdims=True))
            ps.append(p.astype(BF16))
        for h in heads:
            o = _dot(ps[h], v_ref[s, pl.ds(h, m, stride=N_HEADS), :].astype(BF16)) / ls[h]
            y_ref[s, :, h * HEAD_W:(h + 1) * HEAD_W] = o.astype(BF16)


def _mem_attn(z3d, k4, v4, gq, layer, sb, tq):
    b, t, _ = z3d.shape
    m = k4.shape[2] // N_HEADS
    br = N_HEADS * HEAD_W
    return pl.pallas_call(
        functools.partial(_mem_attn_kernel, sb=sb, m=m),
        grid=(b // sb, t // tq),
        in_specs=[pl.BlockSpec((sb, tq, br), lambda i, j: (i, j, COL_MEMQ // br)),
                  pl.BlockSpec((None, sb, m * N_HEADS, HEAD_W), lambda i, j: (layer, i, 0, 0)),
                  pl.BlockSpec((None, sb, m * N_HEADS, HEAD_W), lambda i, j: (layer, i, 0, 0)),
                  pl.BlockSpec((1, HEAD_W), lambda i, j: (0, 0))],
        out_specs=pl.BlockSpec((sb, tq, br), lambda i, j: (i, j, 0)),
        out_shape=jax.ShapeDtypeStruct((b, t, br), BF16),
        compiler_params=_cparams(("arbitrary", "arbitrary")),
        name="mem_attn",
    )(z3d, k4, v4, gq.reshape(1, HEAD_W))


def _outproj_kernel(x_ref, mg_ref, sg_ref, y0_ref, y1_ref, y2_ref, y3_ref, wb_ref, wo_ref, o_ref):
    d = x_ref.shape[1]
    br = y0_ref.shape[1]
    merged = None
    for n, y_ref in enumerate((y0_ref, y1_ref, y2_ref, y3_ref)):
        sg = sg_ref[:, n * br:(n + 1) * br].astype(F32)
        ys = (y_ref[...].astype(F32) * (sg * (1.0 + jnp.tanh(sg)))).astype(BF16)
        proj = _dot(ys, wb_ref[n])
        term = (1.0 + jnp.tanh(mg_ref[:, n * d:(n + 1) * d].astype(F32))) * proj
        merged = term if merged is None else merged + term
    o_ref[...] = x_ref[...] + _dot(merged.astype(BF16), wo_ref[...])


def _outproj(x2d, z2d, ys, wb, wo, tm):
    n, d = x2d.shape
    br = ys[0].shape[1]
    nb = wb.shape[0]
    yspec = pl.BlockSpec((tm, br), lambda i: (i, 0))
    return pl.pallas_call(
        _outproj_kernel,
        grid=(n // tm,),
        in_specs=[pl.BlockSpec((tm, d), lambda i: (i, 0)),
                  pl.BlockSpec((tm, nb * d), lambda i: (i, COL_MG // (nb * d))),
                  pl.BlockSpec((tm, nb * br), lambda i: (i, COL_SG // (nb * br))),
                  yspec, yspec, yspec, yspec,
                  pl.BlockSpec((nb, br, d), lambda i: (0, 0, 0)),
                  pl.BlockSpec((d, d), lambda i: (0, 0))],
        out_specs=pl.BlockSpec((tm, d), lambda i: (i, 0)),
        out_shape=jax.ShapeDtypeStruct((n, d), F32),
        compiler_params=_cparams(("arbitrary",)),
        name="outproj",
    )(x2d, z2d, z2d, *ys, wb, wo)


def _rope_tables(pos):
    half = MLA_ROPE // 2
    freqs = ROPE_THETA ** (-np.arange(half, dtype=np.float64) / half)
    ang = np.asarray(pos, np.float64)[:, None] * freqs[None, :]
    cos = np.tile(np.cos(ang), (1, 8))
    sin = np.tile(np.sin(ang), (1, 8))
    sin[:, :64] *= -1.0
    return jnp.asarray(cos, F32), jnp.asarray(sin, F32)


def _mla_params(q_norm, w_uq, kv_norm, w_uk, gq, gk):
    half = MLA_ROPE // 2
    wq = w_uq.reshape(MLA_Q_RANK, N_HEADS, MLA_QK)
    wuq = jnp.concatenate([wq[:, :, :MLA_NOPE].reshape(MLA_Q_RANK, -1),
                           wq[:, :, MLA_NOPE:MLA_NOPE + half].reshape(MLA_Q_RANK, -1),
                           wq[:, :, MLA_NOPE + half:].reshape(MLA_Q_RANK, -1)], axis=1).astype(BF16)
    g2 = gq * gk
    wk = w_uk.reshape(MLA_KV_RANK, N_HEADS, MLA_NOPE)
    wabs = jnp.zeros((N_HEADS * MLA_NOPE, N_HEADS * MLA_KV_RANK), F32)
    for h in range(N_HEADS):
        blk = (wk[:, h, :] * g2[None, :MLA_NOPE]).T
        wabs = wabs.at[h * MLA_NOPE:(h + 1) * MLA_NOPE, h * MLA_KV_RANK:(h + 1) * MLA_KV_RANK].set(blk)
    grope = jnp.zeros((1, 128), F32).at[0, :MLA_ROPE].set(g2[MLA_NOPE:])
    j = np.arange(384)
    head_of = np.where(j < 256, j // MLA_NOPE, (j % 64) // half)
    indq = (head_of[:, None] == (np.arange(N_HEADS * 128) // 128)[None, :]).astype(np.float32)
    lane = j - 256
    is_rope = (j >= 256) & ((lane < half) | ((lane >= 64) & (lane < 64 + half)))
    indk = np.zeros((8, 384), np.float32)
    for h in range(N_HEADS):
        indk[h] = ((j < 256) & (j // MLA_NOPE == h)) | is_rope
    return dict(gq_norm=q_norm.reshape(1, -1), wuq=wuq, gkv_norm=kv_norm.reshape(1, -1),
                wuk=w_uk.astype(BF16), wukt=w_uk.T.astype(BF16), wabs=wabs.astype(BF16), grope=grope,
                indq=jnp.asarray(indq, BF16), indk=jnp.asarray(indk, BF16))


def _hgrn_lower_bounds(lb_param):
    p = jax.nn.softmax(lb_param.astype(F32), axis=0)
    return jnp.cumsum(p, axis=0) - p[0]


def kernel(x_prompt, x_sample, mem_prompt, cache_mla_latent, cache_mla_rope, page_table, state_hgrn, state_conv, cache_mem_k, cache_mem_v, norm_gain, w_in, conv_w, hgrn_lb, hgrn_norm, mla_q_norm, mla_w_uq, mla_kv_norm, mla_w_uk, mla_w_uv, mla_q_gain, mla_k_gain, mem_norm, mem_w_k, mem_w_v, mem_q_gain, mem_k_gain, w_branch_out, w_out):
    bp, tp, d = x_prompt.shape
    bs, ts, _ = x_sample.shape
    depth = w_in.shape[0]
    br = conv_w.shape[2]
    mem_len = mem_prompt.shape[1]
    n_pages = page_table.shape[1]
    past = n_pages * cache_mla_latent.shape[2]
    tpad = SAMPLE_PAD_T
    n_p = bp * tp
    n_s = bs * tpad

    lbs = _hgrn_lower_bounds(hgrn_lb)
    cos_p, sin_p = _rope_tables(np.arange(tp))
    cos_s, sin_s = _rope_tables(past + np.arange(tpad))
    tm_s = min(n_s, 1024)
    cos_s = jnp.tile(cos_s, (tm_s // tpad, 1))
    sin_s = jnp.tile(sin_s, (tm_s // tpad, 1))
    page_flat = page_table.reshape(-1).astype(jnp.int32)
    rope_t = jnp.swapaxes(cache_mla_rope, 2, 3)
    mem_k4 = cache_mem_k.reshape(depth, bs, mem_len * N_HEADS, HEAD_W)
    mem_v4 = cache_mem_v.reshape(depth, bs, mem_len * N_HEADS, HEAD_W)

    tm_p = min(tp, 512)
    xp = x_prompt.reshape(n_p, d)
    xs = jnp.pad(x_sample, ((0, 0), (0, tpad - ts), (0, 0))).reshape(n_s, d)
    mem2d = mem_prompt.reshape(bp * mem_len, d)
    zero_state = jnp.zeros((1, bp, N_HEADS, HEAD_W, HEAD_W), F32)

    outs = {k: [] for k in ("p_lat", "p_rope", "p_conv", "p_mk", "p_mv", "s_lat", "s_rope", "s_conv")}
    p_hg = s_hg = None
    for l in range(depth):
        w_in_t = _relayout_w_in(w_in[l])
        pw = _mla_params(mla_q_norm[l], mla_w_uq[l], mla_kv_norm[l], mla_w_uk[l], mla_q_gain[l], mla_k_gain[l])
        wuv = mla_w_uv[l].astype(BF16)
        wb = (0.5 * w_branch_out[l]).astype(BF16)
        wo = w_out[l].astype(BF16)

        z = _inproj(xp, norm_gain[l], w_in_t, min(n_p, 1024), 2048)
        z3 = z.reshape(bp, tp, N_COLS)
        y_conv, tail = _conv(z, None, conv_w[l], tp, tm_p, 8)
        y_hg, p_hg = _hgrn(z3, lbs[l], hgrn_norm[l], zero_state, 0, l, 1, min(tp, HG_TILE), HG_CHUNK, None, depth,
                           p_hg)
        c, r, kv, rk, qp = _mla_prep(z, cos_p, sin_p, pw, tm_p)
        y_mla = _mla_prompt(qp, kv, rk, wuv, bp, tp, min(tp, MLA_TQ))
        mk, mv = _memkv(mem2d, mem_norm[l], mem_w_k[l].astype(BF16), mem_w_v[l].astype(BF16), mem_k_gain[l],
                        min(bp * mem_len, 512))
        rows_m = mem_len * N_HEADS
        y_mem = _mem_attn(z3, mk.reshape(1, bp, rows_m, HEAD_W), mv.reshape(1, bp, rows_m, HEAD_W), mem_q_gain[l],
                          0, 1, tm_p)
        xp = _outproj(xp, z, (y_conv, y_hg.reshape(n_p, br), y_mla, y_mem.reshape(n_p, br)), wb, wo, tm_p)
        outs["p_lat"].append(c.reshape(bp, tp, MLA_KV_RANK))
        outs["p_rope"].append(r.reshape(bp, tp, MLA_ROPE))
        outs["p_conv"].append(tail.reshape(bp, 8, br)[:, 8 - (CONV_K - 1):])
        outs["p_mk"].append(mk.reshape(bp, mem_len, N_HEADS, HEAD_W))
        outs["p_mv"].append(mv.reshape(bp, mem_len, N_HEADS, HEAD_W))

        z = _inproj(xs, norm_gain[l], w_in_t, tm_s, 2048)
        z3 = z.reshape(bs, tpad, N_COLS)
        hist = jnp.pad(state_conv[l], ((0, 0), (0, tpad - (CONV_K - 1)), (0, 0))).reshape(n_s, br)
        y_conv, u_all = _conv(z, hist, conv_w[l], tpad, tm_s, tm_s)
        sb = 8 if bs % 8 == 0 else 1
        y_hg, s_hg = _hgrn(z3, lbs[l], hgrn_norm[l], state_hgrn, l, l, sb, tpad, tpad, ts, depth, s_hg)
        c, r, kv, rk, qp = _mla_prep(z, cos_s, sin_s, pw, tm_s)
        qp3 = qp.reshape(N_HEADS, bs, tpad, 256)[:, :, :ts].transpose(1, 0, 2, 3).reshape(bs, N_HEADS * ts, 256)
        rt3 = jnp.pad(r.reshape(bs, tpad, MLA_ROPE).transpose(0, 2, 1), ((0, 0), (0, 0), (0, PAGE - tpad)))
        y_mla = _mla_sample(page_flat, qp3, c.reshape(bs, tpad, MLA_KV_RANK), rt3, pw["wukt"], wuv,
                            cache_mla_latent, rope_t, l, n_pages)
        y_mem = _mem_attn(z3, mem_k4, mem_v4, mem_q_gain[l], l, sb, tpad)
        xs = _outproj(xs, z, (y_conv, y_hg.reshape(n_s, br), y_mla.reshape(n_s, br), y_mem.reshape(n_s, br)),
                      wb, wo, tm_s)
        outs["s_lat"].append(c.reshape(bs, tpad, MLA_KV_RANK)[:, :ts])
        outs["s_rope"].append(r.reshape(bs, tpad, MLA_ROPE)[:, :ts])
        outs["s_conv"].append(u_all.reshape(bs, tpad, br)[:, ts - (CONV_K - 1):ts])

    st = lambda k: jnp.stack(outs[k])
    return (xp.reshape(bp, tp, d), xs.reshape(bs, tpad, d)[:, :ts], st("p_lat"), st("p_rope"), p_hg,
            st("p_conv"), st("p_mk"), st("p_mv"), st("s_lat"), st("s_rope"), s_hg, st("s_conv"))
```

```python
import functools

import numpy as np
import jax
import jax.numpy as jnp
from jax import lax
from jax.experimental import pallas as pl
from jax.experimental.pallas import tpu as pltpu

F32 = jnp.float32
BF16 = jnp.bfloat16

N_HEADS = 4
HEAD_W = 128
MLA_NOPE = 64
MLA_ROPE = 32
MLA_QK = MLA_NOPE + MLA_ROPE
MLA_Q_RANK = 192
MLA_KV_RANK = 128
CONV_K = 3
ROPE_THETA = 10000.0
EPS = 1e-6
NEG = -1e30
PAGE = 128
SAMPLE_PAD_T = 8
HG_CHUNK = 64
HG_BLOCK = 8
HG_TILE = 2048
MLA_ROW_GROUP = 128
MLA_TQ = 512
MLA_KEY_CHUNK = 1664
LOG2E = 1.4426950408889634
VMEM_LIMIT = 56 * 1024 * 1024

COL_CONV = 0
COL_HG = 1536
COL_MLA = 3072
COL_MEMQ = 3584
COL_MG = 4096
COL_SG = 8192
N_COLS = 10240


def _cparams(sem):
    return pltpu.CompilerParams(dimension_semantics=sem, vmem_limit_bytes=VMEM_LIMIT)


def _rms(x, g):
    return x * lax.rsqrt(jnp.mean(x * x, axis=-1, keepdims=True) + EPS) * g


def _dot(a, b):
    return jnp.dot(a, b, preferred_element_type=F32)


def _dot_nt(a, b):
    return lax.dot_general(a, b, (((1,), (1,)), ((), ())), preferred_element_type=F32)


def _dot_tn(a, b):
    return lax.dot_general(a, b, (((0,), (0,)), ((), ())), preferred_element_type=F32)


def _inproj_kernel(x_ref, g_ref, w_ref, z_ref, hn_ref):
    @pl.when(pl.program_id(1) == 0)
    def _():
        hn_ref[...] = _rms(x_ref[...], g_ref[...]).astype(BF16)

    z_ref[...] = _dot_nt(hn_ref[...], w_ref[...]).astype(BF16)


def _relayout_w_in(w):
    wt = jnp.swapaxes(w, 0, 1)
    d = w.shape[0]
    rows = lambda a, b, scale=None: (wt[a:b] if scale is None else wt[a:b] * scale).astype(BF16)
    zeros = lambda n: jnp.zeros((n, d), BF16)
    o_q, o_kv, o_pe, o_mq, o_sg, o_mg = 3072, 3264, 3392, 3424, 3936, 5984
    half = MLA_ROPE // 2
    return jnp.concatenate([rows(0, o_q),
                            rows(o_q, o_kv), zeros(64),
                            rows(o_kv, o_pe),
                            rows(o_pe, o_pe + half), zeros(48), rows(o_pe + half, o_mq), zeros(48),
                            rows(o_mq, o_sg),
                            rows(o_mg, w.shape[1], 0.5),
                            rows(o_sg, o_mg, 0.5)], axis=0)


def _inproj(x2d, g, w_t, tm, tn):
    n, d = x2d.shape
    ncol = w_t.shape[0]
    return pl.pallas_call(
        _inproj_kernel,
        grid=(n // tm, ncol // tn),
        in_specs=[pl.BlockSpec((tm, d), lambda i, j: (i, 0)),
                  pl.BlockSpec((1, d), lambda i, j: (0, 0)),
                  pl.BlockSpec((tn, d), lambda i, j: (j, 0))],
        out_specs=pl.BlockSpec((tm, tn), lambda i, j: (i, j)),
        out_shape=jax.ShapeDtypeStruct((n, ncol), BF16),
        scratch_shapes=[pltpu.VMEM((tm, d), BF16)],
        compiler_params=_cparams(("arbitrary", "arbitrary")),
        name="inproj",
    )(x2d, g.reshape(1, d), w_t)


def _conv_kernel(*refs, seq_t, tr, tail_rows, has_hist):
    if has_hist:
        z_ref, halo_ref, hist_ref, w_ref, y_ref, tail_ref = refs
    else:
        z_ref, halo_ref, w_ref, y_ref, tail_ref = refs
    i = pl.program_id(0)
    br = y_ref.shape[1]
    z = z_ref[...].astype(F32)
    u = z[:, 2 * br:3 * br] * z[:, 0:br]
    zh = halo_ref[...].astype(F32)
    uh = zh[:, 2 * br:3 * br] * zh[:, 0:br]
    u1 = pltpu.roll(u, 1, axis=0)
    u2 = pltpu.roll(u, 2, axis=0)
    if has_hist:
        loc = lax.broadcasted_iota(jnp.int32, (tr, 1), 0)
        t = loc % seq_t
        hp = hist_ref[...]
        u1 = jnp.where(t == 0, pltpu.roll(hp, tr - 1, axis=0), u1)
        u2 = jnp.where(t < 2, hp, u2)
    else:
        loc = lax.broadcasted_iota(jnp.int32, (8, 1), 0)
        inside = (i * tr) % seq_t != 0
        prev1 = jnp.where(inside, uh[7:8], 0.0)
        prev2 = jnp.where(inside, uh[6:7], 0.0)
        head1 = jnp.where(loc == 0, prev1, u1[0:8])
        head2 = jnp.where(loc == 0, prev2, jnp.where(loc == 1, prev1, u2[0:8]))
        u1 = jnp.concatenate([head1, u1[8:]], axis=0)
        u2 = jnp.concatenate([head2, u2[8:]], axis=0)
    w = w_ref[...]
    conv = w[0:1] * u2 + w[1:2] * u1 + w[2:3] * u
    y_ref[...] = (z[:, br:2 * br] * conv).astype(BF16)

    @pl.when(((i + 1) * tr) % max(seq_t, tr) == 0)
    def _():
        tail_ref[...] = u[tr - tail_rows:, :]


def _conv(z2d, hist_rows, w, seq_t, tr, tail_rows):
    n = z2d.shape[0]
    br = w.shape[1]
    has_hist = hist_rows is not None
    assert (n == tr and tr % seq_t == 0) if has_hist else seq_t % tr == 0
    group = max(seq_t, tr)
    n_tail = (n // group) * tail_rows
    in_specs = [pl.BlockSpec((tr, 3 * br), lambda i: (i, 0)),
                pl.BlockSpec((8, 3 * br), lambda i: (jnp.maximum(i * (tr // 8) - 1, 0), 0))]
    args = [z2d, z2d]
    if has_hist:
        in_specs.append(pl.BlockSpec((tr, br), lambda i: (i, 0)))
        args.append(hist_rows)
    in_specs.append(pl.BlockSpec((CONV_K, br), lambda i: (0, 0)))
    args.append(w)
    return pl.pallas_call(
        functools.partial(_conv_kernel, seq_t=seq_t, tr=tr, tail_rows=tail_rows, has_hist=has_hist),
        grid=(n // tr,),
        in_specs=in_specs,
        out_specs=[pl.BlockSpec((tr, br), lambda i: (i, 0)),
                   pl.BlockSpec((tail_rows, br), lambda i: ((i * tr) // group, 0))],
        out_shape=[jax.ShapeDtypeStruct((n, br), BF16), jax.ShapeDtypeStruct((n_tail, br), F32)],
        compiler_params=_cparams(("arbitrary",)),
        name="conv",
    )(*args)


def _hgrn_intra_diag(q, bk, b2, chunk):
    lane = lax.broadcasted_iota(jnp.int32, (HG_BLOCK, chunk), 1)
    trow = lax.broadcasted_iota(jnp.int32, (HG_BLOCK, chunk), 0)
    a_rows = []
    for r in range(0, chunk, HG_BLOCK):
        b_blk = b2[r:r + HG_BLOCK]
        q_blk = q[r:r + HG_BLOCK]
        acc = jnp.zeros((HG_BLOCK, chunk), F32)
        for s in range(HG_BLOCK):
            p = jnp.exp2(b_blk - bk[r + s:r + s + 1]) * q_blk
            acc = jnp.where(lane == r + s, jnp.sum(p, axis=-1, keepdims=True), acc)
        a_rows.append(jnp.where(lane - r <= trow, acc, 0.0))
    return a_rows[0] if len(a_rows) == 1 else jnp.concatenate(a_rows, axis=0)


def _hgrn_inter_block_factors(q, bk, b2, chunk):
    nb = chunk // HG_BLOCK
    zero = jnp.zeros((HG_BLOCK, HEAD_W), F32)
    q_rows, k_rows = [], []
    for i in range(nb):
        blk = slice(i * HG_BLOCK, (i + 1) * HG_BLOCK)
        q_tiles, k_tiles = [], []
        for j in range(nb - 1):
            rho = b2[(j + 1) * HG_BLOCK - 1:(j + 1) * HG_BLOCK]
            q_tiles.append(q[blk] * jnp.exp2(b2[blk] - rho) if j < i else zero)
            k_tiles.append(jnp.exp2(rho - bk[blk]) if j == i else zero)
        q_rows.append(jnp.concatenate(q_tiles, axis=1))
        k_rows.append(jnp.concatenate(k_tiles, axis=1))
    return jnp.concatenate(q_rows, axis=0).astype(BF16), jnp.concatenate(k_rows, axis=0).astype(BF16)


def _hgrn_tile(load, store, lb, gain, sts, valid_fn, tril_bf, chunk, n_c):
    units = [(s, c) for s in range(len(sts)) for c in range(n_c)]
    chunks = range(len(units))
    q, v, v_t, lk, g_hi, g_lo = [], [], [], [], [], []
    for s, c in units:
        qc, zf, vc = load(s, c)
        f = lb + (1.0 - lb) * jax.nn.sigmoid(zf)
        g = jnp.log(f)
        k = 1.0 - f
        valid = valid_fn(c)
        if valid is not None:
            g = jnp.where(valid, g, 0.0)
            k = jnp.where(valid, k, 0.0)
        hi = g.astype(BF16)
        q.append(qc)
        v.append(vc.astype(BF16))
        v_t.append(vc.T.astype(BF16))
        lk.append(jnp.log2(k))
        g_hi.append(hi)
        g_lo.append((g - hi.astype(F32)).astype(BF16))
    b2 = [(_dot(tril_bf, g_hi[c]) + _dot(tril_bf, g_lo[c])) * LOG2E for c in chunks]
    bk = [b2[c] - lk[c] for c in chunks]
    bl2 = [b[chunk - 1:chunk, :] for b in b2]
    upd = [_dot(v_t[c], jnp.exp2(bl2[c] - bk[c]).astype(BF16)) for c in chunks]
    a = [_hgrn_intra_diag(q[c], bk[c], b2[c], chunk) for c in chunks]
    if chunk > HG_BLOCK:
        factors = [_hgrn_inter_block_factors(q[c], bk[c], b2[c], chunk) for c in chunks]
        a = [a[c] + _dot_nt(factors[c][0], factors[c][1]) for c in chunks]
    o = [_dot(a[c].astype(BF16), v[c]) for c in chunks]
    sts = list(sts)
    states = []
    for u, (s, _) in enumerate(units):
        states.append(sts[s].astype(BF16))
        sts[s] = sts[s] * jnp.exp2(bl2[u]) + upd[u]
    for u, (s, c) in enumerate(units):
        oc = o[u] + _dot_nt((q[u] * jnp.exp2(b2[u])).astype(BF16), states[u])
        store(s, c, _rms(oc, gain).astype(BF16))
    return sts


def _hgrn_kernel(*refs, sb, tt, chunk, t_valid, fill_slots):
    q_ref, f_ref, i_ref, lb_ref, g_ref, s0_ref = refs[:6]
    y_ref, sout_ref, st_ref = refs[-3:]
    tstep = pl.program_id(2)
    n_t = pl.num_programs(2)

    @pl.when(tstep == 0)
    def _():
        for s in range(sb):
            st_ref[s] = s0_ref[s, 0].T

    lb = lb_ref[0]
    gain = g_ref[0]
    row = lax.broadcasted_iota(jnp.int32, (chunk, chunk), 0)
    col = lax.broadcasted_iota(jnp.int32, (chunk, chunk), 1)
    tril_bf = (row >= col).astype(BF16)
    scale = HEAD_W ** -0.5

    def load(s, c):
        rows = pl.ds(c * chunk, chunk)
        return (q_ref[s, rows, :].astype(F32) * scale, f_ref[s, rows, :].astype(F32),
                i_ref[s, rows, :].astype(F32))

    def store(s, c, y):
        y_ref[s, pl.ds(c * chunk, chunk), :] = y

    def valid_fn(c):
        if t_valid is None:
            return None
        return tstep * tt + c * chunk + lax.broadcasted_iota(jnp.int32, (chunk, 1), 0) < t_valid

    new_states = _hgrn_tile(load, store, lb, gain, [st_ref[s] for s in range(sb)], valid_fn, tril_bf, chunk,
                            tt // chunk)
    for s in range(sb):
        st_ref[s] = new_states[s]

    @pl.when(tstep == n_t - 1)
    def _():
        for s in range(sb):
            final = st_ref[s].T
            if fill_slots is None:
                sout_ref[s, 0] = final
            else:
                for slot in range(fill_slots):
                    sout_ref[slot, s, 0] = final


def _hgrn(z3d, lb, gain, s0, s0_layer, layer, sb, tt, chunk, t_valid, depth, stacked_prev):
    b, t, _ = z3d.shape
    cb = COL_HG // HEAD_W
    first = stacked_prev is None
    assert first == (layer == 0)
    kern = functools.partial(_hgrn_kernel, sb=sb, tt=tt, chunk=chunk, t_valid=t_valid,
                             fill_slots=depth if first else None)
    state_spec = (pl.BlockSpec((depth, sb, 1, HEAD_W, HEAD_W), lambda i, h, k: (0, i, h, 0, 0)) if first else
                  pl.BlockSpec((None, sb, 1, HEAD_W, HEAD_W), lambda i, h, k: (layer, i, h, 0, 0)))
    in_specs = [pl.BlockSpec((sb, tt, HEAD_W), lambda i, h, k: (i, k, cb + h)),
                pl.BlockSpec((sb, tt, HEAD_W), lambda i, h, k: (i, k, cb + N_HEADS + h)),
                pl.BlockSpec((sb, tt, HEAD_W), lambda i, h, k: (i, k, cb + 2 * N_HEADS + h)),
                pl.BlockSpec((1, 1, HEAD_W), lambda i, h, k: (h, 0, 0)),
                pl.BlockSpec((1, 1, HEAD_W), lambda i, h, k: (h, 0, 0)),
                pl.BlockSpec((None, sb, 1, HEAD_W, HEAD_W), lambda i, h, k: (s0_layer, i, h, 0, 0))]
    args = [z3d, z3d, z3d, lb.reshape(N_HEADS, 1, HEAD_W), gain.reshape(N_HEADS, 1, HEAD_W), s0]
    aliases = {}
    if stacked_prev is not None:
        in_specs.append(pl.BlockSpec(memory_space=pl.ANY))
        args.append(stacked_prev)
        aliases = {len(args) - 1: 1}
    return pl.pallas_call(
        kern,
        grid=(b // sb, N_HEADS, t // tt),
        in_specs=in_specs,
        out_specs=[pl.BlockSpec((sb, tt, HEAD_W), lambda i, h, k: (i, k, h)),
                   state_spec],
        out_shape=[jax.ShapeDtypeStruct((b, t, N_HEADS * HEAD_W), BF16),
                   jax.ShapeDtypeStruct((depth, b, N_HEADS, HEAD_W, HEAD_W), F32)],
        scratch_shapes=[pltpu.VMEM((sb, HEAD_W, HEAD_W), F32)],
        input_output_aliases=aliases,
        compiler_params=_cparams(("arbitrary", "arbitrary", "arbitrary")),
        name="hgrn",
    )(*args)


def _mla_prep_kernel(z_ref, cos_ref, sin_ref, gq_ref, wuq_ref, gkv_ref, wuk_ref, wabs_ref, grope_ref,
                     indq_ref, indk_ref, c_ref, r_ref, kv_ref, rk_ref, qp_ref):
    z = z_ref[...].astype(F32)
    ql = z[:, 0:MLA_Q_RANK]
    kvl = z[:, 256:384]
    kpe = z[:, 384:512]
    cos = cos_ref[...]
    sin = sin_ref[...]
    lane = lax.broadcasted_iota(jnp.int32, (1, 128), 1)

    qf = _dot(_rms(ql, gq_ref[...]).astype(BF16), wuq_ref[...])
    q_nope = qf[:, 0:256]
    rq_in = qf[:, 256:384]
    rot_q = rq_in * cos + pltpu.roll(rq_in, 64, axis=1) * sin
    rot_k = kpe * cos + pltpu.roll(kpe, 64, axis=1) * sin

    c = _rms(kvl, gkv_ref[...])
    c_ref[...] = c
    r32 = jnp.where(lane < 16, rot_k, pltpu.roll(rot_k, 80, axis=1))
    r32 = jnp.where(lane < MLA_ROPE, r32, 0.0)
    r_ref[...] = r32[:, 0:MLA_ROPE]
    cb = c.astype(BF16)
    kv_ref[...] = jnp.concatenate([cb, r32.astype(BF16)], axis=1)

    kn = _dot(cb, wuk_ref[...])
    kcat2 = jnp.concatenate([kn * kn, rot_k * rot_k], axis=1).astype(BF16)
    ssk = _dot_nt(indk_ref[...], kcat2)
    rk_ref[...] = lax.rsqrt(ssk * (1.0 / MLA_QK) + EPS)

    qcat2 = jnp.concatenate([q_nope * q_nope, rot_q * rot_q], axis=1).astype(BF16)
    ssq = _dot(qcat2, indq_ref[...])
    rq = lax.rsqrt(ssq * (1.0 / MLA_QK) + EPS) * (MLA_QK ** -0.5 * LOG2E)

    q_abs = _dot(q_nope.astype(BF16), wabs_ref[...])
    grope = grope_ref[...]
    half = MLA_ROPE // 2
    rot_q_hi = pltpu.roll(rot_q, 80, axis=1)
    for h in range(N_HEADS):
        lo = half * h
        pair = jnp.where((lane >= lo) & (lane < lo + half), rot_q,
                         jnp.where((lane >= lo + half) & (lane < lo + 2 * half), rot_q_hi, 0.0))
        qr = (pair if h == 0 else pltpu.roll(pair, 128 - lo, axis=1)) * grope
        rq_h = rq[:, h * 128:(h + 1) * 128]
        qp_ref[h, :, 0:128] = (q_abs[:, h * 128:(h + 1) * 128] * rq_h).astype(BF16)
        qp_ref[h, :, 128:256] = (qr * rq_h).astype(BF16)


def _mla_prep(z2d, cos_t, sin_t, pw, tm):
    n = z2d.shape[0]
    n_tab = cos_t.shape[0] // tm
    full = lambda shape: pl.BlockSpec(shape, lambda i: (0,) * len(shape))
    return pl.pallas_call(
        _mla_prep_kernel,
        grid=(n // tm,),
        in_specs=[pl.BlockSpec((tm, 512), lambda i: (i, COL_MLA // 512)),
                  pl.BlockSpec((tm, 128), lambda i: (i % n_tab, 0)),
                  pl.BlockSpec((tm, 128), lambda i: (i % n_tab, 0)),
                  full((1, MLA_Q_RANK)), full((MLA_Q_RANK, 384)), full((1, MLA_KV_RANK)),
                  full((MLA_KV_RANK, 256)), full((256, 512)), full((1, 128)),
                  full((384, N_HEADS * 128)), full((8, 384))],
        out_specs=[pl.BlockSpec((tm, MLA_KV_RANK), lambda i: (i, 0)),
                   pl.BlockSpec((tm, MLA_ROPE), lambda i: (i, 0)),
                   pl.BlockSpec((tm, 256), lambda i: (i, 0)),
                   pl.BlockSpec((8, tm), lambda i: (0, i)),
                   pl.BlockSpec((N_HEADS, tm, 256), lambda i: (0, i, 0))],
        out_shape=[jax.ShapeDtypeStruct((n, MLA_KV_RANK), F32),
                   jax.ShapeDtypeStruct((n, MLA_ROPE), F32),
                   jax.ShapeDtypeStruct((n, 256), BF16),
                   jax.ShapeDtypeStruct((8, n), F32),
                   jax.ShapeDtypeStruct((N_HEADS, n, 256), BF16)],
        compiler_params=_cparams(("arbitrary",)),
        name="mla_prep",
    )(z2d, cos_t, sin_t, pw["gq_norm"], pw["wuq"], pw["gkv_norm"], pw["wuk"], pw["wabs"], pw["grope"],
      pw["indq"], pw["indk"])


def _mla_prompt_kernel(qp_ref, kv_ref, rk_ref, wuv_ref, y_ref, m_ref, l_ref, acc_ref, *, tq, rb):
    i = pl.program_id(1)
    nrow = N_HEADS * tq
    m_ref[...] = jnp.full((nrow, 128), NEG, F32)
    l_ref[...] = jnp.zeros((nrow, 128), F32)
    acc_ref[...] = jnp.zeros((nrow, MLA_KV_RANK), F32)
    row = lax.broadcasted_iota(jnp.int32, (rb, 128), 0)
    col = lax.broadcasted_iota(jnp.int32, (rb, 128), 1)
    def block(k0, diagonal):
        kv = kv_ref[pl.ds(k0, tq), :]
        cv = kv[:, 0:MLA_KV_RANK]
        units = [(h, 0, tq, tq) for h in range(N_HEADS)]

        def qk(unit):
            h, r0, nr, nk = unit
            return _dot_nt(qp_ref[h, pl.ds(r0, nr), :], kv[0:nk])

        def softmax_update(unit, s_u):
            h, r0, nr, nk = unit
            p_rows = []
            for r in range(0, nr, rb):
                rows = pl.ds(h * tq + r0 + r, rb)
                tiles = []
                for c in range(0, nk, 128):
                    s = s_u[r:r + rb, c:c + 128] * rk_ref[h:h + 1, pl.ds(k0 + c, 128)]
                    if diagonal:
                        s = jnp.where(col + c <= row + (r0 + r), s, NEG)
                    tiles.append(s)
                m_old = m_ref[rows, :]
                m_new = jnp.maximum(m_old, jnp.max(functools.reduce(jnp.maximum, tiles), axis=-1, keepdims=True))
                alpha = jnp.exp2(m_old - m_new)
                probs = [jnp.exp2(s - m_new) for s in tiles]
                l_ref[rows, :] = alpha * l_ref[rows, :] + jnp.sum(functools.reduce(jnp.add, probs), axis=-1,
                                                                  keepdims=True)
                m_ref[rows, :] = m_new
                acc_ref[rows, :] = alpha * acc_ref[rows, :]
                p_rows.append(jnp.concatenate([p.astype(BF16) for p in probs], axis=1))
            return jnp.concatenate(p_rows, axis=0)

        def pv(unit, p_u):
            h, r0, nr, nk = unit
            acc_ref[pl.ds(h * tq + r0, nr), :] += _dot(p_u, cv[0:nk])

        n_u = len(units)
        scores = {0: qk(units[0])}
        probs = {}
        for u in range(n_u):
            if u + 1 < n_u:
                scores[u + 1] = qk(units[u + 1])
            probs[u] = softmax_update(units[u], scores.pop(u))
            if u >= 1:
                pv(units[u - 1], probs.pop(u - 1))
        pv(units[n_u - 1], probs.pop(n_u - 1))

    def step(j, carry):
        block(pl.multiple_of(j * tq, tq), False)
        return carry

    lax.fori_loop(0, i, step, 0)
    block(pl.multiple_of(i * tq, tq), True)
    for h in range(N_HEADS):
        rows = pl.ds(h * tq, tq)
        o = (acc_ref[rows, :] / l_ref[rows, :]).astype(BF16)
        y_ref[:, h * HEAD_W:(h + 1) * HEAD_W] = _dot(o, wuv_ref[:, h * HEAD_W:(h + 1) * HEAD_W]).astype(BF16)


def _mla_prompt(qp, kv, rk, wuv, b, t, tq):
    n = b * t
    nq = t // tq
    nrow = N_HEADS * tq
    return pl.pallas_call(
        functools.partial(_mla_prompt_kernel, tq=tq, rb=min(tq, MLA_ROW_GROUP)),
        grid=(b, nq),
        scratch_shapes=[pltpu.VMEM((nrow, 128), F32), pltpu.VMEM((nrow, 128), F32),
                        pltpu.VMEM((nrow, MLA_KV_RANK), F32)],
        in_specs=[pl.BlockSpec((N_HEADS, tq, 256), lambda bi, i: (0, bi * nq + i, 0)),
                  pl.BlockSpec((t, 256), lambda bi, i: (bi, 0)),
                  pl.BlockSpec((8, t), lambda bi, i: (0, bi)),
                  pl.BlockSpec((MLA_KV_RANK, N_HEADS * HEAD_W), lambda bi, i: (0, 0))],
        out_specs=pl.BlockSpec((tq, N_HEADS * HEAD_W), lambda bi, i: (bi * nq + i, 0)),
        out_shape=jax.ShapeDtypeStruct((n, N_HEADS * HEAD_W), BF16),
        compiler_params=_cparams(("arbitrary", "arbitrary")),
        name="mla_prompt",
    )(qp, kv, rk, wuv)


def _mla_sample_kernel(pt_ref, qp_ref, cnew_ref, rnewt_ref, wukt_ref, wuv_ref, lat_hbm, ropet_hbm,
                       y_ref, cbuf, rbuf, cbf, s_scr, part_scr, sem, *, layer, n_pages, ck, tv):
    b = pl.program_id(0)
    n_b = pl.num_programs(0)
    past = n_pages * PAGE
    tk = past + PAGE
    slot = b % 2
    tp = SAMPLE_PAD_T
    nrow = N_HEADS * tv
    npart = N_HEADS * 8

    def lat_copy(pg, p, sl):
        return pltpu.make_async_copy(lat_hbm.at[layer, pg], cbuf.at[sl, pl.ds(p * PAGE, PAGE), :], sem.at[0, sl])

    def rope_copy(pg, p, sl):
        return pltpu.make_async_copy(ropet_hbm.at[layer, pg], rbuf.at[sl, :, pl.ds(p * PAGE, PAGE)], sem.at[1, sl])

    def issue(seq, sl):
        def body(p, carry):
            pg = pt_ref[seq * n_pages + p]
            lat_copy(pg, p, sl).start()
            rope_copy(pg, p, sl).start()
            return carry
        lax.fori_loop(0, n_pages, body, 0, unroll=4)

    @pl.when(b == 0)
    def _():
        for sl in range(2):
            cbuf[sl, pl.ds(past, PAGE), :] = jnp.zeros((PAGE, MLA_KV_RANK), F32)
        issue(0, 0)

    def wait_all(sl):
        def wait_body(p, carry):
            lat_copy(0, p, sl).wait()
            rope_copy(0, p, sl).wait()
            return carry
        lax.fori_loop(0, n_pages, wait_body, 0, unroll=4)

    wait_all(slot)
    nxt = jnp.minimum(b + 1, n_b - 1)
    n_chunks = tk // ck
    pages_per_chunk = -(-n_pages // n_chunks)

    def prefetch(chunk_idx):
        for p in range(chunk_idx * pages_per_chunk, min((chunk_idx + 1) * pages_per_chunk, n_pages)):
            pg = pt_ref[nxt * n_pages + p]
            lat_copy(pg, p, 1 - slot).start()
            rope_copy(pg, p, 1 - slot).start()

    cbuf[slot, pl.ds(past, tp), :] = cnew_ref[0]
    rbuf[slot, :, pl.ds(past, PAGE)] = rnewt_ref[0]

    q = qp_ref[0]
    qr = q[:, MLA_KV_RANK:MLA_KV_RANK + MLA_ROPE]
    w_stack = jnp.concatenate([wukt_ref[...], q[:, 0:MLA_KV_RANK]], axis=0)
    n_up = N_HEADS * MLA_NOPE
    qpos = past + lax.broadcasted_iota(jnp.int32, (nrow, 1), 0) % tv
    head_ones = (lax.broadcasted_iota(jnp.int32, (nrow, npart), 0) // tv
                 == lax.broadcasted_iota(jnp.int32, (nrow, npart), 1) // 8).astype(BF16)

    for k0 in range(0, tk, ck):
        prefetch(k0 // ck)
        cb = cbuf[slot, pl.ds(k0, ck), :].astype(BF16)
        cbf[pl.ds(k0, ck), :] = cb
        rt = rbuf[slot, :, pl.ds(k0, ck)]
        big = _dot_nt(w_stack, cb)
        kn2 = big[0:n_up] * big[0:n_up]
        part_r = jnp.sum((rt * rt).reshape(MLA_ROPE // 8, 8, ck), axis=0)
        parts = [jnp.sum(kn2[h * MLA_NOPE:(h + 1) * MLA_NOPE].reshape(MLA_NOPE // 8, 8, ck), axis=0) + part_r
                 for h in range(N_HEADS)]
        part_scr[:, pl.ds(k0, ck)] = jnp.concatenate(parts, axis=0).astype(BF16)
        s_scr[:, pl.ds(k0, ck)] = big[n_up:n_up + nrow]

    rk = lax.rsqrt(_dot(head_ones, part_scr[...]) * (1.0 / MLA_QK) + EPS)
    s = (s_scr[...] + _dot(qr, rbuf[slot].astype(BF16))) * rk
    kpos = lax.broadcasted_iota(jnp.int32, (1, tk), 1)
    s = jnp.where(kpos <= qpos, s, NEG)
    m = jnp.max(s, axis=-1, keepdims=True)
    p = jnp.exp2(s - m)
    l = jnp.sum(p, axis=-1, keepdims=True)
    o = _dot(p.astype(BF16), cbf[...]) / l
    full = _dot(o.astype(BF16), wuv_ref[...])
    lane_head = lax.broadcasted_iota(jnp.int32, (tp, N_HEADS * HEAD_W), 1) // HEAD_W
    token = lax.broadcasted_iota(jnp.int32, (tp, N_HEADS * HEAD_W), 0)
    y = jnp.zeros((tp, N_HEADS * HEAD_W), F32)
    for h in range(N_HEADS):
        tile = full[(h * tv) // tp * tp:(h * tv) // tp * tp + tp]
        if (h * tv) % tp:
            tile = pltpu.roll(tile, tp - (h * tv) % tp, axis=0)
        y = jnp.where(lane_head == h, tile, y)
    y_ref[0] = jnp.where(token < tv, y, 0.0).astype(BF16)

    @pl.when(b == n_b - 1)
    def _():
        wait_all(1 - slot)


def _mla_sample(page_flat, qp3, c3, rt3, wukt, wuv, lat, ropet, layer, n_pages):
    bs, nrow, _ = qp3.shape
    tp = SAMPLE_PAD_T
    tv = nrow // N_HEADS
    assert tp % tv == 0 and nrow % 8 == 0
    tk = n_pages * PAGE + PAGE
    ck = max(d for d in range(PAGE, MLA_KEY_CHUNK + 1, PAGE) if tk % d == 0)
    kern = functools.partial(_mla_sample_kernel, layer=layer, n_pages=n_pages, ck=ck, tv=tv)
    grid_spec = pltpu.PrefetchScalarGridSpec(
        num_scalar_prefetch=1,
        grid=(bs,),
        in_specs=[pl.BlockSpec((1, nrow, 256), lambda b, pt: (b, 0, 0)),
                  pl.BlockSpec((1, tp, MLA_KV_RANK), lambda b, pt: (b, 0, 0)),
                  pl.BlockSpec((1, MLA_ROPE, PAGE), lambda b, pt: (b, 0, 0)),
                  pl.BlockSpec((N_HEADS * MLA_NOPE, MLA_KV_RANK), lambda b, pt: (0, 0)),
                  pl.BlockSpec((MLA_KV_RANK, N_HEADS * HEAD_W), lambda b, pt: (0, 0)),
                  pl.BlockSpec(memory_space=pl.ANY),
                  pl.BlockSpec(memory_space=pl.ANY)],
        out_specs=pl.BlockSpec((1, tp, N_HEADS * HEAD_W), lambda b, pt: (b, 0, 0)),
        scratch_shapes=[pltpu.VMEM((2, tk, MLA_KV_RANK), F32),
                        pltpu.VMEM((2, MLA_ROPE, tk), F32),
                        pltpu.VMEM((tk, MLA_KV_RANK), BF16),
                        pltpu.VMEM((nrow, tk), F32),
                        pltpu.VMEM((N_HEADS * 8, tk), BF16),
                        pltpu.SemaphoreType.DMA((2, 2))],
    )
    return pl.pallas_call(
        kern,
        grid_spec=grid_spec,
        out_shape=jax.ShapeDtypeStruct((bs, tp, N_HEADS * HEAD_W), BF16),
        compiler_params=_cparams(("arbitrary",)),
        name="mla_sample",
    )(page_flat, qp3, c3, rt3, wukt, wuv, lat, ropet)


def _memkv_kernel(x_ref, g_ref, wk_ref, wv_ref, gk_ref, k_ref, v_ref):
    mn = _rms(x_ref[...], g_ref[...]).astype(BF16)
    k = _dot(mn, wk_ref[...])
    gk = gk_ref[...]
    tm = x_ref.shape[0]
    v = _dot(mn, wv_ref[...])
    for h in range(N_HEADS):
        sl = slice(h * HEAD_W, (h + 1) * HEAD_W)
        k_ref[pl.ds(h, tm, stride=N_HEADS), :] = _rms(k[:, sl], gk)
        v_ref[pl.ds(h, tm, stride=N_HEADS), :] = v[:, sl]


def _memkv(mem2d, g, wk, wv, gk, tm):
    n, d = mem2d.shape
    br = wk.shape[1]
    return pl.pallas_call(
        _memkv_kernel,
        grid=(n // tm,),
        in_specs=[pl.BlockSpec((tm, d), lambda i: (i, 0)),
                  pl.BlockSpec((1, d), lambda i: (0, 0)),
                  pl.BlockSpec((d, br), lambda i: (0, 0)),
                  pl.BlockSpec((d, br), lambda i: (0, 0)),
                  pl.BlockSpec((1, HEAD_W), lambda i: (0, 0))],
        out_specs=[pl.BlockSpec((tm * N_HEADS, HEAD_W), lambda i: (i, 0)),
                   pl.BlockSpec((tm * N_HEADS, HEAD_W), lambda i: (i, 0))],
        out_shape=[jax.ShapeDtypeStruct((n * N_HEADS, HEAD_W), F32),
                   jax.ShapeDtypeStruct((n * N_HEADS, HEAD_W), F32)],
        compiler_params=_cparams(("arbitrary",)),
        name="memkv",
    )(mem2d, g.reshape(1, d), wk, wv, gk.reshape(1, HEAD_W))


def _mem_attn_kernel(q_ref, k_ref, v_ref, gq_ref, y_ref, *, sb, m):
    gq = gq_ref[...]
    scale = HEAD_W ** -0.5
    heads = range(N_HEADS)
    for s in range(sb):
        q = q_ref[s].astype(F32)
        qh = [(_rms(q[:, h * HEAD_W:(h + 1) * HEAD_W], gq) * (scale * LOG2E)).astype(BF16) for h in heads]
        sc = [_dot_nt(qh[h], k_ref[s, pl.ds(h, m, stride=N_HEADS), :].astype(BF16)) for h in heads]
        ps = [jnp.exp2(sc[h] - jnp.max(sc[h], axis=-1, keepdims=True)).astype(BF16) for h in heads]
        ones = jnp.ones((m, HEAD_W), BF16)
        for h in heads:
            v_ext = jnp.concatenate([v_ref[s, pl.ds(h, m, stride=N_HEADS), :].astype(BF16), ones], axis=1)
            o = _dot(ps[h], v_ext)
            y_ref[s, :, h * HEAD_W:(h + 1) * HEAD_W] = (o[:, 0:HEAD_W] / o[:, HEAD_W:2 * HEAD_W]).astype(BF16)


def _mem_attn(z3d, k4, v4, gq, layer, sb, tq):
    b, t, _ = z3d.shape
    m = k4.shape[2] // N_HEADS
    br = N_HEADS * HEAD_W
    return pl.pallas_call(
        functools.partial(_mem_attn_kernel, sb=sb, m=m),
        grid=(b // sb, t // tq),
        in_specs=[pl.BlockSpec((sb, tq, br), lambda i, j: (i, j, COL_MEMQ // br)),
                  pl.BlockSpec((None, sb, m * N_HEADS, HEAD_W), lambda i, j: (layer, i, 0, 0)),
                  pl.BlockSpec((None, sb, m * N_HEADS, HEAD_W), lambda i, j: (layer, i, 0, 0)),
                  pl.BlockSpec((1, HEAD_W), lambda i, j: (0, 0))],
        out_specs=pl.BlockSpec((sb, tq, br), lambda i, j: (i, j, 0)),
        out_shape=jax.ShapeDtypeStruct((b, t, br), BF16),
        compiler_params=_cparams(("arbitrary", "arbitrary")),
        name="mem_attn",
    )(z3d, k4, v4, gq.reshape(1, HEAD_W))


def _outproj_kernel(x_ref, mg_ref, sg_ref, y0_ref, y1_ref, y2_ref, y3_ref, wb_ref, wo_ref, o_ref):
    d = x_ref.shape[1]
    br = y0_ref.shape[1]
    merged = None
    for n, y_ref in enumerate((y0_ref, y1_ref, y2_ref, y3_ref)):
        sg = sg_ref[:, n * br:(n + 1) * br].astype(F32)
        ys = (y_ref[...].astype(F32) * (sg * (1.0 + jnp.tanh(sg)))).astype(BF16)
        proj = _dot(ys, wb_ref[n])
        term = (1.0 + jnp.tanh(mg_ref[:, n * d:(n + 1) * d].astype(F32))) * proj
        merged = term if merged is None else merged + term
    o_ref[...] = x_ref[...] + _dot(merged.astype(BF16), wo_ref[...])


def _outproj(x2d, z2d, ys, wb, wo, tm):
    n, d = x2d.shape
    br = ys[0].shape[1]
    nb = wb.shape[0]
    yspec = pl.BlockSpec((tm, br), lambda i: (i, 0))
    return pl.pallas_call(
        _outproj_kernel,
        grid=(n // tm,),
        in_specs=[pl.BlockSpec((tm, d), lambda i: (i, 0)),
                  pl.BlockSpec((tm, nb * d), lambda i: (i, COL_MG // (nb * d))),
                  pl.BlockSpec((tm, nb * br), lambda i: (i, COL_SG // (nb * br))),
                  yspec, yspec, yspec, yspec,
                  pl.BlockSpec((nb, br, d), lambda i: (0, 0, 0)),
                  pl.BlockSpec((d, d), lambda i: (0, 0))],
        out_specs=pl.BlockSpec((tm, d), lambda i: (i, 0)),
        out_shape=jax.ShapeDtypeStruct((n, d), F32),
        compiler_params=_cparams(("arbitrary",)),
        name="outproj",
    )(x2d, z2d, z2d, *ys, wb, wo)


def _rope_tables(pos):
    half = MLA_ROPE // 2
    freqs = ROPE_THETA ** (-np.arange(half, dtype=np.float64) / half)
    ang = np.asarray(pos, np.float64)[:, None] * freqs[None, :]
    cos = np.tile(np.cos(ang), (1, 8))
    sin = np.tile(np.sin(ang), (1, 8))
    sin[:, :64] *= -1.0
    return jnp.asarray(cos, F32), jnp.asarray(sin, F32)


def _mla_params(q_norm, w_uq, kv_norm, w_uk, gq, gk):
    half = MLA_ROPE // 2
    wq = w_uq.reshape(MLA_Q_RANK, N_HEADS, MLA_QK)
    wuq = jnp.concatenate([wq[:, :, :MLA_NOPE].reshape(MLA_Q_RANK, -1),
                           wq[:, :, MLA_NOPE:MLA_NOPE + half].reshape(MLA_Q_RANK, -1),
                           wq[:, :, MLA_NOPE + half:].reshape(MLA_Q_RANK, -1)], axis=1).astype(BF16)
    g2 = gq * gk
    wk = w_uk.reshape(MLA_KV_RANK, N_HEADS, MLA_NOPE)
    wabs = jnp.zeros((N_HEADS * MLA_NOPE, N_HEADS * MLA_KV_RANK), F32)
    for h in range(N_HEADS):
        blk = (wk[:, h, :] * g2[None, :MLA_NOPE]).T
        wabs = wabs.at[h * MLA_NOPE:(h + 1) * MLA_NOPE, h * MLA_KV_RANK:(h + 1) * MLA_KV_RANK].set(blk)
    grope = jnp.zeros((1, 128), F32).at[0, :MLA_ROPE].set(g2[MLA_NOPE:])
    j = np.arange(384)
    head_of = np.where(j < 256, j // MLA_NOPE, (j % 64) // half)
    indq = (head_of[:, None] == (np.arange(N_HEADS * 128) // 128)[None, :]).astype(np.float32)
    lane = j - 256
    is_rope = (j >= 256) & ((lane < half) | ((lane >= 64) & (lane < 64 + half)))
    indk = np.zeros((8, 384), np.float32)
    for h in range(N_HEADS):
        indk[h] = ((j < 256) & (j // MLA_NOPE == h)) | is_rope
    return dict(gq_norm=q_norm.reshape(1, -1), wuq=wuq, gkv_norm=kv_norm.reshape(1, -1),
                wuk=w_uk.astype(BF16), wukt=w_uk.T.astype(BF16), wabs=wabs.astype(BF16), grope=grope,
                indq=jnp.asarray(indq, BF16), indk=jnp.asarray(indk, BF16))


def _hgrn_lower_bounds(lb_param):
    p = jax.nn.softmax(lb_param.astype(F32), axis=0)
    return jnp.cumsum(p, axis=0) - p[0]


def kernel(x_prompt, x_sample, mem_prompt, cache_mla_latent, cache_mla_rope, page_table, state_hgrn, state_conv, cache_mem_k, cache_mem_v, norm_gain, w_in, conv_w, hgrn_lb, hgrn_norm, mla_q_norm, mla_w_uq, mla_kv_norm, mla_w_uk, mla_w_uv, mla_q_gain, mla_k_gain, mem_norm, mem_w_k, mem_w_v, mem_q_gain, mem_k_gain, w_branch_out, w_out):
    bp, tp, d = x_prompt.shape
    bs, ts, _ = x_sample.shape
    depth = w_in.shape[0]
    br = conv_w.shape[2]
    mem_len = mem_prompt.shape[1]
    n_pages = page_table.shape[1]
    past = n_pages * cache_mla_latent.shape[2]
    tpad = SAMPLE_PAD_T
    n_p = bp * tp
    n_s = bs * tpad

    lbs = _hgrn_lower_bounds(hgrn_lb)
    cos_p, sin_p = _rope_tables(np.arange(tp))
    cos_s, sin_s = _rope_tables(past + np.arange(tpad))
    tm_s = min(n_s, 1024)
    cos_s = jnp.tile(cos_s, (tm_s // tpad, 1))
    sin_s = jnp.tile(sin_s, (tm_s // tpad, 1))
    page_flat = page_table.reshape(-1).astype(jnp.int32)
    rope_t = jnp.swapaxes(cache_mla_rope, 2, 3)
    mem_k4 = cache_mem_k.reshape(depth, bs, mem_len * N_HEADS, HEAD_W)
    mem_v4 = cache_mem_v.reshape(depth, bs, mem_len * N_HEADS, HEAD_W)

    tm_p = min(tp, 512)
    xp = x_prompt.reshape(n_p, d)
    xs = jnp.pad(x_sample, ((0, 0), (0, tpad - ts), (0, 0))).reshape(n_s, d)
    mem2d = mem_prompt.reshape(bp * mem_len, d)
    zero_state = jnp.zeros((1, bp, N_HEADS, HEAD_W, HEAD_W), F32)

    outs = {k: [] for k in ("p_lat", "p_rope", "p_conv", "p_mk", "p_mv", "s_lat", "s_rope", "s_conv")}
    p_hg = s_hg = None
    for l in range(depth):
        w_in_t = _relayout_w_in(w_in[l])
        pw = _mla_params(mla_q_norm[l], mla_w_uq[l], mla_kv_norm[l], mla_w_uk[l], mla_q_gain[l], mla_k_gain[l])
        wuv = mla_w_uv[l].astype(BF16)
        wb = (0.5 * w_branch_out[l]).astype(BF16)
        wo = w_out[l].astype(BF16)

        z = _inproj(xp, norm_gain[l], w_in_t, min(n_p, 1024), 2048)
        z3 = z.reshape(bp, tp, N_COLS)
        y_conv, tail = _conv(z, None, conv_w[l], tp, tm_p, 8)
        y_hg, p_hg = _hgrn(z3, lbs[l], hgrn_norm[l], zero_state, 0, l, 1, min(tp, HG_TILE), HG_CHUNK, None, depth,
                           p_hg)
        c, r, kv, rk, qp = _mla_prep(z, cos_p, sin_p, pw, tm_p)
        y_mla = _mla_prompt(qp, kv, rk, wuv, bp, tp, min(tp, MLA_TQ))
        mk, mv = _memkv(mem2d, mem_norm[l], mem_w_k[l].astype(BF16), mem_w_v[l].astype(BF16), mem_k_gain[l],
                        min(bp * mem_len, 512))
        rows_m = mem_len * N_HEADS
        y_mem = _mem_attn(z3, mk.reshape(1, bp, rows_m, HEAD_W), mv.reshape(1, bp, rows_m, HEAD_W), mem_q_gain[l],
                          0, 1, tm_p)
        xp = _outproj(xp, z, (y_conv, y_hg.reshape(n_p, br), y_mla, y_mem.reshape(n_p, br)), wb, wo, tm_p)
        outs["p_lat"].append(c.reshape(bp, tp, MLA_KV_RANK))
        outs["p_rope"].append(r.reshape(bp, tp, MLA_ROPE))
        outs["p_conv"].append(tail.reshape(bp, 8, br)[:, 8 - (CONV_K - 1):])
        outs["p_mk"].append(mk.reshape(bp, mem_len, N_HEADS, HEAD_W))
        outs["p_mv"].append(mv.reshape(bp, mem_len, N_HEADS, HEAD_W))

        z = _inproj(xs, norm_gain[l], w_in_t, tm_s, 2048)
        z3 = z.reshape(bs, tpad, N_COLS)
        hist = jnp.pad(state_conv[l], ((0, 0), (0, tpad - (CONV_K - 1)), (0, 0))).reshape(n_s, br)
        y_conv, u_all = _conv(z, hist, conv_w[l], tpad, tm_s, tm_s)
        sb = 8 if bs % 8 == 0 else 1
        y_hg, s_hg = _hgrn(z3, lbs[l], hgrn_norm[l], state_hgrn, l, l, sb, tpad, tpad, ts, depth, s_hg)
        c, r, kv, rk, qp = _mla_prep(z, cos_s, sin_s, pw, tm_s)
        qp3 = qp.reshape(N_HEADS, bs, tpad, 256)[:, :, :ts].transpose(1, 0, 2, 3).reshape(bs, N_HEADS * ts, 256)
        rt3 = jnp.pad(r.reshape(bs, tpad, MLA_ROPE).transpose(0, 2, 1), ((0, 0), (0, 0), (0, PAGE - tpad)))
        y_mla = _mla_sample(page_flat, qp3, c.reshape(bs, tpad, MLA_KV_RANK), rt3, pw["wukt"], wuv,
                            cache_mla_latent, rope_t, l, n_pages)
        y_mem = _mem_attn(z3, mem_k4, mem_v4, mem_q_gain[l], l, sb, tpad)
        xs = _outproj(xs, z, (y_conv, y_hg.reshape(n_s, br), y_mla.reshape(n_s, br), y_mem.reshape(n_s, br)),
                      wb, wo, tm_s)
        outs["s_lat"].append(c.reshape(bs, tpad, MLA_KV_RANK)[:, :ts])
        outs["s_rope"].append(r.reshape(bs, tpad, MLA_ROPE)[:, :ts])
        outs["s_conv"].append(u_all.reshape(bs, tpad, br)[:, ts - (CONV_K - 1):ts])

    st = lambda k: jnp.stack(outs[k])
    return (xp.reshape(bp, tp, d), xs.reshape(bs, tpad, d)[:, :ts], st("p_lat"), st("p_rope"), p_hg,
            st("p_conv"), st("p_mk"), st("p_mv"), st("s_lat"), st("s_rope"), s_hg, st("s_conv"))
```

```python
import functools

import numpy as np
import jax
import jax.numpy as jnp
from jax import lax
from jax.experimental import pallas as pl
from jax.experimental.pallas import tpu as pltpu

F32 = jnp.float32
BF16 = jnp.bfloat16

N_HEADS = 4
HEAD_W = 128
MLA_NOPE = 64
MLA_ROPE = 32
MLA_QK = MLA_NOPE + MLA_ROPE
MLA_Q_RANK = 192
MLA_KV_RANK = 128
CONV_K = 3
ROPE_THETA = 10000.0
EPS = 1e-6
NEG = -1e30
PAGE = 128
SAMPLE_PAD_T = 8
HG_CHUNK = 64
HG_BLOCK = 8
HG_TILE = 2048
HG_SAMPLE_SEQS = 16
MLA_ROW_GROUP = 128
MLA_TQ = 512
MLA_KEY_CHUNK = 1664
CONV_TILE = 2048
PREP_TILE = 1024
MEM_TILE = 2048
LOG2E = 1.4426950408889634
VMEM_LIMIT = 56 * 1024 * 1024

COL_CONV = 0
COL_HG = 1536
COL_MLA = 3072
COL_MEMQ = 3584
COL_MG = 4096
COL_SG = 8192
N_COLS = 10240


def _cparams(sem):
    return pltpu.CompilerParams(dimension_semantics=sem, vmem_limit_bytes=VMEM_LIMIT)


def _rms(x, g):
    return x * lax.rsqrt(jnp.mean(x * x, axis=-1, keepdims=True) + EPS) * g


def _dot(a, b):
    return jnp.dot(a, b, preferred_element_type=F32)


def _dot_nt(a, b):
    return lax.dot_general(a, b, (((1,), (1,)), ((), ())), preferred_element_type=F32)


def _dot_tn(a, b):
    return lax.dot_general(a, b, (((0,), (0,)), ((), ())), preferred_element_type=F32)


def _inproj_kernel(x_ref, g_ref, w_ref, z_ref, hn_ref):
    @pl.when(pl.program_id(1) == 0)
    def _():
        hn_ref[...] = _rms(x_ref[...], g_ref[...]).astype(BF16)

    z_ref[...] = _dot_nt(hn_ref[...], w_ref[...]).astype(BF16)


def _relayout_w_in(w):
    wt = jnp.swapaxes(w, 0, 1)
    d = w.shape[0]
    rows = lambda a, b, scale=None: (wt[a:b] if scale is None else wt[a:b] * scale).astype(BF16)
    zeros = lambda n: jnp.zeros((n, d), BF16)
    o_q, o_kv, o_pe, o_mq, o_sg, o_mg = 3072, 3264, 3392, 3424, 3936, 5984
    half = MLA_ROPE // 2
    return jnp.concatenate([rows(0, o_q),
                            rows(o_q, o_kv), zeros(64),
                            rows(o_kv, o_pe),
                            rows(o_pe, o_pe + half), zeros(48), rows(o_pe + half, o_mq), zeros(48),
                            rows(o_mq, o_sg),
                            rows(o_mg, w.shape[1], 0.5),
                            rows(o_sg, o_mg, 0.5)], axis=0)


def _inproj(x2d, g, w_t, tm, tn):
    n, d = x2d.shape
    ncol = w_t.shape[0]
    return pl.pallas_call(
        _inproj_kernel,
        grid=(n // tm, ncol // tn),
        in_specs=[pl.BlockSpec((tm, d), lambda i, j: (i, 0)),
                  pl.BlockSpec((1, d), lambda i, j: (0, 0)),
                  pl.BlockSpec((tn, d), lambda i, j: (j, 0))],
        out_specs=pl.BlockSpec((tm, tn), lambda i, j: (i, j)),
        out_shape=jax.ShapeDtypeStruct((n, ncol), BF16),
        scratch_shapes=[pltpu.VMEM((tm, d), BF16)],
        compiler_params=_cparams(("arbitrary", "arbitrary")),
        name="inproj",
    )(x2d, g.reshape(1, d), w_t)


def _conv_kernel(*refs, seq_t, tr, tail_rows, has_hist):
    if has_hist:
        z_ref, halo_ref, hist_ref, w_ref, y_ref, tail_ref = refs
    else:
        z_ref, halo_ref, w_ref, y_ref, tail_ref = refs
    i = pl.program_id(0)
    br = y_ref.shape[1]
    z = z_ref[...].astype(F32)
    u = z[:, 2 * br:3 * br] * z[:, 0:br]
    zh = halo_ref[...].astype(F32)
    uh = zh[:, 2 * br:3 * br] * zh[:, 0:br]
    u1 = pltpu.roll(u, 1, axis=0)
    u2 = pltpu.roll(u, 2, axis=0)
    if has_hist:
        loc = lax.broadcasted_iota(jnp.int32, (tr, 1), 0)
        t = loc % seq_t
        hp = hist_ref[...]
        u1 = jnp.where(t == 0, pltpu.roll(hp, tr - 1, axis=0), u1)
        u2 = jnp.where(t < 2, hp, u2)
    else:
        loc = lax.broadcasted_iota(jnp.int32, (8, 1), 0)
        inside = (i * tr) % seq_t != 0
        prev1 = jnp.where(inside, uh[7:8], 0.0)
        prev2 = jnp.where(inside, uh[6:7], 0.0)
        head1 = jnp.where(loc == 0, prev1, u1[0:8])
        head2 = jnp.where(loc == 0, prev2, jnp.where(loc == 1, prev1, u2[0:8]))
        u1 = jnp.concatenate([head1, u1[8:]], axis=0)
        u2 = jnp.concatenate([head2, u2[8:]], axis=0)
    w = w_ref[...]
    conv = w[0:1] * u2 + w[1:2] * u1 + w[2:3] * u
    y_ref[...] = (z[:, br:2 * br] * conv).astype(BF16)

    @pl.when(((i + 1) * tr) % max(seq_t, tr) == 0)
    def _():
        tail_ref[...] = u[tr - tail_rows:, :]


def _conv(z2d, hist_rows, w, seq_t, tr, tail_rows):
    n = z2d.shape[0]
    br = w.shape[1]
    has_hist = hist_rows is not None
    assert (n == tr and tr % seq_t == 0) if has_hist else seq_t % tr == 0
    group = max(seq_t, tr)
    n_tail = (n // group) * tail_rows
    in_specs = [pl.BlockSpec((tr, 3 * br), lambda i: (i, 0)),
                pl.BlockSpec((8, 3 * br), lambda i: (jnp.maximum(i * (tr // 8) - 1, 0), 0))]
    args = [z2d, z2d]
    if has_hist:
        in_specs.append(pl.BlockSpec((tr, br), lambda i: (i, 0)))
        args.append(hist_rows)
    in_specs.append(pl.BlockSpec((CONV_K, br), lambda i: (0, 0)))
    args.append(w)
    return pl.pallas_call(
        functools.partial(_conv_kernel, seq_t=seq_t, tr=tr, tail_rows=tail_rows, has_hist=has_hist),
        grid=(n // tr,),
        in_specs=in_specs,
        out_specs=[pl.BlockSpec((tr, br), lambda i: (i, 0)),
                   pl.BlockSpec((tail_rows, br), lambda i: ((i * tr) // group, 0))],
        out_shape=[jax.ShapeDtypeStruct((n, br), BF16), jax.ShapeDtypeStruct((n_tail, br), F32)],
        compiler_params=_cparams(("arbitrary",)),
        name="conv",
    )(*args)


def _hgrn_intra_diag(q, bk, b2, chunk):
    lane = lax.broadcasted_iota(jnp.int32, (HG_BLOCK, chunk), 1)
    trow = lax.broadcasted_iota(jnp.int32, (HG_BLOCK, chunk), 0)
    a_rows = []
    for r in range(0, chunk, HG_BLOCK):
        b_blk = b2[r:r + HG_BLOCK]
        q_blk = q[r:r + HG_BLOCK]
        acc = jnp.zeros((HG_BLOCK, chunk), F32)
        for s in range(HG_BLOCK):
            p = jnp.exp2(b_blk - bk[r + s:r + s + 1]) * q_blk
            acc = jnp.where(lane == r + s, jnp.sum(p, axis=-1, keepdims=True), acc)
        a_rows.append(jnp.where(lane - r <= trow, acc, 0.0))
    return a_rows[0] if len(a_rows) == 1 else jnp.concatenate(a_rows, axis=0)


def _hgrn_inter_block_factors(q, bk, b2, chunk):
    nb = chunk // HG_BLOCK
    zero = jnp.zeros((HG_BLOCK, HEAD_W), F32)
    q_rows, k_rows = [], []
    for i in range(nb):
        blk = slice(i * HG_BLOCK, (i + 1) * HG_BLOCK)
        q_tiles, k_tiles = [], []
        for j in range(nb - 1):
            rho = b2[(j + 1) * HG_BLOCK - 1:(j + 1) * HG_BLOCK]
            q_tiles.append(q[blk] * jnp.exp2(b2[blk] - rho) if j < i else zero)
            k_tiles.append(jnp.exp2(rho - bk[blk]) if j == i else zero)
        q_rows.append(jnp.concatenate(q_tiles, axis=1))
        k_rows.append(jnp.concatenate(k_tiles, axis=1))
    return jnp.concatenate(q_rows, axis=0).astype(BF16), jnp.concatenate(k_rows, axis=0).astype(BF16)


def _hgrn_tile(load, store, lb, gain, sts, valid_fn, tril_bf, chunk, n_c):
    units = [(s, c) for s in range(len(sts)) for c in range(n_c)]
    chunks = range(len(units))
    q, v, v_t, lk, g_hi, g_lo = [], [], [], [], [], []
    for s, c in units:
        qc, zf, vc = load(s, c)
        f = lb + (1.0 - lb) * jax.nn.sigmoid(zf)
        g = jnp.log(f)
        k = 1.0 - f
        valid = valid_fn(c)
        if valid is not None:
            g = jnp.where(valid, g, 0.0)
            k = jnp.where(valid, k, 0.0)
        hi = g.astype(BF16)
        q.append(qc)
        v.append(vc.astype(BF16))
        v_t.append(vc.T.astype(BF16))
        lk.append(jnp.log2(k))
        g_hi.append(hi)
        g_lo.append((g - hi.astype(F32)).astype(BF16))
    b2 = [(_dot(tril_bf, g_hi[c]) + _dot(tril_bf, g_lo[c])) * LOG2E for c in chunks]
    bk = [b2[c] - lk[c] for c in chunks]
    bl2 = [b[chunk - 1:chunk, :] for b in b2]
    upd = [_dot(v_t[c], jnp.exp2(bl2[c] - bk[c]).astype(BF16)) for c in chunks]
    a = [_hgrn_intra_diag(q[c], bk[c], b2[c], chunk) for c in chunks]
    if chunk > HG_BLOCK:
        factors = [_hgrn_inter_block_factors(q[c], bk[c], b2[c], chunk) for c in chunks]
        a = [a[c] + _dot_nt(factors[c][0], factors[c][1]) for c in chunks]
    o = [_dot(a[c].astype(BF16), v[c]) for c in chunks]
    sts = list(sts)
    states = []
    for u, (s, _) in enumerate(units):
        states.append(sts[s].astype(BF16))
        sts[s] = sts[s] * jnp.exp2(bl2[u]) + upd[u]
    for u, (s, c) in enumerate(units):
        oc = o[u] + _dot_nt((q[u] * jnp.exp2(b2[u])).astype(BF16), states[u])
        store(s, c, _rms(oc, gain).astype(BF16))
    return sts


def _hgrn_kernel(*refs, sb, tt, chunk, t_valid, fill_slots):
    q_ref, f_ref, i_ref, lb_ref, g_ref, s0_ref = refs[:6]
    y_ref, sout_ref, st_ref = refs[-3:]
    tstep = pl.program_id(2)
    n_t = pl.num_programs(2)

    @pl.when(tstep == 0)
    def _():
        for s in range(sb):
            st_ref[s] = s0_ref[s, 0].T

    lb = lb_ref[0]
    gain = g_ref[0]
    row = lax.broadcasted_iota(jnp.int32, (chunk, chunk), 0)
    col = lax.broadcasted_iota(jnp.int32, (chunk, chunk), 1)
    tril_bf = (row >= col).astype(BF16)
    scale = HEAD_W ** -0.5

    def load(s, c):
        rows = pl.ds(c * chunk, chunk)
        return (q_ref[s, rows, :].astype(F32) * scale, f_ref[s, rows, :].astype(F32),
                i_ref[s, rows, :].astype(F32))

    def store(s, c, y):
        y_ref[s, pl.ds(c * chunk, chunk), :] = y

    def valid_fn(c):
        if t_valid is None:
            return None
        return tstep * tt + c * chunk + lax.broadcasted_iota(jnp.int32, (chunk, 1), 0) < t_valid

    new_states = _hgrn_tile(load, store, lb, gain, [st_ref[s] for s in range(sb)], valid_fn, tril_bf, chunk,
                            tt // chunk)
    for s in range(sb):
        st_ref[s] = new_states[s]

    @pl.when(tstep == n_t - 1)
    def _():
        for s in range(sb):
            final = st_ref[s].T
            if fill_slots is None:
                sout_ref[s, 0] = final
            else:
                for slot in range(fill_slots):
                    sout_ref[slot, s, 0] = final


def _hgrn(z3d, lb, gain, s0, s0_layer, layer, sb, tt, chunk, t_valid, depth, stacked_prev):
    b, t, _ = z3d.shape
    cb = COL_HG // HEAD_W
    first = stacked_prev is None
    assert first == (layer == 0)
    kern = functools.partial(_hgrn_kernel, sb=sb, tt=tt, chunk=chunk, t_valid=t_valid,
                             fill_slots=depth if first else None)
    state_spec = (pl.BlockSpec((depth, sb, 1, HEAD_W, HEAD_W), lambda i, h, k: (0, i, h, 0, 0)) if first else
                  pl.BlockSpec((None, sb, 1, HEAD_W, HEAD_W), lambda i, h, k: (layer, i, h, 0, 0)))
    in_specs = [pl.BlockSpec((sb, tt, HEAD_W), lambda i, h, k: (i, k, cb + h)),
                pl.BlockSpec((sb, tt, HEAD_W), lambda i, h, k: (i, k, cb + N_HEADS + h)),
                pl.BlockSpec((sb, tt, HEAD_W), lambda i, h, k: (i, k, cb + 2 * N_HEADS + h)),
                pl.BlockSpec((1, 1, HEAD_W), lambda i, h, k: (h, 0, 0)),
                pl.BlockSpec((1, 1, HEAD_W), lambda i, h, k: (h, 0, 0)),
                pl.BlockSpec((None, sb, 1, HEAD_W, HEAD_W), lambda i, h, k: (s0_layer, i, h, 0, 0))]
    args = [z3d, z3d, z3d, lb.reshape(N_HEADS, 1, HEAD_W), gain.reshape(N_HEADS, 1, HEAD_W), s0]
    aliases = {}
    if stacked_prev is not None:
        in_specs.append(pl.BlockSpec(memory_space=pl.ANY))
        args.append(stacked_prev)
        aliases = {len(args) - 1: 1}
    return pl.pallas_call(
        kern,
        grid=(b // sb, N_HEADS, t // tt),
        in_specs=in_specs,
        out_specs=[pl.BlockSpec((sb, tt, HEAD_W), lambda i, h, k: (i, k, h)),
                   state_spec],
        out_shape=[jax.ShapeDtypeStruct((b, t, N_HEADS * HEAD_W), BF16),
                   jax.ShapeDtypeStruct((depth, b, N_HEADS, HEAD_W, HEAD_W), F32)],
        scratch_shapes=[pltpu.VMEM((sb, HEAD_W, HEAD_W), F32)],
        input_output_aliases=aliases,
        compiler_params=_cparams(("arbitrary", "arbitrary", "arbitrary")),
        name="hgrn",
    )(*args)


def _mla_prep_kernel(z_ref, cos_ref, sin_ref, gq_ref, wuq_ref, gkv_ref, wuk_ref, wabs_ref, grope_ref,
                     indq_ref, indk_ref, c_ref, r_ref, kv_ref, rk_ref, qp_ref):
    z = z_ref[...].astype(F32)
    ql = z[:, 0:MLA_Q_RANK]
    kvl = z[:, 256:384]
    kpe = z[:, 384:512]
    cos = cos_ref[...]
    sin = sin_ref[...]
    lane = lax.broadcasted_iota(jnp.int32, (1, 128), 1)

    qf = _dot(_rms(ql, gq_ref[...]).astype(BF16), wuq_ref[...])
    q_nope = qf[:, 0:256]
    rq_in = qf[:, 256:384]
    rot_q = rq_in * cos + pltpu.roll(rq_in, 64, axis=1) * sin
    rot_k = kpe * cos + pltpu.roll(kpe, 64, axis=1) * sin

    c = _rms(kvl, gkv_ref[...])
    c_ref[...] = c
    r32 = jnp.where(lane < 16, rot_k, pltpu.roll(rot_k, 80, axis=1))
    r32 = jnp.where(lane < MLA_ROPE, r32, 0.0)
    r_ref[...] = r32[:, 0:MLA_ROPE]
    cb = c.astype(BF16)
    kv_ref[...] = jnp.concatenate([cb, r32.astype(BF16)], axis=1)

    kn = _dot(cb, wuk_ref[...])
    kcat2 = jnp.concatenate([kn * kn, rot_k * rot_k], axis=1).astype(BF16)
    ssk = _dot_nt(indk_ref[...], kcat2)
    rk_ref[...] = lax.rsqrt(ssk * (1.0 / MLA_QK) + EPS)

    qcat2 = jnp.concatenate([q_nope * q_nope, rot_q * rot_q], axis=1).astype(BF16)
    ssq = _dot(qcat2, indq_ref[...])
    rq = lax.rsqrt(ssq * (1.0 / MLA_QK) + EPS) * (MLA_QK ** -0.5 * LOG2E)

    q_abs = _dot(q_nope.astype(BF16), wabs_ref[...])
    grope = grope_ref[...]
    half = MLA_ROPE // 2
    rot_q_hi = pltpu.roll(rot_q, 80, axis=1)
    for h in range(N_HEADS):
        lo = half * h
        pair = jnp.where((lane >= lo) & (lane < lo + half), rot_q,
                         jnp.where((lane >= lo + half) & (lane < lo + 2 * half), rot_q_hi, 0.0))
        qr = (pair if h == 0 else pltpu.roll(pair, 128 - lo, axis=1)) * grope
        rq_h = rq[:, h * 128:(h + 1) * 128]
        qp_ref[h, :, 0:128] = (q_abs[:, h * 128:(h + 1) * 128] * rq_h).astype(BF16)
        qp_ref[h, :, 128:256] = (qr * rq_h).astype(BF16)


def _mla_prep(z2d, cos_t, sin_t, pw, tm):
    n = z2d.shape[0]
    n_tab = cos_t.shape[0] // tm
    full = lambda shape: pl.BlockSpec(shape, lambda i: (0,) * len(shape))
    return pl.pallas_call(
        _mla_prep_kernel,
        grid=(n // tm,),
        in_specs=[pl.BlockSpec((tm, 512), lambda i: (i, COL_MLA // 512)),
                  pl.BlockSpec((tm, 128), lambda i: (i % n_tab, 0)),
                  pl.BlockSpec((tm, 128), lambda i: (i % n_tab, 0)),
                  full((1, MLA_Q_RANK)), full((MLA_Q_RANK, 384)), full((1, MLA_KV_RANK)),
                  full((MLA_KV_RANK, 256)), full((256, 512)), full((1, 128)),
                  full((384, N_HEADS * 128)), full((8, 384))],
        out_specs=[pl.BlockSpec((tm, MLA_KV_RANK), lambda i: (i, 0)),
                   pl.BlockSpec((tm, MLA_ROPE), lambda i: (i, 0)),
                   pl.BlockSpec((tm, 256), lambda i: (i, 0)),
                   pl.BlockSpec((8, tm), lambda i: (0, i)),
                   pl.BlockSpec((N_HEADS, tm, 256), lambda i: (0, i, 0))],
        out_shape=[jax.ShapeDtypeStruct((n, MLA_KV_RANK), F32),
                   jax.ShapeDtypeStruct((n, MLA_ROPE), F32),
                   jax.ShapeDtypeStruct((n, 256), BF16),
                   jax.ShapeDtypeStruct((8, n), F32),
                   jax.ShapeDtypeStruct((N_HEADS, n, 256), BF16)],
        compiler_params=_cparams(("arbitrary",)),
        name="mla_prep",
    )(z2d, cos_t, sin_t, pw["gq_norm"], pw["wuq"], pw["gkv_norm"], pw["wuk"], pw["wabs"], pw["grope"],
      pw["indq"], pw["indk"])


def _mla_prompt_kernel(qp_ref, kv_ref, rk_ref, wuv_ref, y_ref, m_ref, l_ref, acc_ref, *, tq, rb):
    i = pl.program_id(1)
    nrow = N_HEADS * tq
    m_ref[...] = jnp.full((nrow, 128), NEG, F32)
    l_ref[...] = jnp.zeros((nrow, 128), F32)
    acc_ref[...] = jnp.zeros((nrow, MLA_KV_RANK), F32)
    row = lax.broadcasted_iota(jnp.int32, (rb, 128), 0)
    col = lax.broadcasted_iota(jnp.int32, (rb, 128), 1)
    def block(k0, diagonal):
        kv = kv_ref[pl.ds(k0, tq), :]
        cv = kv[:, 0:MLA_KV_RANK]
        units = [(h, 0, tq, tq) for h in range(N_HEADS)]

        def qk(unit):
            h, r0, nr, nk = unit
            return _dot_nt(qp_ref[h, pl.ds(r0, nr), :], kv[0:nk])

        def softmax_update(unit, s_u):
            h, r0, nr, nk = unit
            p_rows = []
            for r in range(0, nr, rb):
                rows = pl.ds(h * tq + r0 + r, rb)
                tiles = []
                for c in range(0, nk, 128):
                    s = s_u[r:r + rb, c:c + 128] * rk_ref[h:h + 1, pl.ds(k0 + c, 128)]
                    if diagonal:
                        s = jnp.where(col + c <= row + (r0 + r), s, NEG)
                    tiles.append(s)
                m_old = m_ref[rows, :]
                m_new = jnp.maximum(m_old, jnp.max(functools.reduce(jnp.maximum, tiles), axis=-1, keepdims=True))
                alpha = jnp.exp2(m_old - m_new)
                probs = [jnp.exp2(s - m_new) for s in tiles]
                l_ref[rows, :] = alpha * l_ref[rows, :] + jnp.sum(functools.reduce(jnp.add, probs), axis=-1,
                                                                  keepdims=True)
                m_ref[rows, :] = m_new
                acc_ref[rows, :] = alpha * acc_ref[rows, :]
                p_rows.append(jnp.concatenate([p.astype(BF16) for p in probs], axis=1))
            return jnp.concatenate(p_rows, axis=0)

        def pv(unit, p_u):
            h, r0, nr, nk = unit
            acc_ref[pl.ds(h * tq + r0, nr), :] += _dot(p_u, cv[0:nk])

        n_u = len(units)
        scores = {0: qk(units[0])}
        probs = {}
        for u in range(n_u):
            if u + 1 < n_u:
                scores[u + 1] = qk(units[u + 1])
            probs[u] = softmax_update(units[u], scores.pop(u))
            if u >= 1:
                pv(units[u - 1], probs.pop(u - 1))
        pv(units[n_u - 1], probs.pop(n_u - 1))

    def step(j, carry):
        block(pl.multiple_of(j * tq, tq), False)
        return carry

    lax.fori_loop(0, i, step, 0)
    block(pl.multiple_of(i * tq, tq), True)
    for h in range(N_HEADS):
        rows = pl.ds(h * tq, tq)
        o = (acc_ref[rows, :] / l_ref[rows, :]).astype(BF16)
        y_ref[:, h * HEAD_W:(h + 1) * HEAD_W] = _dot(o, wuv_ref[:, h * HEAD_W:(h + 1) * HEAD_W]).astype(BF16)


def _mla_prompt(qp, kv, rk, wuv, b, t, tq):
    n = b * t
    nq = t // tq
    nrow = N_HEADS * tq
    return pl.pallas_call(
        functools.partial(_mla_prompt_kernel, tq=tq, rb=min(tq, MLA_ROW_GROUP)),
        grid=(b, nq),
        scratch_shapes=[pltpu.VMEM((nrow, 128), F32), pltpu.VMEM((nrow, 128), F32),
                        pltpu.VMEM((nrow, MLA_KV_RANK), F32)],
        in_specs=[pl.BlockSpec((N_HEADS, tq, 256), lambda bi, i: (0, bi * nq + i, 0)),
                  pl.BlockSpec((t, 256), lambda bi, i: (bi, 0)),
                  pl.BlockSpec((8, t), lambda bi, i: (0, bi)),
                  pl.BlockSpec((MLA_KV_RANK, N_HEADS * HEAD_W), lambda bi, i: (0, 0))],
        out_specs=pl.BlockSpec((tq, N_HEADS * HEAD_W), lambda bi, i: (bi * nq + i, 0)),
        out_shape=jax.ShapeDtypeStruct((n, N_HEADS * HEAD_W), BF16),
        compiler_params=_cparams(("arbitrary", "arbitrary")),
        name="mla_prompt",
    )(qp, kv, rk, wuv)


def _mla_sample_kernel(pt_ref, qp_ref, cnew_ref, rnewt_ref, wukt_ref, wuv_ref, lat_hbm, ropet_hbm,
                       y_ref, cbuf, rbuf, cbf, s_scr, part_scr, sem, *, layer, n_pages, ck, tv):
    b = pl.program_id(0)
    n_b = pl.num_programs(0)
    past = n_pages * PAGE
    tk = past + PAGE
    slot = b % 2
    tp = SAMPLE_PAD_T
    nrow = N_HEADS * tv
    npart = N_HEADS * 8

    def lat_copy(pg, p, sl):
        return pltpu.make_async_copy(lat_hbm.at[layer, pg], cbuf.at[sl, pl.ds(p * PAGE, PAGE), :], sem.at[0, sl])

    def rope_copy(pg, p, sl):
        return pltpu.make_async_copy(ropet_hbm.at[layer, pg], rbuf.at[sl, :, pl.ds(p * PAGE, PAGE)], sem.at[1, sl])

    def issue(seq, sl):
        def body(p, carry):
            pg = pt_ref[seq * n_pages + p]
            lat_copy(pg, p, sl).start()
            rope_copy(pg, p, sl).start()
            return carry
        lax.fori_loop(0, n_pages, body, 0, unroll=4)

    @pl.when(b == 0)
    def _():
        for sl in range(2):
            cbuf[sl, pl.ds(past, PAGE), :] = jnp.zeros((PAGE, MLA_KV_RANK), F32)
        issue(0, 0)

    def wait_all(sl):
        def wait_body(p, carry):
            lat_copy(0, p, sl).wait()
            rope_copy(0, p, sl).wait()
            return carry
        lax.fori_loop(0, n_pages, wait_body, 0, unroll=4)

    wait_all(slot)
    nxt = jnp.minimum(b + 1, n_b - 1)
    n_chunks = tk // ck
    pages_per_chunk = -(-n_pages // n_chunks)

    def prefetch(chunk_idx):
        for p in range(chunk_idx * pages_per_chunk, min((chunk_idx + 1) * pages_per_chunk, n_pages)):
            pg = pt_ref[nxt * n_pages + p]
            lat_copy(pg, p, 1 - slot).start()
            rope_copy(pg, p, 1 - slot).start()

    cbuf[slot, pl.ds(past, tp), :] = cnew_ref[0]
    rbuf[slot, :, pl.ds(past, PAGE)] = rnewt_ref[0]

    q = qp_ref[0]
    qr = q[:, MLA_KV_RANK:MLA_KV_RANK + MLA_ROPE]
    w_stack = jnp.concatenate([wukt_ref[...], q[:, 0:MLA_KV_RANK]], axis=0)
    n_up = N_HEADS * MLA_NOPE
    qpos = past + lax.broadcasted_iota(jnp.int32, (nrow, 1), 0) % tv
    head_ones = (lax.broadcasted_iota(jnp.int32, (nrow, npart), 0) // tv
                 == lax.broadcasted_iota(jnp.int32, (nrow, npart), 1) // 8).astype(BF16)

    for k0 in range(0, tk, ck):
        prefetch(k0 // ck)
        cb = cbuf[slot, pl.ds(k0, ck), :].astype(BF16)
        cbf[pl.ds(k0, ck), :] = cb
        rt = rbuf[slot, :, pl.ds(k0, ck)]
        big = _dot_nt(w_stack, cb)
        kn2 = big[0:n_up] * big[0:n_up]
        part_r = jnp.sum((rt * rt).reshape(MLA_ROPE // 8, 8, ck), axis=0)
        parts = [jnp.sum(kn2[h * MLA_NOPE:(h + 1) * MLA_NOPE].reshape(MLA_NOPE // 8, 8, ck), axis=0) + part_r
                 for h in range(N_HEADS)]
        part_scr[:, pl.ds(k0, ck)] = jnp.concatenate(parts, axis=0).astype(BF16)
        s_scr[:, pl.ds(k0, ck)] = big[n_up:n_up + nrow]

    rk = lax.rsqrt(_dot(head_ones, part_scr[...]) * (1.0 / MLA_QK) + EPS)
    s = (s_scr[...] + _dot(qr, rbuf[slot].astype(BF16))) * rk
    kpos = lax.broadcasted_iota(jnp.int32, (1, tk), 1)
    s = jnp.where(kpos <= qpos, s, NEG)
    m = jnp.max(s, axis=-1, keepdims=True)
    p = jnp.exp2(s - m)
    l = jnp.sum(p, axis=-1, keepdims=True)
    o = _dot(p.astype(BF16), cbf[...]) / l
    full = _dot(o.astype(BF16), wuv_ref[...])
    lane_head = lax.broadcasted_iota(jnp.int32, (tp, N_HEADS * HEAD_W), 1) // HEAD_W
    token = lax.broadcasted_iota(jnp.int32, (tp, N_HEADS * HEAD_W), 0)
    y = jnp.zeros((tp, N_HEADS * HEAD_W), F32)
    for h in range(N_HEADS):
        tile = full[(h * tv) // tp * tp:(h * tv) // tp * tp + tp]
        if (h * tv) % tp:
            tile = pltpu.roll(tile, tp - (h * tv) % tp, axis=0)
        y = jnp.where(lane_head == h, tile, y)
    y_ref[0] = jnp.where(token < tv, y, 0.0).astype(BF16)

    @pl.when(b == n_b - 1)
    def _():
        wait_all(1 - slot)


def _mla_sample(page_flat, qp3, c3, rt3, wukt, wuv, lat, ropet, layer, n_pages):
    bs, nrow, _ = qp3.shape
    tp = SAMPLE_PAD_T
    tv = nrow // N_HEADS
    assert tp % tv == 0 and nrow % 8 == 0
    tk = n_pages * PAGE + PAGE
    ck = max(d for d in range(PAGE, MLA_KEY_CHUNK + 1, PAGE) if tk % d == 0)
    kern = functools.partial(_mla_sample_kernel, layer=layer, n_pages=n_pages, ck=ck, tv=tv)
    grid_spec = pltpu.PrefetchScalarGridSpec(
        num_scalar_prefetch=1,
        grid=(bs,),
        in_specs=[pl.BlockSpec((1, nrow, 256), lambda b, pt: (b, 0, 0)),
                  pl.BlockSpec((1, tp, MLA_KV_RANK), lambda b, pt: (b, 0, 0)),
                  pl.BlockSpec((1, MLA_ROPE, PAGE), lambda b, pt: (b, 0, 0)),
                  pl.BlockSpec((N_HEADS * MLA_NOPE, MLA_KV_RANK), lambda b, pt: (0, 0)),
                  pl.BlockSpec((MLA_KV_RANK, N_HEADS * HEAD_W), lambda b, pt: (0, 0)),
                  pl.BlockSpec(memory_space=pl.ANY),
                  pl.BlockSpec(memory_space=pl.ANY)],
        out_specs=pl.BlockSpec((1, tp, N_HEADS * HEAD_W), lambda b, pt: (b, 0, 0)),
        scratch_shapes=[pltpu.VMEM((2, tk, MLA_KV_RANK), F32),
                        pltpu.VMEM((2, MLA_ROPE, tk), F32),
                        pltpu.VMEM((tk, MLA_KV_RANK), BF16),
                        pltpu.VMEM((nrow, tk), F32),
                        pltpu.VMEM((N_HEADS * 8, tk), BF16),
                        pltpu.SemaphoreType.DMA((2, 2))],
    )
    return pl.pallas_call(
        kern,
        grid_spec=grid_spec,
        out_shape=jax.ShapeDtypeStruct((bs, tp, N_HEADS * HEAD_W), BF16),
        compiler_params=_cparams(("arbitrary",)),
        name="mla_sample",
    )(page_flat, qp3, c3, rt3, wukt, wuv, lat, ropet)


def _memkv_kernel(x_ref, g_ref, wk_ref, wv_ref, gk_ref, k_ref, v_ref):
    mn = _rms(x_ref[...], g_ref[...]).astype(BF16)
    k = _dot(mn, wk_ref[...])
    gk = gk_ref[...]
    tm = x_ref.shape[0]
    v = _dot(mn, wv_ref[...])
    for h in range(N_HEADS):
        sl = slice(h * HEAD_W, (h + 1) * HEAD_W)
        k_ref[pl.ds(h, tm, stride=N_HEADS), :] = _rms(k[:, sl], gk)
        v_ref[pl.ds(h, tm, stride=N_HEADS), :] = v[:, sl]


def _memkv(mem2d, g, wk, wv, gk, tm):
    n, d = mem2d.shape
    br = wk.shape[1]
    return pl.pallas_call(
        _memkv_kernel,
        grid=(n // tm,),
        in_specs=[pl.BlockSpec((tm, d), lambda i: (i, 0)),
                  pl.BlockSpec((1, d), lambda i: (0, 0)),
                  pl.BlockSpec((d, br), lambda i: (0, 0)),
                  pl.BlockSpec((d, br), lambda i: (0, 0)),
                  pl.BlockSpec((1, HEAD_W), lambda i: (0, 0))],
        out_specs=[pl.BlockSpec((tm * N_HEADS, HEAD_W), lambda i: (i, 0)),
                   pl.BlockSpec((tm * N_HEADS, HEAD_W), lambda i: (i, 0))],
        out_shape=[jax.ShapeDtypeStruct((n * N_HEADS, HEAD_W), F32),
                   jax.ShapeDtypeStruct((n * N_HEADS, HEAD_W), F32)],
        compiler_params=_cparams(("arbitrary",)),
        name="memkv",
    )(mem2d, g.reshape(1, d), wk, wv, gk.reshape(1, HEAD_W))


def _mem_attn_kernel(q_ref, k_ref, v_ref, gq_ref, y_ref, *, sb, m):
    gq = gq_ref[...]
    scale = HEAD_W ** -0.5
    heads = range(N_HEADS)
    for s in range(sb):
        q = q_ref[s].astype(F32)
        qh = [(_rms(q[:, h * HEAD_W:(h + 1) * HEAD_W], gq) * (scale * LOG2E)).astype(BF16) for h in heads]
        sc = [_dot_nt(qh[h], k_ref[s, pl.ds(h, m, stride=N_HEADS), :].astype(BF16)) for h in heads]
        ps = [jnp.exp2(sc[h] - jnp.max(sc[h], axis=-1, keepdims=True)).astype(BF16) for h in heads]
        ones = jnp.ones((m, HEAD_W), BF16)
        for h in heads:
            v_ext = jnp.concatenate([v_ref[s, pl.ds(h, m, stride=N_HEADS), :].astype(BF16), ones], axis=1)
            o = _dot(ps[h], v_ext)
            y_ref[s, :, h * HEAD_W:(h + 1) * HEAD_W] = (o[:, 0:HEAD_W] / o[:, HEAD_W:2 * HEAD_W]).astype(BF16)


def _mem_attn(z3d, k4, v4, gq, layer, sb, tq):
    b, t, _ = z3d.shape
    m = k4.shape[2] // N_HEADS
    br = N_HEADS * HEAD_W
    return pl.pallas_call(
        functools.partial(_mem_attn_kernel, sb=sb, m=m),
        grid=(b // sb, t // tq),
        in_specs=[pl.BlockSpec((sb, tq, br), lambda i, j: (i, j, COL_MEMQ // br)),
                  pl.BlockSpec((None, sb, m * N_HEADS, HEAD_W), lambda i, j: (layer, i, 0, 0)),
                  pl.BlockSpec((None, sb, m * N_HEADS, HEAD_W), lambda i, j: (layer, i, 0, 0)),
                  pl.BlockSpec((1, HEAD_W), lambda i, j: (0, 0))],
        out_specs=pl.BlockSpec((sb, tq, br), lambda i, j: (i, j, 0)),
        out_shape=jax.ShapeDtypeStruct((b, t, br), BF16),
        compiler_params=_cparams(("arbitrary", "arbitrary")),
        name="mem_attn",
    )(z3d, k4, v4, gq.reshape(1, HEAD_W))


def _outproj_kernel(x_ref, mg_ref, sg_ref, y0_ref, y1_ref, y2_ref, y3_ref, wb_ref, wo_ref, o_ref):
    d = x_ref.shape[1]
    br = y0_ref.shape[1]
    merged = None
    for n, y_ref in enumerate((y0_ref, y1_ref, y2_ref, y3_ref)):
        sg = sg_ref[:, n * br:(n + 1) * br].astype(F32)
        ys = (y_ref[...].astype(F32) * (sg * (1.0 + jnp.tanh(sg)))).astype(BF16)
        proj = _dot(ys, wb_ref[n])
        term = (1.0 + jnp.tanh(mg_ref[:, n * d:(n + 1) * d].astype(F32))) * proj
        merged = term if merged is None else merged + term
    o_ref[...] = x_ref[...] + _dot(merged.astype(BF16), wo_ref[...])


def _outproj(x2d, z2d, ys, wb, wo, tm):
    n, d = x2d.shape
    br = ys[0].shape[1]
    nb = wb.shape[0]
    yspec = pl.BlockSpec((tm, br), lambda i: (i, 0))
    return pl.pallas_call(
        _outproj_kernel,
        grid=(n // tm,),
        in_specs=[pl.BlockSpec((tm, d), lambda i: (i, 0)),
                  pl.BlockSpec((tm, nb * d), lambda i: (i, COL_MG // (nb * d))),
                  pl.BlockSpec((tm, nb * br), lambda i: (i, COL_SG // (nb * br))),
                  yspec, yspec, yspec, yspec,
                  pl.BlockSpec((nb, br, d), lambda i: (0, 0, 0)),
                  pl.BlockSpec((d, d), lambda i: (0, 0))],
        out_specs=pl.BlockSpec((tm, d), lambda i: (i, 0)),
        out_shape=jax.ShapeDtypeStruct((n, d), F32),
        compiler_params=_cparams(("arbitrary",)),
        name="outproj",
    )(x2d, z2d, z2d, *ys, wb, wo)


def _rope_tables(pos):
    half = MLA_ROPE // 2
    freqs = ROPE_THETA ** (-np.arange(half, dtype=np.float64) / half)
    ang = np.asarray(pos, np.float64)[:, None] * freqs[None, :]
    cos = np.tile(np.cos(ang), (1, 8))
    sin = np.tile(np.sin(ang), (1, 8))
    sin[:, :64] *= -1.0
    return jnp.asarray(cos, F32), jnp.asarray(sin, F32)


def _mla_params(q_norm, w_uq, kv_norm, w_uk, gq, gk):
    half = MLA_ROPE // 2
    wq = w_uq.reshape(MLA_Q_RANK, N_HEADS, MLA_QK)
    wuq = jnp.concatenate([wq[:, :, :MLA_NOPE].reshape(MLA_Q_RANK, -1),
                           wq[:, :, MLA_NOPE:MLA_NOPE + half].reshape(MLA_Q_RANK, -1),
                           wq[:, :, MLA_NOPE + half:].reshape(MLA_Q_RANK, -1)], axis=1).astype(BF16)
    g2 = gq * gk
    wk = w_uk.reshape(MLA_KV_RANK, N_HEADS, MLA_NOPE)
    wabs = jnp.zeros((N_HEADS * MLA_NOPE, N_HEADS * MLA_KV_RANK), F32)
    for h in range(N_HEADS):
        blk = (wk[:, h, :] * g2[None, :MLA_NOPE]).T
        wabs = wabs.at[h * MLA_NOPE:(h + 1) * MLA_NOPE, h * MLA_KV_RANK:(h + 1) * MLA_KV_RANK].set(blk)
    grope = jnp.zeros((1, 128), F32).at[0, :MLA_ROPE].set(g2[MLA_NOPE:])
    j = np.arange(384)
    head_of = np.where(j < 256, j // MLA_NOPE, (j % 64) // half)
    indq = (head_of[:, None] == (np.arange(N_HEADS * 128) // 128)[None, :]).astype(np.float32)
    lane = j - 256
    is_rope = (j >= 256) & ((lane < half) | ((lane >= 64) & (lane < 64 + half)))
    indk = np.zeros((8, 384), np.float32)
    for h in range(N_HEADS):
        indk[h] = ((j < 256) & (j // MLA_NOPE == h)) | is_rope
    return dict(gq_norm=q_norm.reshape(1, -1), wuq=wuq, gkv_norm=kv_norm.reshape(1, -1),
                wuk=w_uk.astype(BF16), wukt=w_uk.T.astype(BF16), wabs=wabs.astype(BF16), grope=grope,
                indq=jnp.asarray(indq, BF16), indk=jnp.asarray(indk, BF16))


def _hgrn_lower_bounds(lb_param):
    p = jax.nn.softmax(lb_param.astype(F32), axis=0)
    return jnp.cumsum(p, axis=0) - p[0]


def kernel(x_prompt, x_sample, mem_prompt, cache_mla_latent, cache_mla_rope, page_table, state_hgrn, state_conv, cache_mem_k, cache_mem_v, norm_gain, w_in, conv_w, hgrn_lb, hgrn_norm, mla_q_norm, mla_w_uq, mla_kv_norm, mla_w_uk, mla_w_uv, mla_q_gain, mla_k_gain, mem_norm, mem_w_k, mem_w_v, mem_q_gain, mem_k_gain, w_branch_out, w_out):
    bp, tp, d = x_prompt.shape
    bs, ts, _ = x_sample.shape
    depth = w_in.shape[0]
    br = conv_w.shape[2]
    mem_len = mem_prompt.shape[1]
    n_pages = page_table.shape[1]
    past = n_pages * cache_mla_latent.shape[2]
    tpad = SAMPLE_PAD_T
    n_p = bp * tp
    n_s = bs * tpad

    lbs = _hgrn_lower_bounds(hgrn_lb)
    cos_p, sin_p = _rope_tables(np.arange(tp))
    cos_s, sin_s = _rope_tables(past + np.arange(tpad))
    tm_s = min(n_s, 1024)
    cos_s = jnp.tile(cos_s, (tm_s // tpad, 1))
    sin_s = jnp.tile(sin_s, (tm_s // tpad, 1))
    page_flat = page_table.reshape(-1).astype(jnp.int32)
    rope_t = jnp.swapaxes(cache_mla_rope, 2, 3)
    mem_k4 = cache_mem_k.reshape(depth, bs, mem_len * N_HEADS, HEAD_W)
    mem_v4 = cache_mem_v.reshape(depth, bs, mem_len * N_HEADS, HEAD_W)

    tm_p = min(tp, 512)
    xp = x_prompt.reshape(n_p, d)
    xs = jnp.pad(x_sample, ((0, 0), (0, tpad - ts), (0, 0))).reshape(n_s, d)
    mem2d = mem_prompt.reshape(bp * mem_len, d)
    zero_state = jnp.zeros((1, bp, N_HEADS, HEAD_W, HEAD_W), F32)

    outs = {k: [] for k in ("p_lat", "p_rope", "p_conv", "p_mk", "p_mv", "s_lat", "s_rope", "s_conv")}
    p_hg = s_hg = None
    for l in range(depth):
        w_in_t = _relayout_w_in(w_in[l])
        pw = _mla_params(mla_q_norm[l], mla_w_uq[l], mla_kv_norm[l], mla_w_uk[l], mla_q_gain[l], mla_k_gain[l])
        wuv = mla_w_uv[l].astype(BF16)
        wb = (0.5 * w_branch_out[l]).astype(BF16)
        wo = w_out[l].astype(BF16)

        z = _inproj(xp, norm_gain[l], w_in_t, min(n_p, 1024), 2048)
        z3 = z.reshape(bp, tp, N_COLS)
        y_conv, tail = _conv(z, None, conv_w[l], tp, min(tp, CONV_TILE), 8)
        y_hg, p_hg = _hgrn(z3, lbs[l], hgrn_norm[l], zero_state, 0, l, 1, min(tp, HG_TILE), HG_CHUNK, None, depth,
                           p_hg)
        c, r, kv, rk, qp = _mla_prep(z, cos_p, sin_p, pw, min(tp, PREP_TILE))
        y_mla = _mla_prompt(qp, kv, rk, wuv, bp, tp, min(tp, MLA_TQ))
        mk, mv = _memkv(mem2d, mem_norm[l], mem_w_k[l].astype(BF16), mem_w_v[l].astype(BF16), mem_k_gain[l],
                        min(bp * mem_len, 512))
        rows_m = mem_len * N_HEADS
        y_mem = _mem_attn(z3, mk.reshape(1, bp, rows_m, HEAD_W), mv.reshape(1, bp, rows_m, HEAD_W), mem_q_gain[l],
                          0, 1, min(tp, MEM_TILE))
        xp = _outproj(xp, z, (y_conv, y_hg.reshape(n_p, br), y_mla, y_mem.reshape(n_p, br)), wb, wo, tm_p)
        outs["p_lat"].append(c.reshape(bp, tp, MLA_KV_RANK))
        outs["p_rope"].append(r.reshape(bp, tp, MLA_ROPE))
        outs["p_conv"].append(tail.reshape(bp, 8, br)[:, 8 - (CONV_K - 1):])
        outs["p_mk"].append(mk.reshape(bp, mem_len, N_HEADS, HEAD_W))
        outs["p_mv"].append(mv.reshape(bp, mem_len, N_HEADS, HEAD_W))

        z = _inproj(xs, norm_gain[l], w_in_t, tm_s, 2048)
        z3 = z.reshape(bs, tpad, N_COLS)
        hist = jnp.pad(state_conv[l], ((0, 0), (0, tpad - (CONV_K - 1)), (0, 0))).reshape(n_s, br)
        y_conv, u_all = _conv(z, hist, conv_w[l], tpad, tm_s, tm_s)
        sb = 8 if bs % 8 == 0 else 1
        sb_hg = HG_SAMPLE_SEQS if bs % HG_SAMPLE_SEQS == 0 else sb
        y_hg, s_hg = _hgrn(z3, lbs[l], hgrn_norm[l], state_hgrn, l, l, sb_hg, tpad, tpad, ts, depth, s_hg)
        c, r, kv, rk, qp = _mla_prep(z, cos_s, sin_s, pw, tm_s)
        qp3 = qp.reshape(N_HEADS, bs, tpad, 256)[:, :, :ts].transpose(1, 0, 2, 3).reshape(bs, N_HEADS * ts, 256)
        rt3 = jnp.pad(r.reshape(bs, tpad, MLA_ROPE).transpose(0, 2, 1), ((0, 0), (0, 0), (0, PAGE - tpad)))
        y_mla = _mla_sample(page_flat, qp3, c.reshape(bs, tpad, MLA_KV_RANK), rt3, pw["wukt"], wuv,
                            cache_mla_latent, rope_t, l, n_pages)
        y_mem = _mem_attn(z3, mem_k4, mem_v4, mem_q_gain[l], l, sb, tpad)
        xs = _outproj(xs, z, (y_conv, y_hg.reshape(n_s, br), y_mla.reshape(n_s, br), y_mem.reshape(n_s, br)),
                      wb, wo, tm_s)
        outs["s_lat"].append(c.reshape(bs, tpad, MLA_KV_RANK)[:, :ts])
        outs["s_rope"].append(r.reshape(bs, tpad, MLA_ROPE)[:, :ts])
        outs["s_conv"].append(u_all.reshape(bs, tpad, br)[:, ts - (CONV_K - 1):ts])

    st = lambda k: jnp.stack(outs[k])
    return (xp.reshape(bp, tp, d), xs.reshape(bs, tpad, d)[:, :ts], st("p_lat"), st("p_rope"), p_hg,
            st("p_conv"), st("p_mk"), st("p_mv"), st("s_lat"), st("s_rope"), s_hg, st("s_conv"))
```

```python
import functools

import numpy as np
import jax
import jax.numpy as jnp
from jax import lax
from jax.experimental import pallas as pl
from jax.experimental.pallas import tpu as pltpu

F32 = jnp.float32
BF16 = jnp.bfloat16

N_HEADS = 4
HEAD_W = 128
MLA_NOPE = 64
MLA_ROPE = 32
MLA_QK = MLA_NOPE + MLA_ROPE
MLA_Q_RANK = 192
MLA_KV_RANK = 128
CONV_K = 3
ROPE_THETA = 10000.0
EPS = 1e-6
NEG = -1e30
PAGE = 128
SAMPLE_PAD_T = 8
HG_CHUNK = 64
HG_BLOCK = 8
HG_TILE = 2048
HG_SAMPLE_SEQS = 32
INPROJ_ROWS = 1024
INPROJ_COLS = 2560
MLA_ROW_GROUP = 128
MLA_TQ = 512
MLA_KEY_CHUNK = 1664
CONV_TILE = 2048
PREP_TILE = 1024
MEM_TILE = 2048
LOG2E = 1.4426950408889634
VMEM_LIMIT = 56 * 1024 * 1024

COL_CONV = 0
COL_HG = 1536
COL_MLA = 3072
COL_MEMQ = 3584
COL_MG = 4096
COL_SG = 8192
N_COLS = 10240


def _cparams(sem):
    return pltpu.CompilerParams(dimension_semantics=sem, vmem_limit_bytes=VMEM_LIMIT)


def _rms(x, g):
    return x * lax.rsqrt(jnp.mean(x * x, axis=-1, keepdims=True) + EPS) * g


def _dot(a, b):
    return jnp.dot(a, b, preferred_element_type=F32)


def _dot_nt(a, b):
    return lax.dot_general(a, b, (((1,), (1,)), ((), ())), preferred_element_type=F32)


def _dot_tn(a, b):
    return lax.dot_general(a, b, (((0,), (0,)), ((), ())), preferred_element_type=F32)


def _inproj_kernel(x_ref, g_ref, w_ref, z_ref, hn_ref):
    @pl.when(pl.program_id(1) == 0)
    def _():
        hn_ref[...] = _rms(x_ref[...], g_ref[...]).astype(BF16)

    z_ref[...] = _dot_nt(hn_ref[...], w_ref[...]).astype(BF16)


def _relayout_w_in(w):
    wt = jnp.swapaxes(w, 0, 1)
    d = w.shape[0]
    rows = lambda a, b, scale=None: (wt[a:b] if scale is None else wt[a:b] * scale).astype(BF16)
    zeros = lambda n: jnp.zeros((n, d), BF16)
    o_q, o_kv, o_pe, o_mq, o_sg, o_mg = 3072, 3264, 3392, 3424, 3936, 5984
    half = MLA_ROPE // 2
    return jnp.concatenate([rows(0, o_q),
                            rows(o_q, o_kv), zeros(64),
                            rows(o_kv, o_pe),
                            rows(o_pe, o_pe + half), zeros(48), rows(o_pe + half, o_mq), zeros(48),
                            rows(o_mq, o_sg),
                            rows(o_mg, w.shape[1], 0.5),
                            rows(o_sg, o_mg, 0.5)], axis=0)


def _inproj(x2d, g, w_t, tm, tn):
    n, d = x2d.shape
    ncol = w_t.shape[0]
    return pl.pallas_call(
        _inproj_kernel,
        grid=(n // tm, ncol // tn),
        in_specs=[pl.BlockSpec((tm, d), lambda i, j: (i, 0)),
                  pl.BlockSpec((1, d), lambda i, j: (0, 0)),
                  pl.BlockSpec((tn, d), lambda i, j: (j, 0))],
        out_specs=pl.BlockSpec((tm, tn), lambda i, j: (i, j)),
        out_shape=jax.ShapeDtypeStruct((n, ncol), BF16),
        scratch_shapes=[pltpu.VMEM((tm, d), BF16)],
        compiler_params=_cparams(("arbitrary", "arbitrary")),
        name="inproj",
    )(x2d, g.reshape(1, d), w_t)


def _conv_kernel(*refs, seq_t, tr, tail_rows, has_hist):
    if has_hist:
        z_ref, halo_ref, hist_ref, w_ref, y_ref, tail_ref = refs
    else:
        z_ref, halo_ref, w_ref, y_ref, tail_ref = refs
    i = pl.program_id(0)
    br = y_ref.shape[1]
    z = z_ref[...].astype(F32)
    u = z[:, 2 * br:3 * br] * z[:, 0:br]
    zh = halo_ref[...].astype(F32)
    uh = zh[:, 2 * br:3 * br] * zh[:, 0:br]
    u1 = pltpu.roll(u, 1, axis=0)
    u2 = pltpu.roll(u, 2, axis=0)
    if has_hist:
        loc = lax.broadcasted_iota(jnp.int32, (tr, 1), 0)
        t = loc % seq_t
        hp = hist_ref[...]
        u1 = jnp.where(t == 0, pltpu.roll(hp, tr - 1, axis=0), u1)
        u2 = jnp.where(t < 2, hp, u2)
    else:
        loc = lax.broadcasted_iota(jnp.int32, (8, 1), 0)
        inside = (i * tr) % seq_t != 0
        prev1 = jnp.where(inside, uh[7:8], 0.0)
        prev2 = jnp.where(inside, uh[6:7], 0.0)
        head1 = jnp.where(loc == 0, prev1, u1[0:8])
        head2 = jnp.where(loc == 0, prev2, jnp.where(loc == 1, prev1, u2[0:8]))
        u1 = jnp.concatenate([head1, u1[8:]], axis=0)
        u2 = jnp.concatenate([head2, u2[8:]], axis=0)
    w = w_ref[...]
    conv = w[0:1] * u2 + w[1:2] * u1 + w[2:3] * u
    y_ref[...] = (z[:, br:2 * br] * conv).astype(BF16)

    @pl.when(((i + 1) * tr) % max(seq_t, tr) == 0)
    def _():
        tail_ref[...] = u[tr - tail_rows:, :]


def _conv(z2d, hist_rows, w, seq_t, tr, tail_rows):
    n = z2d.shape[0]
    br = w.shape[1]
    has_hist = hist_rows is not None
    assert (n == tr and tr % seq_t == 0) if has_hist else seq_t % tr == 0
    group = max(seq_t, tr)
    n_tail = (n // group) * tail_rows
    in_specs = [pl.BlockSpec((tr, 3 * br), lambda i: (i, 0)),
                pl.BlockSpec((8, 3 * br), lambda i: (jnp.maximum(i * (tr // 8) - 1, 0), 0))]
    args = [z2d, z2d]
    if has_hist:
        in_specs.append(pl.BlockSpec((tr, br), lambda i: (i, 0)))
        args.append(hist_rows)
    in_specs.append(pl.BlockSpec((CONV_K, br), lambda i: (0, 0)))
    args.append(w)
    return pl.pallas_call(
        functools.partial(_conv_kernel, seq_t=seq_t, tr=tr, tail_rows=tail_rows, has_hist=has_hist),
        grid=(n // tr,),
        in_specs=in_specs,
        out_specs=[pl.BlockSpec((tr, br), lambda i: (i, 0)),
                   pl.BlockSpec((tail_rows, br), lambda i: ((i * tr) // group, 0))],
        out_shape=[jax.ShapeDtypeStruct((n, br), BF16), jax.ShapeDtypeStruct((n_tail, br), F32)],
        compiler_params=_cparams(("arbitrary",)),
        name="conv",
    )(*args)


def _hgrn_intra_diag(q, bk, b2, chunk):
    lane = lax.broadcasted_iota(jnp.int32, (HG_BLOCK, chunk), 1)
    trow = lax.broadcasted_iota(jnp.int32, (HG_BLOCK, chunk), 0)
    a_rows = []
    for r in range(0, chunk, HG_BLOCK):
        b_blk = b2[r:r + HG_BLOCK]
        q_blk = q[r:r + HG_BLOCK]
        acc = jnp.zeros((HG_BLOCK, chunk), F32)
        for s in range(HG_BLOCK):
            p = jnp.exp2(b_blk - bk[r + s:r + s + 1]) * q_blk
            acc = jnp.where(lane == r + s, jnp.sum(p, axis=-1, keepdims=True), acc)
        a_rows.append(jnp.where(lane - r <= trow, acc, 0.0))
    return a_rows[0] if len(a_rows) == 1 else jnp.concatenate(a_rows, axis=0)


def _hgrn_inter_block_factors(q, bk, b2, chunk):
    nb = chunk // HG_BLOCK
    zero = jnp.zeros((HG_BLOCK, HEAD_W), F32)
    q_rows, k_rows = [], []
    for i in range(nb):
        blk = slice(i * HG_BLOCK, (i + 1) * HG_BLOCK)
        q_tiles, k_tiles = [], []
        for j in range(nb - 1):
            rho = b2[(j + 1) * HG_BLOCK - 1:(j + 1) * HG_BLOCK]
            q_tiles.append(q[blk] * jnp.exp2(b2[blk] - rho) if j < i else zero)
            k_tiles.append(jnp.exp2(rho - bk[blk]) if j == i else zero)
        q_rows.append(jnp.concatenate(q_tiles, axis=1))
        k_rows.append(jnp.concatenate(k_tiles, axis=1))
    return jnp.concatenate(q_rows, axis=0).astype(BF16), jnp.concatenate(k_rows, axis=0).astype(BF16)


def _hgrn_tile(load, store, lb, gain, sts, valid_fn, tril_bf, chunk, n_c):
    units = [(s, c) for s in range(len(sts)) for c in range(n_c)]
    chunks = range(len(units))
    q, v, v_t, lk, g_hi, g_lo = [], [], [], [], [], []
    for s, c in units:
        qc, zf, vc = load(s, c)
        f = lb + (1.0 - lb) * jax.nn.sigmoid(zf)
        g = jnp.log(f)
        k = 1.0 - f
        valid = valid_fn(c)
        if valid is not None:
            g = jnp.where(valid, g, 0.0)
            k = jnp.where(valid, k, 0.0)
        hi = g.astype(BF16)
        q.append(qc)
        v.append(vc.astype(BF16))
        v_t.append(vc.T.astype(BF16))
        lk.append(jnp.log2(k))
        g_hi.append(hi)
        g_lo.append((g - hi.astype(F32)).astype(BF16))
    b2 = [(_dot(tril_bf, g_hi[c]) + _dot(tril_bf, g_lo[c])) * LOG2E for c in chunks]
    bk = [b2[c] - lk[c] for c in chunks]
    bl2 = [b[chunk - 1:chunk, :] for b in b2]
    upd = [_dot(v_t[c], jnp.exp2(bl2[c] - bk[c]).astype(BF16)) for c in chunks]
    a = [_hgrn_intra_diag(q[c], bk[c], b2[c], chunk) for c in chunks]
    if chunk > HG_BLOCK:
        factors = [_hgrn_inter_block_factors(q[c], bk[c], b2[c], chunk) for c in chunks]
        a = [a[c] + _dot_nt(factors[c][0], factors[c][1]) for c in chunks]
    o = [_dot(a[c].astype(BF16), v[c]) for c in chunks]
    sts = list(sts)
    states = []
    for u, (s, _) in enumerate(units):
        states.append(sts[s].astype(BF16))
        sts[s] = sts[s] * jnp.exp2(bl2[u]) + upd[u]
    for u, (s, c) in enumerate(units):
        oc = o[u] + _dot_nt((q[u] * jnp.exp2(b2[u])).astype(BF16), states[u])
        store(s, c, _rms(oc, gain).astype(BF16))
    return sts


def _hgrn_kernel(*refs, sb, tt, chunk, t_valid, fill_slots):
    q_ref, f_ref, i_ref, lb_ref, g_ref, s0_ref = refs[:6]
    y_ref, sout_ref, st_ref = refs[-3:]
    tstep = pl.program_id(2)
    n_t = pl.num_programs(2)

    @pl.when(tstep == 0)
    def _():
        for s in range(sb):
            st_ref[s] = s0_ref[s, 0].T

    lb = lb_ref[0]
    gain = g_ref[0]
    row = lax.broadcasted_iota(jnp.int32, (chunk, chunk), 0)
    col = lax.broadcasted_iota(jnp.int32, (chunk, chunk), 1)
    tril_bf = (row >= col).astype(BF16)
    scale = HEAD_W ** -0.5

    def load(s, c):
        rows = pl.ds(c * chunk, chunk)
        return (q_ref[s, rows, :].astype(F32) * scale, f_ref[s, rows, :].astype(F32),
                i_ref[s, rows, :].astype(F32))

    def store(s, c, y):
        y_ref[s, pl.ds(c * chunk, chunk), :] = y

    def valid_fn(c):
        if t_valid is None:
            return None
        return tstep * tt + c * chunk + lax.broadcasted_iota(jnp.int32, (chunk, 1), 0) < t_valid

    new_states = _hgrn_tile(load, store, lb, gain, [st_ref[s] for s in range(sb)], valid_fn, tril_bf, chunk,
                            tt // chunk)
    for s in range(sb):
        st_ref[s] = new_states[s]

    @pl.when(tstep == n_t - 1)
    def _():
        for s in range(sb):
            final = st_ref[s].T
            if fill_slots is None:
                sout_ref[s, 0] = final
            else:
                for slot in range(fill_slots):
                    sout_ref[slot, s, 0] = final


def _hgrn(z3d, lb, gain, s0, s0_layer, layer, sb, tt, chunk, t_valid, depth, stacked_prev):
    b, t, _ = z3d.shape
    cb = COL_HG // HEAD_W
    first = stacked_prev is None
    assert first == (layer == 0)
    kern = functools.partial(_hgrn_kernel, sb=sb, tt=tt, chunk=chunk, t_valid=t_valid,
                             fill_slots=depth if first else None)
    state_spec = (pl.BlockSpec((depth, sb, 1, HEAD_W, HEAD_W), lambda i, h, k: (0, i, h, 0, 0)) if first else
                  pl.BlockSpec((None, sb, 1, HEAD_W, HEAD_W), lambda i, h, k: (layer, i, h, 0, 0)))
    in_specs = [pl.BlockSpec((sb, tt, HEAD_W), lambda i, h, k: (i, k, cb + h)),
                pl.BlockSpec((sb, tt, HEAD_W), lambda i, h, k: (i, k, cb + N_HEADS + h)),
                pl.BlockSpec((sb, tt, HEAD_W), lambda i, h, k: (i, k, cb + 2 * N_HEADS + h)),
                pl.BlockSpec((1, 1, HEAD_W), lambda i, h, k: (h, 0, 0)),
                pl.BlockSpec((1, 1, HEAD_W), lambda i, h, k: (h, 0, 0)),
                pl.BlockSpec((None, sb, 1, HEAD_W, HEAD_W), lambda i, h, k: (s0_layer, i, h, 0, 0))]
    args = [z3d, z3d, z3d, lb.reshape(N_HEADS, 1, HEAD_W), gain.reshape(N_HEADS, 1, HEAD_W), s0]
    aliases = {}
    if stacked_prev is not None:
        in_specs.append(pl.BlockSpec(memory_space=pl.ANY))
        args.append(stacked_prev)
        aliases = {len(args) - 1: 1}
    return pl.pallas_call(
        kern,
        grid=(b // sb, N_HEADS, t // tt),
        in_specs=in_specs,
        out_specs=[pl.BlockSpec((sb, tt, HEAD_W), lambda i, h, k: (i, k, h)),
                   state_spec],
        out_shape=[jax.ShapeDtypeStruct((b, t, N_HEADS * HEAD_W), BF16),
                   jax.ShapeDtypeStruct((depth, b, N_HEADS, HEAD_W, HEAD_W), F32)],
        scratch_shapes=[pltpu.VMEM((sb, HEAD_W, HEAD_W), F32)],
        input_output_aliases=aliases,
        compiler_params=_cparams(("arbitrary", "arbitrary", "arbitrary")),
        name="hgrn",
    )(*args)


def _mla_prep_kernel(z_ref, cos_ref, sin_ref, gq_ref, wuq_ref, gkv_ref, wuk_ref, wabs_ref, grope_ref,
                     indq_ref, indk_ref, c_ref, r_ref, kv_ref, rk_ref, qp_ref):
    z = z_ref[...].astype(F32)
    ql = z[:, 0:MLA_Q_RANK]
    kvl = z[:, 256:384]
    kpe = z[:, 384:512]
    cos = cos_ref[...]
    sin = sin_ref[...]
    lane = lax.broadcasted_iota(jnp.int32, (1, 128), 1)

    qf = _dot(_rms(ql, gq_ref[...]).astype(BF16), wuq_ref[...])
    q_nope = qf[:, 0:256]
    rq_in = qf[:, 256:384]
    rot_q = rq_in * cos + pltpu.roll(rq_in, 64, axis=1) * sin
    rot_k = kpe * cos + pltpu.roll(kpe, 64, axis=1) * sin

    c = _rms(kvl, gkv_ref[...])
    c_ref[...] = c
    r32 = jnp.where(lane < 16, rot_k, pltpu.roll(rot_k, 80, axis=1))
    r32 = jnp.where(lane < MLA_ROPE, r32, 0.0)
    r_ref[...] = r32[:, 0:MLA_ROPE]
    cb = c.astype(BF16)
    kv_ref[...] = jnp.concatenate([cb, r32.astype(BF16)], axis=1)

    kn = _dot(cb, wuk_ref[...])
    kcat2 = jnp.concatenate([kn * kn, rot_k * rot_k], axis=1).astype(BF16)
    ssk = _dot_nt(indk_ref[...], kcat2)
    rk_ref[...] = lax.rsqrt(ssk * (1.0 / MLA_QK) + EPS)

    qcat2 = jnp.concatenate([q_nope * q_nope, rot_q * rot_q], axis=1).astype(BF16)
    ssq = _dot(qcat2, indq_ref[...])
    rq = lax.rsqrt(ssq * (1.0 / MLA_QK) + EPS) * (MLA_QK ** -0.5 * LOG2E)

    q_abs = _dot(q_nope.astype(BF16), wabs_ref[...])
    grope = grope_ref[...]
    half = MLA_ROPE // 2
    rot_q_hi = pltpu.roll(rot_q, 80, axis=1)
    for h in range(N_HEADS):
        lo = half * h
        pair = jnp.where((lane >= lo) & (lane < lo + half), rot_q,
                         jnp.where((lane >= lo + half) & (lane < lo + 2 * half), rot_q_hi, 0.0))
        qr = (pair if h == 0 else pltpu.roll(pair, 128 - lo, axis=1)) * grope
        rq_h = rq[:, h * 128:(h + 1) * 128]
        qp_ref[h, :, 0:128] = (q_abs[:, h * 128:(h + 1) * 128] * rq_h).astype(BF16)
        qp_ref[h, :, 128:256] = (qr * rq_h).astype(BF16)


def _mla_prep(z2d, cos_t, sin_t, pw, tm):
    n = z2d.shape[0]
    n_tab = cos_t.shape[0] // tm
    full = lambda shape: pl.BlockSpec(shape, lambda i: (0,) * len(shape))
    return pl.pallas_call(
        _mla_prep_kernel,
        grid=(n // tm,),
        in_specs=[pl.BlockSpec((tm, 512), lambda i: (i, COL_MLA // 512)),
                  pl.BlockSpec((tm, 128), lambda i: (i % n_tab, 0)),
                  pl.BlockSpec((tm, 128), lambda i: (i % n_tab, 0)),
                  full((1, MLA_Q_RANK)), full((MLA_Q_RANK, 384)), full((1, MLA_KV_RANK)),
                  full((MLA_KV_RANK, 256)), full((256, 512)), full((1, 128)),
                  full((384, N_HEADS * 128)), full((8, 384))],
        out_specs=[pl.BlockSpec((tm, MLA_KV_RANK), lambda i: (i, 0)),
                   pl.BlockSpec((tm, MLA_ROPE), lambda i: (i, 0)),
                   pl.BlockSpec((tm, 256), lambda i: (i, 0)),
                   pl.BlockSpec((8, tm), lambda i: (0, i)),
                   pl.BlockSpec((N_HEADS, tm, 256), lambda i: (0, i, 0))],
        out_shape=[jax.ShapeDtypeStruct((n, MLA_KV_RANK), F32),
                   jax.ShapeDtypeStruct((n, MLA_ROPE), F32),
                   jax.ShapeDtypeStruct((n, 256), BF16),
                   jax.ShapeDtypeStruct((8, n), F32),
                   jax.ShapeDtypeStruct((N_HEADS, n, 256), BF16)],
        compiler_params=_cparams(("arbitrary",)),
        name="mla_prep",
    )(z2d, cos_t, sin_t, pw["gq_norm"], pw["wuq"], pw["gkv_norm"], pw["wuk"], pw["wabs"], pw["grope"],
      pw["indq"], pw["indk"])


def _mla_prompt_kernel(qp_ref, kv_ref, rk_ref, wuv_ref, y_ref, m_ref, l_ref, acc_ref, *, tq, rb):
    i = pl.program_id(1)
    nrow = N_HEADS * tq
    m_ref[...] = jnp.full((nrow, 128), NEG, F32)
    l_ref[...] = jnp.zeros((nrow, 128), F32)
    acc_ref[...] = jnp.zeros((nrow, MLA_KV_RANK), F32)
    row = lax.broadcasted_iota(jnp.int32, (rb, 128), 0)
    col = lax.broadcasted_iota(jnp.int32, (rb, 128), 1)
    def block(k0, diagonal):
        kv = kv_ref[pl.ds(k0, tq), :]
        cv = kv[:, 0:MLA_KV_RANK]
        units = [(h, 0, tq, tq) for h in range(N_HEADS)]

        def qk(unit):
            h, r0, nr, nk = unit
            return _dot_nt(qp_ref[h, pl.ds(r0, nr), :], kv[0:nk])

        def softmax_update(unit, s_u):
            h, r0, nr, nk = unit
            p_rows = []
            for r in range(0, nr, rb):
                rows = pl.ds(h * tq + r0 + r, rb)
                tiles = []
                for c in range(0, nk, 128):
                    s = s_u[r:r + rb, c:c + 128] * rk_ref[h:h + 1, pl.ds(k0 + c, 128)]
                    if diagonal:
                        s = jnp.where(col + c <= row + (r0 + r), s, NEG)
                    tiles.append(s)
                m_old = m_ref[rows, :]
                m_new = jnp.maximum(m_old, jnp.max(functools.reduce(jnp.maximum, tiles), axis=-1, keepdims=True))
                alpha = jnp.exp2(m_old - m_new)
                probs = [jnp.exp2(s - m_new) for s in tiles]
                l_ref[rows, :] = alpha * l_ref[rows, :] + jnp.sum(functools.reduce(jnp.add, probs), axis=-1,
                                                                  keepdims=True)
                m_ref[rows, :] = m_new
                acc_ref[rows, :] = alpha * acc_ref[rows, :]
                p_rows.append(jnp.concatenate([p.astype(BF16) for p in probs], axis=1))
            return jnp.concatenate(p_rows, axis=0)

        def pv(unit, p_u):
            h, r0, nr, nk = unit
            acc_ref[pl.ds(h * tq + r0, nr), :] += _dot(p_u, cv[0:nk])

        n_u = len(units)
        scores = {0: qk(units[0])}
        probs = {}
        for u in range(n_u):
            if u + 1 < n_u:
                scores[u + 1] = qk(units[u + 1])
            probs[u] = softmax_update(units[u], scores.pop(u))
            if u >= 1:
                pv(units[u - 1], probs.pop(u - 1))
        pv(units[n_u - 1], probs.pop(n_u - 1))

    def step(j, carry):
        block(pl.multiple_of(j * tq, tq), False)
        return carry

    lax.fori_loop(0, i, step, 0)
    block(pl.multiple_of(i * tq, tq), True)
    for h in range(N_HEADS):
        rows = pl.ds(h * tq, tq)
        o = (acc_ref[rows, :] / l_ref[rows, :]).astype(BF16)
        y_ref[:, h * HEAD_W:(h + 1) * HEAD_W] = _dot(o, wuv_ref[:, h * HEAD_W:(h + 1) * HEAD_W]).astype(BF16)


def _mla_prompt(qp, kv, rk, wuv, b, t, tq):
    n = b * t
    nq = t // tq
    nrow = N_HEADS * tq
    return pl.pallas_call(
        functools.partial(_mla_prompt_kernel, tq=tq, rb=min(tq, MLA_ROW_GROUP)),
        grid=(b, nq),
        scratch_shapes=[pltpu.VMEM((nrow, 128), F32), pltpu.VMEM((nrow, 128), F32),
                        pltpu.VMEM((nrow, MLA_KV_RANK), F32)],
        in_specs=[pl.BlockSpec((N_HEADS, tq, 256), lambda bi, i: (0, bi * nq + i, 0)),
                  pl.BlockSpec((t, 256), lambda bi, i: (bi, 0)),
                  pl.BlockSpec((8, t), lambda bi, i: (0, bi)),
                  pl.BlockSpec((MLA_KV_RANK, N_HEADS * HEAD_W), lambda bi, i: (0, 0))],
        out_specs=pl.BlockSpec((tq, N_HEADS * HEAD_W), lambda bi, i: (bi * nq + i, 0)),
        out_shape=jax.ShapeDtypeStruct((n, N_HEADS * HEAD_W), BF16),
        compiler_params=_cparams(("arbitrary", "arbitrary")),
        name="mla_prompt",
    )(qp, kv, rk, wuv)


def _mla_sample_kernel(pt_ref, qp_ref, cnew_ref, rnewt_ref, wukt_ref, wuv_ref, lat_hbm, ropet_hbm,
                       y_ref, cbuf, rbuf, cbf, s_scr, part_scr, sem, *, layer, n_pages, ck, tv):
    b = pl.program_id(0)
    n_b = pl.num_programs(0)
    past = n_pages * PAGE
    tk = past + PAGE
    slot = b % 2
    tp = SAMPLE_PAD_T
    nrow = N_HEADS * tv
    npart = N_HEADS * 8

    def lat_copy(pg, p, sl):
        return pltpu.make_async_copy(lat_hbm.at[layer, pg], cbuf.at[sl, pl.ds(p * PAGE, PAGE), :], sem.at[0, sl])

    def rope_copy(pg, p, sl):
        return pltpu.make_async_copy(ropet_hbm.at[layer, pg], rbuf.at[sl, :, pl.ds(p * PAGE, PAGE)], sem.at[1, sl])

    def issue(seq, sl):
        def body(p, carry):
            pg = pt_ref[seq * n_pages + p]
            lat_copy(pg, p, sl).start()
            rope_copy(pg, p, sl).start()
            return carry
        lax.fori_loop(0, n_pages, body, 0, unroll=4)

    @pl.when(b == 0)
    def _():
        for sl in range(2):
            cbuf[sl, pl.ds(past, PAGE), :] = jnp.zeros((PAGE, MLA_KV_RANK), F32)
        issue(0, 0)

    def wait_all(sl):
        def wait_body(p, carry):
            lat_copy(0, p, sl).wait()
            rope_copy(0, p, sl).wait()
            return carry
        lax.fori_loop(0, n_pages, wait_body, 0, unroll=4)

    wait_all(slot)
    nxt = jnp.minimum(b + 1, n_b - 1)
    n_chunks = tk // ck
    pages_per_chunk = -(-n_pages // n_chunks)

    def prefetch(chunk_idx):
        for p in range(chunk_idx * pages_per_chunk, min((chunk_idx + 1) * pages_per_chunk, n_pages)):
            pg = pt_ref[nxt * n_pages + p]
            lat_copy(pg, p, 1 - slot).start()
            rope_copy(pg, p, 1 - slot).start()

    cbuf[slot, pl.ds(past, tp), :] = cnew_ref[0]
    rbuf[slot, :, pl.ds(past, PAGE)] = rnewt_ref[0]

    q = qp_ref[0]
    qr = q[:, MLA_KV_RANK:MLA_KV_RANK + MLA_ROPE]
    w_stack = jnp.concatenate([wukt_ref[...], q[:, 0:MLA_KV_RANK]], axis=0)
    n_up = N_HEADS * MLA_NOPE
    qpos = past + lax.broadcasted_iota(jnp.int32, (nrow, 1), 0) % tv
    head_ones = (lax.broadcasted_iota(jnp.int32, (nrow, npart), 0) // tv
                 == lax.broadcasted_iota(jnp.int32, (nrow, npart), 1) // 8).astype(BF16)

    for k0 in range(0, tk, ck):
        prefetch(k0 // ck)
        cb = cbuf[slot, pl.ds(k0, ck), :].astype(BF16)
        cbf[pl.ds(k0, ck), :] = cb
        rt = rbuf[slot, :, pl.ds(k0, ck)]
        big = _dot_nt(w_stack, cb)
        kn2 = big[0:n_up] * big[0:n_up]
        part_r = jnp.sum((rt * rt).reshape(MLA_ROPE // 8, 8, ck), axis=0)
        parts = [jnp.sum(kn2[h * MLA_NOPE:(h + 1) * MLA_NOPE].reshape(MLA_NOPE // 8, 8, ck), axis=0) + part_r
                 for h in range(N_HEADS)]
        part_scr[:, pl.ds(k0, ck)] = jnp.concatenate(parts, axis=0).astype(BF16)
        s_scr[:, pl.ds(k0, ck)] = big[n_up:n_up + nrow]

    rk = lax.rsqrt(_dot(head_ones, part_scr[...]) * (1.0 / MLA_QK) + EPS)
    s = (s_scr[...] + _dot(qr, rbuf[slot].astype(BF16))) * rk
    kpos = lax.broadcasted_iota(jnp.int32, (1, tk), 1)
    s = jnp.where(kpos <= qpos, s, NEG)
    m = jnp.max(s, axis=-1, keepdims=True)
    p = jnp.exp2(s - m)
    l = jnp.sum(p, axis=-1, keepdims=True)
    o = _dot(p.astype(BF16), cbf[...]) / l
    full = _dot(o.astype(BF16), wuv_ref[...])
    lane_head = lax.broadcasted_iota(jnp.int32, (tp, N_HEADS * HEAD_W), 1) // HEAD_W
    token = lax.broadcasted_iota(jnp.int32, (tp, N_HEADS * HEAD_W), 0)
    y = jnp.zeros((tp, N_HEADS * HEAD_W), F32)
    for h in range(N_HEADS):
        tile = full[(h * tv) // tp * tp:(h * tv) // tp * tp + tp]
        if (h * tv) % tp:
            tile = pltpu.roll(tile, tp - (h * tv) % tp, axis=0)
        y = jnp.where(lane_head == h, tile, y)
    y_ref[0] = jnp.where(token < tv, y, 0.0).astype(BF16)

    @pl.when(b == n_b - 1)
    def _():
        wait_all(1 - slot)


def _mla_sample(page_flat, qp3, c3, rt3, wukt, wuv, lat, ropet, layer, n_pages):
    bs, nrow, _ = qp3.shape
    tp = SAMPLE_PAD_T
    tv = nrow // N_HEADS
    assert tp % tv == 0 and nrow % 8 == 0
    tk = n_pages * PAGE + PAGE
    ck = max(d for d in range(PAGE, MLA_KEY_CHUNK + 1, PAGE) if tk % d == 0)
    kern = functools.partial(_mla_sample_kernel, layer=layer, n_pages=n_pages, ck=ck, tv=tv)
    grid_spec = pltpu.PrefetchScalarGridSpec(
        num_scalar_prefetch=1,
        grid=(bs,),
        in_specs=[pl.BlockSpec((1, nrow, 256), lambda b, pt: (b, 0, 0)),
                  pl.BlockSpec((1, tp, MLA_KV_RANK), lambda b, pt: (b, 0, 0)),
                  pl.BlockSpec((1, MLA_ROPE, PAGE), lambda b, pt: (b, 0, 0)),
                  pl.BlockSpec((N_HEADS * MLA_NOPE, MLA_KV_RANK), lambda b, pt: (0, 0)),
                  pl.BlockSpec((MLA_KV_RANK, N_HEADS * HEAD_W), lambda b, pt: (0, 0)),
                  pl.BlockSpec(memory_space=pl.ANY),
                  pl.BlockSpec(memory_space=pl.ANY)],
        out_specs=pl.BlockSpec((1, tp, N_HEADS * HEAD_W), lambda b, pt: (b, 0, 0)),
        scratch_shapes=[pltpu.VMEM((2, tk, MLA_KV_RANK), F32),
                        pltpu.VMEM((2, MLA_ROPE, tk), F32),
                        pltpu.VMEM((tk, MLA_KV_RANK), BF16),
                        pltpu.VMEM((nrow, tk), F32),
                        pltpu.VMEM((N_HEADS * 8, tk), BF16),
                        pltpu.SemaphoreType.DMA((2, 2))],
    )
    return pl.pallas_call(
        kern,
        grid_spec=grid_spec,
        out_shape=jax.ShapeDtypeStruct((bs, tp, N_HEADS * HEAD_W), BF16),
        compiler_params=_cparams(("arbitrary",)),
        name="mla_sample",
    )(page_flat, qp3, c3, rt3, wukt, wuv, lat, ropet)


def _memkv_kernel(x_ref, g_ref, wk_ref, wv_ref, gk_ref, k_ref, v_ref):
    mn = _rms(x_ref[...], g_ref[...]).astype(BF16)
    k = _dot(mn, wk_ref[...])
    gk = gk_ref[...]
    tm = x_ref.shape[0]
    v = _dot(mn, wv_ref[...])
    for h in range(N_HEADS):
        sl = slice(h * HEAD_W, (h + 1) * HEAD_W)
        k_ref[pl.ds(h, tm, stride=N_HEADS), :] = _rms(k[:, sl], gk)
        v_ref[pl.ds(h, tm, stride=N_HEADS), :] = v[:, sl]


def _memkv(mem2d, g, wk, wv, gk, tm):
    n, d = mem2d.shape
    br = wk.shape[1]
    return pl.pallas_call(
        _memkv_kernel,
        grid=(n // tm,),
        in_specs=[pl.BlockSpec((tm, d), lambda i: (i, 0)),
                  pl.BlockSpec((1, d), lambda i: (0, 0)),
                  pl.BlockSpec((d, br), lambda i: (0, 0)),
                  pl.BlockSpec((d, br), lambda i: (0, 0)),
                  pl.BlockSpec((1, HEAD_W), lambda i: (0, 0))],
        out_specs=[pl.BlockSpec((tm * N_HEADS, HEAD_W), lambda i: (i, 0)),
                   pl.BlockSpec((tm * N_HEADS, HEAD_W), lambda i: (i, 0))],
        out_shape=[jax.ShapeDtypeStruct((n * N_HEADS, HEAD_W), F32),
                   jax.ShapeDtypeStruct((n * N_HEADS, HEAD_W), F32)],
        compiler_params=_cparams(("arbitrary",)),
        name="memkv",
    )(mem2d, g.reshape(1, d), wk, wv, gk.reshape(1, HEAD_W))


def _mem_attn_kernel(q_ref, k_ref, v_ref, gq_ref, y_ref, *, sb, m):
    gq = gq_ref[...]
    scale = HEAD_W ** -0.5
    heads = range(N_HEADS)
    for s in range(sb):
        q = q_ref[s].astype(F32)
        qh = [(_rms(q[:, h * HEAD_W:(h + 1) * HEAD_W], gq) * (scale * LOG2E)).astype(BF16) for h in heads]
        sc = [_dot_nt(qh[h], k_ref[s, pl.ds(h, m, stride=N_HEADS), :].astype(BF16)) for h in heads]
        ps = [jnp.exp2(sc[h] - jnp.max(sc[h], axis=-1, keepdims=True)).astype(BF16) for h in heads]
        ones = jnp.ones((m, HEAD_W), BF16)
        for h in heads:
            v_ext = jnp.concatenate([v_ref[s, pl.ds(h, m, stride=N_HEADS), :].astype(BF16), ones], axis=1)
            o = _dot(ps[h], v_ext)
            y_ref[s, :, h * HEAD_W:(h + 1) * HEAD_W] = (o[:, 0:HEAD_W] / o[:, HEAD_W:2 * HEAD_W]).astype(BF16)


def _mem_attn(z3d, k4, v4, gq, layer, sb, tq):
    b, t, _ = z3d.shape
    m = k4.shape[2] // N_HEADS
    br = N_HEADS * HEAD_W
    return pl.pallas_call(
        functools.partial(_mem_attn_kernel, sb=sb, m=m),
        grid=(b // sb, t // tq),
        in_specs=[pl.BlockSpec((sb, tq, br), lambda i, j: (i, j, COL_MEMQ // br)),
                  pl.BlockSpec((None, sb, m * N_HEADS, HEAD_W), lambda i, j: (layer, i, 0, 0)),
                  pl.BlockSpec((None, sb, m * N_HEADS, HEAD_W), lambda i, j: (layer, i, 0, 0)),
                  pl.BlockSpec((1, HEAD_W), lambda i, j: (0, 0))],
        out_specs=pl.BlockSpec((sb, tq, br), lambda i, j: (i, j, 0)),
        out_shape=jax.ShapeDtypeStruct((b, t, br), BF16),
        compiler_params=_cparams(("arbitrary", "arbitrary")),
        name="mem_attn",
    )(z3d, k4, v4, gq.reshape(1, HEAD_W))


def _outproj_kernel(x_ref, mg_ref, sg_ref, y0_ref, y1_ref, y2_ref, y3_ref, wb_ref, wo_ref, o_ref):
    d = x_ref.shape[1]
    br = y0_ref.shape[1]
    merged = None
    for n, y_ref in enumerate((y0_ref, y1_ref, y2_ref, y3_ref)):
        sg = sg_ref[:, n * br:(n + 1) * br].astype(F32)
        ys = (y_ref[...].astype(F32) * (sg * (1.0 + jnp.tanh(sg)))).astype(BF16)
        proj = _dot(ys, wb_ref[n])
        term = (1.0 + jnp.tanh(mg_ref[:, n * d:(n + 1) * d].astype(F32))) * proj
        merged = term if merged is None else merged + term
    o_ref[...] = x_ref[...] + _dot(merged.astype(BF16), wo_ref[...])


def _outproj(x2d, z2d, ys, wb, wo, tm):
    n, d = x2d.shape
    br = ys[0].shape[1]
    nb = wb.shape[0]
    yspec = pl.BlockSpec((tm, br), lambda i: (i, 0))
    return pl.pallas_call(
        _outproj_kernel,
        grid=(n // tm,),
        in_specs=[pl.BlockSpec((tm, d), lambda i: (i, 0)),
                  pl.BlockSpec((tm, nb * d), lambda i: (i, COL_MG // (nb * d))),
                  pl.BlockSpec((tm, nb * br), lambda i: (i, COL_SG // (nb * br))),
                  yspec, yspec, yspec, yspec,
                  pl.BlockSpec((nb, br, d), lambda i: (0, 0, 0)),
                  pl.BlockSpec((d, d), lambda i: (0, 0))],
        out_specs=pl.BlockSpec((tm, d), lambda i: (i, 0)),
        out_shape=jax.ShapeDtypeStruct((n, d), F32),
        compiler_params=_cparams(("arbitrary",)),
        name="outproj",
    )(x2d, z2d, z2d, *ys, wb, wo)


def _rope_tables(pos):
    half = MLA_ROPE // 2
    freqs = ROPE_THETA ** (-np.arange(half, dtype=np.float64) / half)
    ang = np.asarray(pos, np.float64)[:, None] * freqs[None, :]
    cos = np.tile(np.cos(ang), (1, 8))
    sin = np.tile(np.sin(ang), (1, 8))
    sin[:, :64] *= -1.0
    return jnp.asarray(cos, F32), jnp.asarray(sin, F32)


def _mla_params(q_norm, w_uq, kv_norm, w_uk, gq, gk):
    half = MLA_ROPE // 2
    wq = w_uq.reshape(MLA_Q_RANK, N_HEADS, MLA_QK)
    wuq = jnp.concatenate([wq[:, :, :MLA_NOPE].reshape(MLA_Q_RANK, -1),
                           wq[:, :, MLA_NOPE:MLA_NOPE + half].reshape(MLA_Q_RANK, -1),
                           wq[:, :, MLA_NOPE + half:].reshape(MLA_Q_RANK, -1)], axis=1).astype(BF16)
    g2 = gq * gk
    wk = w_uk.reshape(MLA_KV_RANK, N_HEADS, MLA_NOPE)
    wabs = jnp.zeros((N_HEADS * MLA_NOPE, N_HEADS * MLA_KV_RANK), F32)
    for h in range(N_HEADS):
        blk = (wk[:, h, :] * g2[None, :MLA_NOPE]).T
        wabs = wabs.at[h * MLA_NOPE:(h + 1) * MLA_NOPE, h * MLA_KV_RANK:(h + 1) * MLA_KV_RANK].set(blk)
    grope = jnp.zeros((1, 128), F32).at[0, :MLA_ROPE].set(g2[MLA_NOPE:])
    j = np.arange(384)
    head_of = np.where(j < 256, j // MLA_NOPE, (j % 64) // half)
    indq = (head_of[:, None] == (np.arange(N_HEADS * 128) // 128)[None, :]).astype(np.float32)
    lane = j - 256
    is_rope = (j >= 256) & ((lane < half) | ((lane >= 64) & (lane < 64 + half)))
    indk = np.zeros((8, 384), np.float32)
    for h in range(N_HEADS):
        indk[h] = ((j < 256) & (j // MLA_NOPE == h)) | is_rope
    return dict(gq_norm=q_norm.reshape(1, -1), wuq=wuq, gkv_norm=kv_norm.reshape(1, -1),
                wuk=w_uk.astype(BF16), wukt=w_uk.T.astype(BF16), wabs=wabs.astype(BF16), grope=grope,
                indq=jnp.asarray(indq, BF16), indk=jnp.asarray(indk, BF16))


def _hgrn_lower_bounds(lb_param):
    p = jax.nn.softmax(lb_param.astype(F32), axis=0)
    return jnp.cumsum(p, axis=0) - p[0]


def kernel(x_prompt, x_sample, mem_prompt, cache_mla_latent, cache_mla_rope, page_table, state_hgrn, state_conv, cache_mem_k, cache_mem_v, norm_gain, w_in, conv_w, hgrn_lb, hgrn_norm, mla_q_norm, mla_w_uq, mla_kv_norm, mla_w_uk, mla_w_uv, mla_q_gain, mla_k_gain, mem_norm, mem_w_k, mem_w_v, mem_q_gain, mem_k_gain, w_branch_out, w_out):
    bp, tp, d = x_prompt.shape
    bs, ts, _ = x_sample.shape
    depth = w_in.shape[0]
    br = conv_w.shape[2]
    mem_len = mem_prompt.shape[1]
    n_pages = page_table.shape[1]
    past = n_pages * cache_mla_latent.shape[2]
    tpad = SAMPLE_PAD_T
    n_p = bp * tp
    n_s = bs * tpad

    lbs = _hgrn_lower_bounds(hgrn_lb)
    cos_p, sin_p = _rope_tables(np.arange(tp))
    cos_s, sin_s = _rope_tables(past + np.arange(tpad))
    tm_s = min(n_s, 1024)
    cos_s = jnp.tile(cos_s, (tm_s // tpad, 1))
    sin_s = jnp.tile(sin_s, (tm_s // tpad, 1))
    page_flat = page_table.reshape(-1).astype(jnp.int32)
    rope_t = jnp.swapaxes(cache_mla_rope, 2, 3)
    mem_k4 = cache_mem_k.reshape(depth, bs, mem_len * N_HEADS, HEAD_W)
    mem_v4 = cache_mem_v.reshape(depth, bs, mem_len * N_HEADS, HEAD_W)

    tm_p = min(tp, 512)
    xp = x_prompt.reshape(n_p, d)
    xs = jnp.pad(x_sample, ((0, 0), (0, tpad - ts), (0, 0))).reshape(n_s, d)
    mem2d = mem_prompt.reshape(bp * mem_len, d)
    zero_state = jnp.zeros((1, bp, N_HEADS, HEAD_W, HEAD_W), F32)

    outs = {k: [] for k in ("p_lat", "p_rope", "p_conv", "p_mk", "p_mv", "s_lat", "s_rope", "s_conv")}
    p_hg = s_hg = None
    for l in range(depth):
        w_in_t = _relayout_w_in(w_in[l])
        pw = _mla_params(mla_q_norm[l], mla_w_uq[l], mla_kv_norm[l], mla_w_uk[l], mla_q_gain[l], mla_k_gain[l])
        wuv = mla_w_uv[l].astype(BF16)
        wb = (0.5 * w_branch_out[l]).astype(BF16)
        wo = w_out[l].astype(BF16)

        z = _inproj(xp, norm_gain[l], w_in_t, min(n_p, INPROJ_ROWS), INPROJ_COLS)
        z3 = z.reshape(bp, tp, N_COLS)
        y_conv, tail = _conv(z, None, conv_w[l], tp, min(tp, CONV_TILE), 8)
        y_hg, p_hg = _hgrn(z3, lbs[l], hgrn_norm[l], zero_state, 0, l, 1, min(tp, HG_TILE), HG_CHUNK, None, depth,
                           p_hg)
        c, r, kv, rk, qp = _mla_prep(z, cos_p, sin_p, pw, min(tp, PREP_TILE))
        y_mla = _mla_prompt(qp, kv, rk, wuv, bp, tp, min(tp, MLA_TQ))
        mk, mv = _memkv(mem2d, mem_norm[l], mem_w_k[l].astype(BF16), mem_w_v[l].astype(BF16), mem_k_gain[l],
                        min(bp * mem_len, 512))
        rows_m = mem_len * N_HEADS
        y_mem = _mem_attn(z3, mk.reshape(1, bp, rows_m, HEAD_W), mv.reshape(1, bp, rows_m, HEAD_W), mem_q_gain[l],
                          0, 1, min(tp, MEM_TILE))
        xp = _outproj(xp, z, (y_conv, y_hg.reshape(n_p, br), y_mla, y_mem.reshape(n_p, br)), wb, wo, tm_p)
        outs["p_lat"].append(c.reshape(bp, tp, MLA_KV_RANK))
        outs["p_rope"].append(r.reshape(bp, tp, MLA_ROPE))
        outs["p_conv"].append(tail.reshape(bp, 8, br)[:, 8 - (CONV_K - 1):])
        outs["p_mk"].append(mk.reshape(bp, mem_len, N_HEADS, HEAD_W))
        outs["p_mv"].append(mv.reshape(bp, mem_len, N_HEADS, HEAD_W))

        z = _inproj(xs, norm_gain[l], w_in_t, tm_s, INPROJ_COLS)
        z3 = z.reshape(bs, tpad, N_COLS)
        hist = jnp.pad(state_conv[l], ((0, 0), (0, tpad - (CONV_K - 1)), (0, 0))).reshape(n_s, br)
        y_conv, u_all = _conv(z, hist, conv_w[l], tpad, tm_s, tm_s)
        sb = 8 if bs % 8 == 0 else 1
        sb_hg = HG_SAMPLE_SEQS if bs % HG_SAMPLE_SEQS == 0 else sb
        y_hg, s_hg = _hgrn(z3, lbs[l], hgrn_norm[l], state_hgrn, l, l, sb_hg, tpad, tpad, ts, depth, s_hg)
        c, r, kv, rk, qp = _mla_prep(z, cos_s, sin_s, pw, tm_s)
        qp3 = qp.reshape(N_HEADS, bs, tpad, 256)[:, :, :ts].transpose(1, 0, 2, 3).reshape(bs, N_HEADS * ts, 256)
        rt3 = jnp.pad(r.reshape(bs, tpad, MLA_ROPE).transpose(0, 2, 1), ((0, 0), (0, 0), (0, PAGE - tpad)))
        y_mla = _mla_sample(page_flat, qp3, c.reshape(bs, tpad, MLA_KV_RANK), rt3, pw["wukt"], wuv,
                            cache_mla_latent, rope_t, l, n_pages)
        y_mem = _mem_attn(z3, mem_k4, mem_v4, mem_q_gain[l], l, sb, tpad)
        xs = _outproj(xs, z, (y_conv, y_hg.reshape(n_s, br), y_mla.reshape(n_s, br), y_mem.reshape(n_s, br)),
                      wb, wo, tm_s)
        outs["s_lat"].append(c.reshape(bs, tpad, MLA_KV_RANK)[:, :ts])
        outs["s_rope"].append(r.reshape(bs, tpad, MLA_ROPE)[:, :ts])
        outs["s_conv"].append(u_all.reshape(bs, tpad, br)[:, ts - (CONV_K - 1):ts])

    st = lambda k: jnp.stack(outs[k])
    return (xp.reshape(bp, tp, d), xs.reshape(bs, tpad, d)[:, :ts], st("p_lat"), st("p_rope"), p_hg,
            st("p_conv"), st("p_mk"), st("p_mv"), st("s_lat"), st("s_rope"), s_hg, st("s_conv"))
```

```python
import functools

import numpy as np
import jax
import jax.numpy as jnp
from jax import lax
from jax.experimental import pallas as pl
from jax.experimental.pallas import tpu as pltpu

F32 = jnp.float32
BF16 = jnp.bfloat16

N_HEADS = 4
HEAD_W = 128
MLA_NOPE = 64
MLA_ROPE = 32
MLA_QK = MLA_NOPE + MLA_ROPE
MLA_Q_RANK = 192
MLA_KV_RANK = 128
CONV_K = 3
ROPE_THETA = 10000.0
EPS = 1e-6
NEG = -1e30
PAGE = 128
SAMPLE_PAD_T = 8
HG_CHUNK = 64
HG_BLOCK = 8
HG_TILE = 2048
HG_SAMPLE_SEQS = 32
INPROJ_ROWS = 1024
INPROJ_COLS = 2560
MLA_ROW_GROUP = 128
MLA_TQ = 512
MLA_KEY_CHUNK = 1664
CONV_TILE = 2048
PREP_TILE = 1024
MEM_TILE = 2048
LOG2E = 1.4426950408889634
VMEM_LIMIT = 56 * 1024 * 1024

COL_CONV = 0
COL_HG = 1536
COL_MLA = 3072
COL_MEMQ = 3584
COL_MG = 4096
COL_SG = 8192
N_COLS = 10240


def _cparams(sem):
    return pltpu.CompilerParams(dimension_semantics=sem, vmem_limit_bytes=VMEM_LIMIT)


def _rms(x, g):
    return x * lax.rsqrt(jnp.mean(x * x, axis=-1, keepdims=True) + EPS) * g


def _dot(a, b):
    return jnp.dot(a, b, preferred_element_type=F32)


def _dot_nt(a, b):
    return lax.dot_general(a, b, (((1,), (1,)), ((), ())), preferred_element_type=F32)


def _dot_tn(a, b):
    return lax.dot_general(a, b, (((0,), (0,)), ((), ())), preferred_element_type=F32)


def _inproj_kernel(x_ref, g_ref, w_ref, z_ref, hn_ref):
    @pl.when(pl.program_id(1) == 0)
    def _():
        hn_ref[...] = _rms(x_ref[...], g_ref[...]).astype(BF16)

    z_ref[...] = _dot_nt(hn_ref[...], w_ref[...]).astype(BF16)


def _relayout_w_in(w):
    wt = jnp.swapaxes(w, 0, 1)
    d = w.shape[0]
    rows = lambda a, b, scale=None: (wt[a:b] if scale is None else wt[a:b] * scale).astype(BF16)
    zeros = lambda n: jnp.zeros((n, d), BF16)
    o_q, o_kv, o_pe, o_mq, o_sg, o_mg = 3072, 3264, 3392, 3424, 3936, 5984
    half = MLA_ROPE // 2
    return jnp.concatenate([rows(0, o_q),
                            rows(o_q, o_kv), zeros(64),
                            rows(o_kv, o_pe),
                            rows(o_pe, o_pe + half), zeros(48), rows(o_pe + half, o_mq), zeros(48),
                            rows(o_mq, o_sg),
                            rows(o_mg, w.shape[1], 0.5),
                            rows(o_sg, o_mg, 0.5)], axis=0)


def _inproj(x2d, g, w_t, tm, tn):
    n, d = x2d.shape
    ncol = w_t.shape[0]
    return pl.pallas_call(
        _inproj_kernel,
        grid=(n // tm, ncol // tn),
        in_specs=[pl.BlockSpec((tm, d), lambda i, j: (i, 0)),
                  pl.BlockSpec((1, d), lambda i, j: (0, 0)),
                  pl.BlockSpec((tn, d), lambda i, j: (j, 0))],
        out_specs=pl.BlockSpec((tm, tn), lambda i, j: (i, j)),
        out_shape=jax.ShapeDtypeStruct((n, ncol), BF16),
        scratch_shapes=[pltpu.VMEM((tm, d), BF16)],
        compiler_params=_cparams(("arbitrary", "arbitrary")),
        name="inproj",
    )(x2d, g.reshape(1, d), w_t)


def _conv_kernel(*refs, seq_t, tr, tail_rows, has_hist):
    if has_hist:
        z_ref, halo_ref, hist_ref, w_ref, y_ref, tail_ref = refs
    else:
        z_ref, halo_ref, w_ref, y_ref, tail_ref = refs
    i = pl.program_id(0)
    br = y_ref.shape[1]
    z = z_ref[...].astype(F32)
    u = z[:, 2 * br:3 * br] * z[:, 0:br]
    zh = halo_ref[...].astype(F32)
    uh = zh[:, 2 * br:3 * br] * zh[:, 0:br]
    u1 = pltpu.roll(u, 1, axis=0)
    u2 = pltpu.roll(u, 2, axis=0)
    if has_hist:
        loc = lax.broadcasted_iota(jnp.int32, (tr, 1), 0)
        t = loc % seq_t
        hp = hist_ref[...]
        u1 = jnp.where(t == 0, pltpu.roll(hp, tr - 1, axis=0), u1)
        u2 = jnp.where(t < 2, hp, u2)
    else:
        loc = lax.broadcasted_iota(jnp.int32, (8, 1), 0)
        inside = (i * tr) % seq_t != 0
        prev1 = jnp.where(inside, uh[7:8], 0.0)
        prev2 = jnp.where(inside, uh[6:7], 0.0)
        head1 = jnp.where(loc == 0, prev1, u1[0:8])
        head2 = jnp.where(loc == 0, prev2, jnp.where(loc == 1, prev1, u2[0:8]))
        u1 = jnp.concatenate([head1, u1[8:]], axis=0)
        u2 = jnp.concatenate([head2, u2[8:]], axis=0)
    w = w_ref[...]
    conv = w[0:1] * u2 + w[1:2] * u1 + w[2:3] * u
    y_ref[...] = (z[:, br:2 * br] * conv).astype(BF16)

    @pl.when(((i + 1) * tr) % max(seq_t, tr) == 0)
    def _():
        tail_ref[...] = u[tr - tail_rows:, :]


def _conv(z2d, hist_rows, w, seq_t, tr, tail_rows):
    n = z2d.shape[0]
    br = w.shape[1]
    has_hist = hist_rows is not None
    assert (n == tr and tr % seq_t == 0) if has_hist else seq_t % tr == 0
    group = max(seq_t, tr)
    n_tail = (n // group) * tail_rows
    in_specs = [pl.BlockSpec((tr, 3 * br), lambda i: (i, 0)),
                pl.BlockSpec((8, 3 * br), lambda i: (jnp.maximum(i * (tr // 8) - 1, 0), 0))]
    args = [z2d, z2d]
    if has_hist:
        in_specs.append(pl.BlockSpec((tr, br), lambda i: (i, 0)))
        args.append(hist_rows)
    in_specs.append(pl.BlockSpec((CONV_K, br), lambda i: (0, 0)))
    args.append(w)
    return pl.pallas_call(
        functools.partial(_conv_kernel, seq_t=seq_t, tr=tr, tail_rows=tail_rows, has_hist=has_hist),
        grid=(n // tr,),
        in_specs=in_specs,
        out_specs=[pl.BlockSpec((tr, br), lambda i: (i, 0)),
                   pl.BlockSpec((tail_rows, br), lambda i: ((i * tr) // group, 0))],
        out_shape=[jax.ShapeDtypeStruct((n, br), BF16), jax.ShapeDtypeStruct((n_tail, br), F32)],
        compiler_params=_cparams(("arbitrary",)),
        name="conv",
    )(*args)


def _hgrn_intra_diag(q, bk, b2, chunk):
    lane = lax.broadcasted_iota(jnp.int32, (HG_BLOCK, chunk), 1)
    trow = lax.broadcasted_iota(jnp.int32, (HG_BLOCK, chunk), 0)
    a_rows = []
    for r in range(0, chunk, HG_BLOCK):
        b_blk = b2[r:r + HG_BLOCK]
        q_blk = q[r:r + HG_BLOCK]
        acc = jnp.zeros((HG_BLOCK, chunk), F32)
        for s in range(HG_BLOCK):
            p = jnp.exp2(b_blk - bk[r + s:r + s + 1]) * q_blk
            acc = jnp.where(lane == r + s, jnp.sum(p, axis=-1, keepdims=True), acc)
        a_rows.append(jnp.where(lane - r <= trow, acc, 0.0))
    return a_rows[0] if len(a_rows) == 1 else jnp.concatenate(a_rows, axis=0)


def _hgrn_inter_block_factors(q, bk, b2, chunk):
    nb = chunk // HG_BLOCK
    zero = jnp.zeros((HG_BLOCK, HEAD_W), F32)
    q_rows, k_rows = [], []
    for i in range(nb):
        blk = slice(i * HG_BLOCK, (i + 1) * HG_BLOCK)
        q_tiles, k_tiles = [], []
        for j in range(nb - 1):
            rho = b2[(j + 1) * HG_BLOCK - 1:(j + 1) * HG_BLOCK]
            q_tiles.append(q[blk] * jnp.exp2(b2[blk] - rho) if j < i else zero)
            k_tiles.append(jnp.exp2(rho - bk[blk]) if j == i else zero)
        q_rows.append(jnp.concatenate(q_tiles, axis=1))
        k_rows.append(jnp.concatenate(k_tiles, axis=1))
    return jnp.concatenate(q_rows, axis=0).astype(BF16), jnp.concatenate(k_rows, axis=0).astype(BF16)


def _hgrn_tile(load, store, lb, gain, sts, valid_fn, tril_bf, chunk, n_c):
    units = [(s, c) for s in range(len(sts)) for c in range(n_c)]
    chunks = range(len(units))
    q, v, v_t, lk, g_hi, g_lo = [], [], [], [], [], []
    for s, c in units:
        qc, zf, vc = load(s, c)
        f = lb + (1.0 - lb) * jax.nn.sigmoid(zf)
        g = jnp.log(f)
        k = 1.0 - f
        valid = valid_fn(c)
        if valid is not None:
            g = jnp.where(valid, g, 0.0)
            k = jnp.where(valid, k, 0.0)
        hi = g.astype(BF16)
        q.append(qc)
        v.append(vc.astype(BF16))
        v_t.append(vc.T.astype(BF16))
        lk.append(jnp.log2(k))
        g_hi.append(hi)
        g_lo.append((g - hi.astype(F32)).astype(BF16))
    b2 = [(_dot(tril_bf, g_hi[c]) + _dot(tril_bf, g_lo[c])) * LOG2E for c in chunks]
    bk = [b2[c] - lk[c] for c in chunks]
    bl2 = [b[chunk - 1:chunk, :] for b in b2]
    upd = [_dot(v_t[c], jnp.exp2(bl2[c] - bk[c]).astype(BF16)) for c in chunks]
    a = [_hgrn_intra_diag(q[c], bk[c], b2[c], chunk) for c in chunks]
    if chunk > HG_BLOCK:
        factors = [_hgrn_inter_block_factors(q[c], bk[c], b2[c], chunk) for c in chunks]
        a = [a[c] + _dot_nt(factors[c][0], factors[c][1]) for c in chunks]
    o = [_dot(a[c].astype(BF16), v[c]) for c in chunks]
    sts = list(sts)
    states = []
    for u, (s, _) in enumerate(units):
        states.append(sts[s].astype(BF16))
        sts[s] = sts[s] * jnp.exp2(bl2[u]) + upd[u]
    for u, (s, c) in enumerate(units):
        oc = o[u] + _dot_nt((q[u] * jnp.exp2(b2[u])).astype(BF16), states[u])
        store(s, c, _rms(oc, gain).astype(BF16))
    return sts


def _hgrn_kernel(*refs, sb, tt, chunk, t_valid, fill_slots):
    q_ref, f_ref, i_ref, lb_ref, g_ref, s0_ref = refs[:6]
    y_ref, sout_ref, st_ref = refs[-3:]
    tstep = pl.program_id(2)
    n_t = pl.num_programs(2)

    @pl.when(tstep == 0)
    def _():
        for s in range(sb):
            st_ref[s] = s0_ref[s, 0].T

    lb = lb_ref[0]
    gain = g_ref[0]
    row = lax.broadcasted_iota(jnp.int32, (chunk, chunk), 0)
    col = lax.broadcasted_iota(jnp.int32, (chunk, chunk), 1)
    tril_bf = (row >= col).astype(BF16)
    scale = HEAD_W ** -0.5

    def load(s, c):
        rows = pl.ds(c * chunk, chunk)
        return (q_ref[s, rows, :].astype(F32) * scale, f_ref[s, rows, :].astype(F32),
                i_ref[s, rows, :].astype(F32))

    def store(s, c, y):
        y_ref[s, pl.ds(c * chunk, chunk), :] = y

    def valid_fn(c):
        if t_valid is None:
            return None
        return tstep * tt + c * chunk + lax.broadcasted_iota(jnp.int32, (chunk, 1), 0) < t_valid

    new_states = _hgrn_tile(load, store, lb, gain, [st_ref[s] for s in range(sb)], valid_fn, tril_bf, chunk,
                            tt // chunk)
    for s in range(sb):
        st_ref[s] = new_states[s]

    @pl.when(tstep == n_t - 1)
    def _():
        for s in range(sb):
            final = st_ref[s].T
            if fill_slots is None:
                sout_ref[s, 0] = final
            else:
                for slot in range(fill_slots):
                    sout_ref[slot, s, 0] = final


def _hgrn(z3d, lb, gain, s0, s0_layer, layer, sb, tt, chunk, t_valid, depth, stacked_prev):
    b, t, _ = z3d.shape
    cb = COL_HG // HEAD_W
    first = stacked_prev is None
    assert first == (layer == 0)
    kern = functools.partial(_hgrn_kernel, sb=sb, tt=tt, chunk=chunk, t_valid=t_valid,
                             fill_slots=depth if first else None)
    state_spec = (pl.BlockSpec((depth, sb, 1, HEAD_W, HEAD_W), lambda i, h, k: (0, i, h, 0, 0)) if first else
                  pl.BlockSpec((None, sb, 1, HEAD_W, HEAD_W), lambda i, h, k: (layer, i, h, 0, 0)))
    in_specs = [pl.BlockSpec((sb, tt, HEAD_W), lambda i, h, k: (i, k, cb + h)),
                pl.BlockSpec((sb, tt, HEAD_W), lambda i, h, k: (i, k, cb + N_HEADS + h)),
                pl.BlockSpec((sb, tt, HEAD_W), lambda i, h, k: (i, k, cb + 2 * N_HEADS + h)),
                pl.BlockSpec((1, 1, HEAD_W), lambda i, h, k: (h, 0, 0)),
                pl.BlockSpec((1, 1, HEAD_W), lambda i, h, k: (h, 0, 0)),
                pl.BlockSpec((None, sb, 1, HEAD_W, HEAD_W), lambda i, h, k: (s0_layer, i, h, 0, 0))]
    args = [z3d, z3d, z3d, lb.reshape(N_HEADS, 1, HEAD_W), gain.reshape(N_HEADS, 1, HEAD_W), s0]
    aliases = {}
    if stacked_prev is not None:
        in_specs.append(pl.BlockSpec(memory_space=pl.ANY))
        args.append(stacked_prev)
        aliases = {len(args) - 1: 1}
    return pl.pallas_call(
        kern,
        grid=(b // sb, N_HEADS, t // tt),
        in_specs=in_specs,
        out_specs=[pl.BlockSpec((sb, tt, HEAD_W), lambda i, h, k: (i, k, h)),
                   state_spec],
        out_shape=[jax.ShapeDtypeStruct((b, t, N_HEADS * HEAD_W), BF16),
                   jax.ShapeDtypeStruct((depth, b, N_HEADS, HEAD_W, HEAD_W), F32)],
        scratch_shapes=[pltpu.VMEM((sb, HEAD_W, HEAD_W), F32)],
        input_output_aliases=aliases,
        compiler_params=_cparams(("arbitrary", "arbitrary", "arbitrary")),
        name="hgrn",
    )(*args)


def _mla_prep_kernel(z_ref, cos_ref, sin_ref, gq_ref, wuq_ref, gkv_ref, wuk_ref, wabs_ref, grope_ref,
                     indq_ref, indk_ref, c_ref, r_ref, kv_ref, rk_ref, qp_ref):
    z = z_ref[...].astype(F32)
    ql = z[:, 0:MLA_Q_RANK]
    kvl = z[:, 256:384]
    kpe = z[:, 384:512]
    cos = cos_ref[...]
    sin = sin_ref[...]
    lane = lax.broadcasted_iota(jnp.int32, (1, 128), 1)

    qf = _dot(_rms(ql, gq_ref[...]).astype(BF16), wuq_ref[...])
    q_nope = qf[:, 0:256]
    rq_in = qf[:, 256:384]
    rot_q = rq_in * cos + pltpu.roll(rq_in, 64, axis=1) * sin
    rot_k = kpe * cos + pltpu.roll(kpe, 64, axis=1) * sin

    c = _rms(kvl, gkv_ref[...])
    c_ref[...] = c
    r32 = jnp.where(lane < 16, rot_k, pltpu.roll(rot_k, 80, axis=1))
    r32 = jnp.where(lane < MLA_ROPE, r32, 0.0)
    r_ref[...] = r32[:, 0:MLA_ROPE]
    cb = c.astype(BF16)
    kv_ref[...] = jnp.concatenate([cb, r32.astype(BF16)], axis=1)

    kn = _dot(cb, wuk_ref[...])
    kcat2 = jnp.concatenate([kn * kn, rot_k * rot_k], axis=1).astype(BF16)
    ssk = _dot_nt(indk_ref[...], kcat2)
    rk_ref[...] = lax.rsqrt(ssk * (1.0 / MLA_QK) + EPS)

    qcat2 = jnp.concatenate([q_nope * q_nope, rot_q * rot_q], axis=1).astype(BF16)
    ssq = _dot(qcat2, indq_ref[...])
    rq = lax.rsqrt(ssq * (1.0 / MLA_QK) + EPS) * (MLA_QK ** -0.5 * LOG2E)

    q_abs = _dot(q_nope.astype(BF16), wabs_ref[...])
    grope = grope_ref[...]
    half = MLA_ROPE // 2
    rot_q_hi = pltpu.roll(rot_q, 80, axis=1)
    for h in range(N_HEADS):
        lo = half * h
        pair = jnp.where((lane >= lo) & (lane < lo + half), rot_q,
                         jnp.where((lane >= lo + half) & (lane < lo + 2 * half), rot_q_hi, 0.0))
        qr = (pair if h == 0 else pltpu.roll(pair, 128 - lo, axis=1)) * grope
        rq_h = rq[:, h * 128:(h + 1) * 128]
        qp_ref[h, :, 0:128] = (q_abs[:, h * 128:(h + 1) * 128] * rq_h).astype(BF16)
        qp_ref[h, :, 128:256] = (qr * rq_h).astype(BF16)


def _mla_prep(z2d, cos_t, sin_t, pw, tm):
    n = z2d.shape[0]
    n_tab = cos_t.shape[0] // tm
    full = lambda shape: pl.BlockSpec(shape, lambda i: (0,) * len(shape))
    return pl.pallas_call(
        _mla_prep_kernel,
        grid=(n // tm,),
        in_specs=[pl.BlockSpec((tm, 512), lambda i: (i, COL_MLA // 512)),
                  pl.BlockSpec((tm, 128), lambda i: (i % n_tab, 0)),
                  pl.BlockSpec((tm, 128), lambda i: (i % n_tab, 0)),
                  full((1, MLA_Q_RANK)), full((MLA_Q_RANK, 384)), full((1, MLA_KV_RANK)),
                  full((MLA_KV_RANK, 256)), full((256, 512)), full((1, 128)),
                  full((384, N_HEADS * 128)), full((8, 384))],
        out_specs=[pl.BlockSpec((tm, MLA_KV_RANK), lambda i: (i, 0)),
                   pl.BlockSpec((tm, MLA_ROPE), lambda i: (i, 0)),
                   pl.BlockSpec((tm, 256), lambda i: (i, 0)),
                   pl.BlockSpec((8, tm), lambda i: (0, i)),
                   pl.BlockSpec((N_HEADS, tm, 256), lambda i: (0, i, 0))],
        out_shape=[jax.ShapeDtypeStruct((n, MLA_KV_RANK), F32),
                   jax.ShapeDtypeStruct((n, MLA_ROPE), F32),
                   jax.ShapeDtypeStruct((n, 256), BF16),
                   jax.ShapeDtypeStruct((8, n), F32),
                   jax.ShapeDtypeStruct((N_HEADS, n, 256), BF16)],
        compiler_params=_cparams(("arbitrary",)),
        name="mla_prep",
    )(z2d, cos_t, sin_t, pw["gq_norm"], pw["wuq"], pw["gkv_norm"], pw["wuk"], pw["wabs"], pw["grope"],
      pw["indq"], pw["indk"])


def _mla_prompt_kernel(qp_ref, kv_ref, rk_ref, wuv_ref, y_ref, m_ref, l_ref, acc_ref, *, tq, rb):
    i = pl.program_id(1)
    nrow = N_HEADS * tq
    m_ref[...] = jnp.full((nrow, 128), NEG, F32)
    l_ref[...] = jnp.zeros((nrow, 128), F32)
    acc_ref[...] = jnp.zeros((nrow, MLA_KV_RANK), F32)
    row = lax.broadcasted_iota(jnp.int32, (rb, 128), 0)
    col = lax.broadcasted_iota(jnp.int32, (rb, 128), 1)
    def block(k0, diagonal):
        kv = kv_ref[pl.ds(k0, tq), :]
        cv = kv[:, 0:MLA_KV_RANK]
        units = [(h, 0, tq, tq) for h in range(N_HEADS)]

        def qk(unit):
            h, r0, nr, nk = unit
            return _dot_nt(qp_ref[h, pl.ds(r0, nr), :], kv[0:nk])

        def softmax_update(unit, s_u):
            h, r0, nr, nk = unit
            p_rows = []
            for r in range(0, nr, rb):
                rows = pl.ds(h * tq + r0 + r, rb)
                tiles = []
                for c in range(0, nk, 128):
                    s = s_u[r:r + rb, c:c + 128] * rk_ref[h:h + 1, pl.ds(k0 + c, 128)]
                    if diagonal:
                        s = jnp.where(col + c <= row + (r0 + r), s, NEG)
                    tiles.append(s)
                m_old = m_ref[rows, :]
                m_new = jnp.maximum(m_old, jnp.max(functools.reduce(jnp.maximum, tiles), axis=-1, keepdims=True))
                alpha = jnp.exp2(m_old - m_new)
                probs = [jnp.exp2(s - m_new) for s in tiles]
                l_ref[rows, :] = alpha * l_ref[rows, :] + jnp.sum(functools.reduce(jnp.add, probs), axis=-1,
                                                                  keepdims=True)
                m_ref[rows, :] = m_new
                acc_ref[rows, :] = alpha * acc_ref[rows, :]
                p_rows.append(jnp.concatenate([p.astype(BF16) for p in probs], axis=1))
            return jnp.concatenate(p_rows, axis=0)

        def pv(unit, p_u):
            h, r0, nr, nk = unit
            acc_ref[pl.ds(h * tq + r0, nr), :] += _dot(p_u, cv[0:nk])

        n_u = len(units)
        scores = {0: qk(units[0])}
        probs = {}
        for u in range(n_u):
            if u + 1 < n_u:
                scores[u + 1] = qk(units[u + 1])
            probs[u] = softmax_update(units[u], scores.pop(u))
            if u >= 1:
                pv(units[u - 1], probs.pop(u - 1))
        pv(units[n_u - 1], probs.pop(n_u - 1))

    def step(j, carry):
        block(pl.multiple_of(j * tq, tq), False)
        return carry

    lax.fori_loop(0, i, step, 0)
    block(pl.multiple_of(i * tq, tq), True)
    for h in range(N_HEADS):
        rows = pl.ds(h * tq, tq)
        o = (acc_ref[rows, :] / l_ref[rows, :]).astype(BF16)
        y_ref[:, h * HEAD_W:(h + 1) * HEAD_W] = _dot(o, wuv_ref[:, h * HEAD_W:(h + 1) * HEAD_W]).astype(BF16)


def _mla_prompt(qp, kv, rk, wuv, b, t, tq):
    n = b * t
    nq = t // tq
    nrow = N_HEADS * tq
    return pl.pallas_call(
        functools.partial(_mla_prompt_kernel, tq=tq, rb=min(tq, MLA_ROW_GROUP)),
        grid=(b, nq),
        scratch_shapes=[pltpu.VMEM((nrow, 128), F32), pltpu.VMEM((nrow, 128), F32),
                        pltpu.VMEM((nrow, MLA_KV_RANK), F32)],
        in_specs=[pl.BlockSpec((N_HEADS, tq, 256), lambda bi, i: (0, bi * nq + i, 0)),
                  pl.BlockSpec((t, 256), lambda bi, i: (bi, 0)),
                  pl.BlockSpec((8, t), lambda bi, i: (0, bi)),
                  pl.BlockSpec((MLA_KV_RANK, N_HEADS * HEAD_W), lambda bi, i: (0, 0))],
        out_specs=pl.BlockSpec((tq, N_HEADS * HEAD_W), lambda bi, i: (bi * nq + i, 0)),
        out_shape=jax.ShapeDtypeStruct((n, N_HEADS * HEAD_W), BF16),
        compiler_params=_cparams(("arbitrary", "arbitrary")),
        name="mla_prompt",
    )(qp, kv, rk, wuv)


def _mla_sample_kernel(pt_ref, qp_ref, cnew_ref, rnewt_ref, wukt_ref, wuv_ref, lat_hbm, ropet_hbm,
                       y_ref, cbuf, rbuf, cbf, s_scr, part_scr, sem, *, layer, n_pages, ck, tv):
    b = pl.program_id(0)
    n_b = pl.num_programs(0)
    past = n_pages * PAGE
    tk = past + PAGE
    slot = b % 2
    tp = SAMPLE_PAD_T
    nrow = N_HEADS * tv
    npart = N_HEADS * 8

    def lat_copy(pg, p, sl):
        return pltpu.make_async_copy(lat_hbm.at[layer, pg], cbuf.at[sl, pl.ds(p * PAGE, PAGE), :], sem.at[0, sl])

    def rope_copy(pg, p, sl):
        return pltpu.make_async_copy(ropet_hbm.at[layer, pg], rbuf.at[sl, :, pl.ds(p * PAGE, PAGE)], sem.at[1, sl])

    def issue(seq, sl):
        def body(p, carry):
            pg = pt_ref[seq * n_pages + p]
            lat_copy(pg, p, sl).start()
            rope_copy(pg, p, sl).start()
            return carry
        lax.fori_loop(0, n_pages, body, 0, unroll=4)

    @pl.when(b == 0)
    def _():
        for sl in range(2):
            cbuf[sl, pl.ds(past, PAGE), :] = jnp.zeros((PAGE, MLA_KV_RANK), F32)
        issue(0, 0)

    def wait_all(sl):
        def wait_body(p, carry):
            lat_copy(0, p, sl).wait()
            rope_copy(0, p, sl).wait()
            return carry
        lax.fori_loop(0, n_pages, wait_body, 0, unroll=4)

    wait_all(slot)
    nxt = jnp.minimum(b + 1, n_b - 1)
    n_chunks = tk // ck
    pages_per_chunk = -(-n_pages // n_chunks)

    def prefetch(chunk_idx):
        for p in range(chunk_idx * pages_per_chunk, min((chunk_idx + 1) * pages_per_chunk, n_pages)):
            pg = pt_ref[nxt * n_pages + p]
            lat_copy(pg, p, 1 - slot).start()
            rope_copy(pg, p, 1 - slot).start()

    cbuf[slot, pl.ds(past, tp), :] = cnew_ref[0]
    rbuf[slot, :, pl.ds(past, PAGE)] = rnewt_ref[0]

    q = qp_ref[0]
    qr = q[:, MLA_KV_RANK:MLA_KV_RANK + MLA_ROPE]
    w_stack = jnp.concatenate([wukt_ref[...], q[:, 0:MLA_KV_RANK]], axis=0)
    n_up = N_HEADS * MLA_NOPE
    qpos = past + lax.broadcasted_iota(jnp.int32, (nrow, 1), 0) % tv
    head_ones = (lax.broadcasted_iota(jnp.int32, (nrow, npart), 0) // tv
                 == lax.broadcasted_iota(jnp.int32, (nrow, npart), 1) // 8).astype(BF16)

    for k0 in range(0, tk, ck):
        prefetch(k0 // ck)
        cb = cbuf[slot, pl.ds(k0, ck), :].astype(BF16)
        cbf[pl.ds(k0, ck), :] = cb
        rt = rbuf[slot, :, pl.ds(k0, ck)]
        big = _dot_nt(w_stack, cb)
        kn2 = big[0:n_up] * big[0:n_up]
        part_r = jnp.sum((rt * rt).reshape(MLA_ROPE // 8, 8, ck), axis=0)
        parts = [jnp.sum(kn2[h * MLA_NOPE:(h + 1) * MLA_NOPE].reshape(MLA_NOPE // 8, 8, ck), axis=0) + part_r
                 for h in range(N_HEADS)]
        part_scr[:, pl.ds(k0, ck)] = jnp.concatenate(parts, axis=0).astype(BF16)
        s_scr[:, pl.ds(k0, ck)] = big[n_up:n_up + nrow]

    rk = lax.rsqrt(_dot(head_ones, part_scr[...]) * (1.0 / MLA_QK) + EPS)
    s = (s_scr[...] + _dot(qr, rbuf[slot].astype(BF16))) * rk
    kpos = lax.broadcasted_iota(jnp.int32, (1, tk), 1)
    s = jnp.where(kpos <= qpos, s, NEG)
    m = jnp.max(s, axis=-1, keepdims=True)
    p = jnp.exp2(s - m)
    l = jnp.sum(p, axis=-1, keepdims=True)
    o = _dot(p.astype(BF16), cbf[...]) / l
    full = _dot(o.astype(BF16), wuv_ref[...])
    lane_head = lax.broadcasted_iota(jnp.int32, (tp, N_HEADS * HEAD_W), 1) // HEAD_W
    token = lax.broadcasted_iota(jnp.int32, (tp, N_HEADS * HEAD_W), 0)
    y = jnp.zeros((tp, N_HEADS * HEAD_W), F32)
    for h in range(N_HEADS):
        tile = full[(h * tv) // tp * tp:(h * tv) // tp * tp + tp]
        if (h * tv) % tp:
            tile = pltpu.roll(tile, tp - (h * tv) % tp, axis=0)
        y = jnp.where(lane_head == h, tile, y)
    y_ref[0] = jnp.where(token < tv, y, 0.0).astype(BF16)

    @pl.when(b == n_b - 1)
    def _():
        wait_all(1 - slot)


def _mla_sample(page_flat, qp3, c3, rt3, wukt, wuv, lat, ropet, layer, n_pages):
    bs, nrow, _ = qp3.shape
    tp = SAMPLE_PAD_T
    tv = nrow // N_HEADS
    assert tp % tv == 0 and nrow % 8 == 0
    tk = n_pages * PAGE + PAGE
    ck = max(d for d in range(PAGE, MLA_KEY_CHUNK + 1, PAGE) if tk % d == 0)
    kern = functools.partial(_mla_sample_kernel, layer=layer, n_pages=n_pages, ck=ck, tv=tv)
    grid_spec = pltpu.PrefetchScalarGridSpec(
        num_scalar_prefetch=1,
        grid=(bs,),
        in_specs=[pl.BlockSpec((1, nrow, 256), lambda b, pt: (b, 0, 0)),
                  pl.BlockSpec((1, tp, MLA_KV_RANK), lambda b, pt: (b, 0, 0)),
                  pl.BlockSpec((1, MLA_ROPE, PAGE), lambda b, pt: (b, 0, 0)),
                  pl.BlockSpec((N_HEADS * MLA_NOPE, MLA_KV_RANK), lambda b, pt: (0, 0)),
                  pl.BlockSpec((MLA_KV_RANK, N_HEADS * HEAD_W), lambda b, pt: (0, 0)),
                  pl.BlockSpec(memory_space=pl.ANY),
                  pl.BlockSpec(memory_space=pl.ANY)],
        out_specs=pl.BlockSpec((1, tp, N_HEADS * HEAD_W), lambda b, pt: (b, 0, 0)),
        scratch_shapes=[pltpu.VMEM((2, tk, MLA_KV_RANK), F32),
                        pltpu.VMEM((2, MLA_ROPE, tk), F32),
                        pltpu.VMEM((tk, MLA_KV_RANK), BF16),
                        pltpu.VMEM((nrow, tk), F32),
                        pltpu.VMEM((N_HEADS * 8, tk), BF16),
                        pltpu.SemaphoreType.DMA((2, 2))],
    )
    return pl.pallas_call(
        kern,
        grid_spec=grid_spec,
        out_shape=jax.ShapeDtypeStruct((bs, tp, N_HEADS * HEAD_W), BF16),
        compiler_params=_cparams(("arbitrary",)),
        name="mla_sample",
    )(page_flat, qp3, c3, rt3, wukt, wuv, lat, ropet)


def _memkv_kernel(x_ref, g_ref, wk_ref, wv_ref, gk_ref, k_ref, v_ref):
    mn = _rms(x_ref[...], g_ref[...]).astype(BF16)
    k = _dot(mn, wk_ref[...])
    gk = gk_ref[...]
    tm = x_ref.shape[0]
    v = _dot(mn, wv_ref[...])
    for h in range(N_HEADS):
        sl = slice(h * HEAD_W, (h + 1) * HEAD_W)
        k_ref[pl.ds(h, tm, stride=N_HEADS), :] = _rms(k[:, sl], gk)
        v_ref[pl.ds(h, tm, stride=N_HEADS), :] = v[:, sl]


def _memkv(mem2d, g, wk, wv, gk, tm):
    n, d = mem2d.shape
    br = wk.shape[1]
    return pl.pallas_call(
        _memkv_kernel,
        grid=(n // tm,),
        in_specs=[pl.BlockSpec((tm, d), lambda i: (i, 0)),
                  pl.BlockSpec((1, d), lambda i: (0, 0)),
                  pl.BlockSpec((d, br), lambda i: (0, 0)),
                  pl.BlockSpec((d, br), lambda i: (0, 0)),
                  pl.BlockSpec((1, HEAD_W), lambda i: (0, 0))],
        out_specs=[pl.BlockSpec((tm * N_HEADS, HEAD_W), lambda i: (i, 0)),
                   pl.BlockSpec((tm * N_HEADS, HEAD_W), lambda i: (i, 0))],
        out_shape=[jax.ShapeDtypeStruct((n * N_HEADS, HEAD_W), F32),
                   jax.ShapeDtypeStruct((n * N_HEADS, HEAD_W), F32)],
        compiler_params=_cparams(("arbitrary",)),
        name="memkv",
    )(mem2d, g.reshape(1, d), wk, wv, gk.reshape(1, HEAD_W))


def _mem_attn_kernel(q_ref, k_ref, v_ref, gq_ref, y_ref, *, sb, m):
    gq = gq_ref[...]
    scale = HEAD_W ** -0.5
    units = [(s, h) for s in range(sb) for h in range(N_HEADS)]
    qs = [q_ref[s].astype(F32) for s in range(sb)]
    qh = [(_rms(qs[s][:, h * HEAD_W:(h + 1) * HEAD_W], gq) * (scale * LOG2E)).astype(BF16) for s, h in units]
    sc = [_dot_nt(qh[u], k_ref[s, pl.ds(h, m, stride=N_HEADS), :].astype(BF16)) for u, (s, h) in enumerate(units)]
    ps = [jnp.exp2(x - jnp.max(x, axis=-1, keepdims=True)).astype(BF16) for x in sc]
    ones = jnp.ones((m, HEAD_W), BF16)
    for u, (s, h) in enumerate(units):
        v_ext = jnp.concatenate([v_ref[s, pl.ds(h, m, stride=N_HEADS), :].astype(BF16), ones], axis=1)
        o = _dot(ps[u], v_ext)
        y_ref[s, :, h * HEAD_W:(h + 1) * HEAD_W] = (o[:, 0:HEAD_W] / o[:, HEAD_W:2 * HEAD_W]).astype(BF16)


def _mem_attn(z3d, k4, v4, gq, layer, sb, tq):
    b, t, _ = z3d.shape
    m = k4.shape[2] // N_HEADS
    br = N_HEADS * HEAD_W
    return pl.pallas_call(
        functools.partial(_mem_attn_kernel, sb=sb, m=m),
        grid=(b // sb, t // tq),
        in_specs=[pl.BlockSpec((sb, tq, br), lambda i, j: (i, j, COL_MEMQ // br)),
                  pl.BlockSpec((None, sb, m * N_HEADS, HEAD_W), lambda i, j: (layer, i, 0, 0)),
                  pl.BlockSpec((None, sb, m * N_HEADS, HEAD_W), lambda i, j: (layer, i, 0, 0)),
                  pl.BlockSpec((1, HEAD_W), lambda i, j: (0, 0))],
        out_specs=pl.BlockSpec((sb, tq, br), lambda i, j: (i, j, 0)),
        out_shape=jax.ShapeDtypeStruct((b, t, br), BF16),
        compiler_params=_cparams(("arbitrary", "arbitrary")),
        name="mem_attn",
    )(z3d, k4, v4, gq.reshape(1, HEAD_W))


def _outproj_kernel(x_ref, mg_ref, sg_ref, y0_ref, y1_ref, y2_ref, y3_ref, wb_ref, wo_ref, o_ref):
    d = x_ref.shape[1]
    br = y0_ref.shape[1]
    merged = None
    for n, y_ref in enumerate((y0_ref, y1_ref, y2_ref, y3_ref)):
        sg = sg_ref[:, n * br:(n + 1) * br].astype(F32)
        ys = (y_ref[...].astype(F32) * (sg * (1.0 + jnp.tanh(sg)))).astype(BF16)
        proj = _dot(ys, wb_ref[n])
        term = (1.0 + jnp.tanh(mg_ref[:, n * d:(n + 1) * d].astype(F32))) * proj
        merged = term if merged is None else merged + term
    o_ref[...] = x_ref[...] + _dot(merged.astype(BF16), wo_ref[...])


def _outproj(x2d, z2d, ys, wb, wo, tm):
    n, d = x2d.shape
    br = ys[0].shape[1]
    nb = wb.shape[0]
    yspec = pl.BlockSpec((tm, br), lambda i: (i, 0))
    return pl.pallas_call(
        _outproj_kernel,
        grid=(n // tm,),
        in_specs=[pl.BlockSpec((tm, d), lambda i: (i, 0)),
                  pl.BlockSpec((tm, nb * d), lambda i: (i, COL_MG // (nb * d))),
                  pl.BlockSpec((tm, nb * br), lambda i: (i, COL_SG // (nb * br))),
                  yspec, yspec, yspec, yspec,
                  pl.BlockSpec((nb, br, d), lambda i: (0, 0, 0)),
                  pl.BlockSpec((d, d), lambda i: (0, 0))],
        out_specs=pl.BlockSpec((tm, d), lambda i: (i, 0)),
        out_shape=jax.ShapeDtypeStruct((n, d), F32),
        compiler_params=_cparams(("arbitrary",)),
        name="outproj",
    )(x2d, z2d, z2d, *ys, wb, wo)


def _rope_tables(pos):
    half = MLA_ROPE // 2
    freqs = ROPE_THETA ** (-np.arange(half, dtype=np.float64) / half)
    ang = np.asarray(pos, np.float64)[:, None] * freqs[None, :]
    cos = np.tile(np.cos(ang), (1, 8))
    sin = np.tile(np.sin(ang), (1, 8))
    sin[:, :64] *= -1.0
    return jnp.asarray(cos, F32), jnp.asarray(sin, F32)


def _mla_params(q_norm, w_uq, kv_norm, w_uk, gq, gk):
    half = MLA_ROPE // 2
    wq = w_uq.reshape(MLA_Q_RANK, N_HEADS, MLA_QK)
    wuq = jnp.concatenate([wq[:, :, :MLA_NOPE].reshape(MLA_Q_RANK, -1),
                           wq[:, :, MLA_NOPE:MLA_NOPE + half].reshape(MLA_Q_RANK, -1),
                           wq[:, :, MLA_NOPE + half:].reshape(MLA_Q_RANK, -1)], axis=1).astype(BF16)
    g2 = gq * gk
    wk = w_uk.reshape(MLA_KV_RANK, N_HEADS, MLA_NOPE)
    wabs = jnp.zeros((N_HEADS * MLA_NOPE, N_HEADS * MLA_KV_RANK), F32)
    for h in range(N_HEADS):
        blk = (wk[:, h, :] * g2[None, :MLA_NOPE]).T
        wabs = wabs.at[h * MLA_NOPE:(h + 1) * MLA_NOPE, h * MLA_KV_RANK:(h + 1) * MLA_KV_RANK].set(blk)
    grope = jnp.zeros((1, 128), F32).at[0, :MLA_ROPE].set(g2[MLA_NOPE:])
    j = np.arange(384)
    head_of = np.where(j < 256, j // MLA_NOPE, (j % 64) // half)
    indq = (head_of[:, None] == (np.arange(N_HEADS * 128) // 128)[None, :]).astype(np.float32)
    lane = j - 256
    is_rope = (j >= 256) & ((lane < half) | ((lane >= 64) & (lane < 64 + half)))
    indk = np.zeros((8, 384), np.float32)
    for h in range(N_HEADS):
        indk[h] = ((j < 256) & (j // MLA_NOPE == h)) | is_rope
    return dict(gq_norm=q_norm.reshape(1, -1), wuq=wuq, gkv_norm=kv_norm.reshape(1, -1),
                wuk=w_uk.astype(BF16), wukt=w_uk.T.astype(BF16), wabs=wabs.astype(BF16), grope=grope,
                indq=jnp.asarray(indq, BF16), indk=jnp.asarray(indk, BF16))


def _hgrn_lower_bounds(lb_param):
    p = jax.nn.softmax(lb_param.astype(F32), axis=0)
    return jnp.cumsum(p, axis=0) - p[0]


def kernel(x_prompt, x_sample, mem_prompt, cache_mla_latent, cache_mla_rope, page_table, state_hgrn, state_conv, cache_mem_k, cache_mem_v, norm_gain, w_in, conv_w, hgrn_lb, hgrn_norm, mla_q_norm, mla_w_uq, mla_kv_norm, mla_w_uk, mla_w_uv, mla_q_gain, mla_k_gain, mem_norm, mem_w_k, mem_w_v, mem_q_gain, mem_k_gain, w_branch_out, w_out):
    bp, tp, d = x_prompt.shape
    bs, ts, _ = x_sample.shape
    depth = w_in.shape[0]
    br = conv_w.shape[2]
    mem_len = mem_prompt.shape[1]
    n_pages = page_table.shape[1]
    past = n_pages * cache_mla_latent.shape[2]
    tpad = SAMPLE_PAD_T
    n_p = bp * tp
    n_s = bs * tpad

    lbs = _hgrn_lower_bounds(hgrn_lb)
    cos_p, sin_p = _rope_tables(np.arange(tp))
    cos_s, sin_s = _rope_tables(past + np.arange(tpad))
    tm_s = min(n_s, 1024)
    cos_s = jnp.tile(cos_s, (tm_s // tpad, 1))
    sin_s = jnp.tile(sin_s, (tm_s // tpad, 1))
    page_flat = page_table.reshape(-1).astype(jnp.int32)
    rope_t = jnp.swapaxes(cache_mla_rope, 2, 3)
    mem_k4 = cache_mem_k.reshape(depth, bs, mem_len * N_HEADS, HEAD_W)
    mem_v4 = cache_mem_v.reshape(depth, bs, mem_len * N_HEADS, HEAD_W)

    tm_p = min(tp, 512)
    xp = x_prompt.reshape(n_p, d)
    xs = jnp.pad(x_sample, ((0, 0), (0, tpad - ts), (0, 0))).reshape(n_s, d)
    mem2d = mem_prompt.reshape(bp * mem_len, d)
    zero_state = jnp.zeros((1, bp, N_HEADS, HEAD_W, HEAD_W), F32)

    outs = {k: [] for k in ("p_lat", "p_rope", "p_conv", "p_mk", "p_mv", "s_lat", "s_rope", "s_conv")}
    p_hg = s_hg = None
    for l in range(depth):
        w_in_t = _relayout_w_in(w_in[l])
        pw = _mla_params(mla_q_norm[l], mla_w_uq[l], mla_kv_norm[l], mla_w_uk[l], mla_q_gain[l], mla_k_gain[l])
        wuv = mla_w_uv[l].astype(BF16)
        wb = (0.5 * w_branch_out[l]).astype(BF16)
        wo = w_out[l].astype(BF16)

        z = _inproj(xp, norm_gain[l], w_in_t, min(n_p, INPROJ_ROWS), INPROJ_COLS)
        z3 = z.reshape(bp, tp, N_COLS)
        y_conv, tail = _conv(z, None, conv_w[l], tp, min(tp, CONV_TILE), 8)
        y_hg, p_hg = _hgrn(z3, lbs[l], hgrn_norm[l], zero_state, 0, l, 1, min(tp, HG_TILE), HG_CHUNK, None, depth,
                           p_hg)
        c, r, kv, rk, qp = _mla_prep(z, cos_p, sin_p, pw, min(tp, PREP_TILE))
        y_mla = _mla_prompt(qp, kv, rk, wuv, bp, tp, min(tp, MLA_TQ))
        mk, mv = _memkv(mem2d, mem_norm[l], mem_w_k[l].astype(BF16), mem_w_v[l].astype(BF16), mem_k_gain[l],
                        min(bp * mem_len, 512))
        rows_m = mem_len * N_HEADS
        y_mem = _mem_attn(z3, mk.reshape(1, bp, rows_m, HEAD_W), mv.reshape(1, bp, rows_m, HEAD_W), mem_q_gain[l],
                          0, 1, min(tp, MEM_TILE))
        xp = _outproj(xp, z, (y_conv, y_hg.reshape(n_p, br), y_mla, y_mem.reshape(n_p, br)), wb, wo, tm_p)
        outs["p_lat"].append(c.reshape(bp, tp, MLA_KV_RANK))
        outs["p_rope"].append(r.reshape(bp, tp, MLA_ROPE))
        outs["p_conv"].append(tail.reshape(bp, 8, br)[:, 8 - (CONV_K - 1):])
        outs["p_mk"].append(mk.reshape(bp, mem_len, N_HEADS, HEAD_W))
        outs["p_mv"].append(mv.reshape(bp, mem_len, N_HEADS, HEAD_W))

        z = _inproj(xs, norm_gain[l], w_in_t, tm_s, INPROJ_COLS)
        z3 = z.reshape(bs, tpad, N_COLS)
        hist = jnp.pad(state_conv[l], ((0, 0), (0, tpad - (CONV_K - 1)), (0, 0))).reshape(n_s, br)
        y_conv, u_all = _conv(z, hist, conv_w[l], tpad, tm_s, tm_s)
        sb = 8 if bs % 8 == 0 else 1
        sb_hg = HG_SAMPLE_SEQS if bs % HG_SAMPLE_SEQS == 0 else sb
        y_hg, s_hg = _hgrn(z3, lbs[l], hgrn_norm[l], state_hgrn, l, l, sb_hg, tpad, tpad, ts, depth, s_hg)
        c, r, kv, rk, qp = _mla_prep(z, cos_s, sin_s, pw, tm_s)
        qp3 = qp.reshape(N_HEADS, bs, tpad, 256)[:, :, :ts].transpose(1, 0, 2, 3).reshape(bs, N_HEADS * ts, 256)
        rt3 = jnp.pad(r.reshape(bs, tpad, MLA_ROPE).transpose(0, 2, 1), ((0, 0), (0, 0), (0, PAGE - tpad)))
        y_mla = _mla_sample(page_flat, qp3, c.reshape(bs, tpad, MLA_KV_RANK), rt3, pw["wukt"], wuv,
                            cache_mla_latent, rope_t, l, n_pages)
        y_mem = _mem_attn(z3, mem_k4, mem_v4, mem_q_gain[l], l, sb, tpad)
        xs = _outproj(xs, z, (y_conv, y_hg.reshape(n_s, br), y_mla.reshape(n_s, br), y_mem.reshape(n_s, br)),
                      wb, wo, tm_s)
        outs["s_lat"].append(c.reshape(bs, tpad, MLA_KV_RANK)[:, :ts])
        outs["s_rope"].append(r.reshape(bs, tpad, MLA_ROPE)[:, :ts])
        outs["s_conv"].append(u_all.reshape(bs, tpad, br)[:, ts - (CONV_K - 1):ts])

    st = lambda k: jnp.stack(outs[k])
    return (xp.reshape(bp, tp, d), xs.reshape(bs, tpad, d)[:, :ts], st("p_lat"), st("p_rope"), p_hg,
            st("p_conv"), st("p_mk"), st("p_mv"), st("s_lat"), st("s_rope"), s_hg, st("s_conv"))
```
